```python
import math
import jax, jax.numpy as jnp
from jax import lax
import numpy as np

D_MODEL = 2048
BATCH = 8
SEQ = 2048
DEPTH = 1

N_DIR = 2
N_BRANCH = 2
MLSTM_HEADS = 4
MLSTM_V_DIM = 256
MLSTM_QK_DIM = 128
MLSTM_W = MLSTM_HEADS * MLSTM_V_DIM
MLSTM_QK_W = MLSTM_HEADS * MLSTM_QK_DIM
MLSTM_CHUNK = 64
DELTA_HEADS = 8
DELTA_HEAD_DIM = 128
DELTA_W = DELTA_HEADS * DELTA_HEAD_DIM
DELTA_CONV = 5
DELTA_CHUNK = 64
D_FF = 4 * D_MODEL
RMS_EPS = 1e-6
L2_EPS = 1e-6
IN_SPLITS = (MLSTM_QK_W, MLSTM_QK_W, MLSTM_W, MLSTM_W,
             N_DIR * MLSTM_HEADS, N_DIR * MLSTM_HEADS,
             3 * DELTA_W, DELTA_W, N_DIR * DELTA_HEADS, N_DIR * DELTA_HEADS,
             N_BRANCH * D_MODEL)
D_IN = sum(IN_SPLITS)

kernel_name = "bidir_mlstm_gdn_hybrid_layer"


def _rms_norm(x, g):
    xf = x.astype(jnp.float32)
    y = xf * lax.rsqrt(jnp.mean(xf * xf, axis=-1, keepdims=True) + RMS_EPS)
    return (y * g.astype(jnp.float32)).astype(x.dtype)


def _heads(t, n):
    b, s, w = t.shape
    return t.reshape(b, s, n, w // n).transpose(0, 2, 1, 3)


def _merge_heads(t):
    b, n, s, d = t.shape
    return t.transpose(0, 2, 1, 3).reshape(b, s, n * d)


def _dir_heads(t, n):
    b, s, _ = t.shape
    return t.astype(jnp.float32).reshape(b, s, N_DIR, n).transpose(2, 0, 3, 1)


def _flip(t):
    return jnp.flip(t, axis=2)


def _to_chunks(t, size):
    s = t.shape[2]
    t = t.reshape(t.shape[:2] + (s // size, size) + t.shape[3:])
    return jnp.moveaxis(t, 2, 0)


def _from_chunks(t):
    t = jnp.moveaxis(t, 0, 2)
    return t.reshape(t.shape[:2] + (t.shape[2] * t.shape[3],) + t.shape[4:])


def _l2_normalize(t):
    return t * lax.rsqrt(jnp.sum(t * t, axis=-1, keepdims=True) + L2_EPS)


def _mlstm_chunk_scan(q, k, v, i_pre, logf):
    b_, h_, s_, dk = q.shape
    dv = v.shape[-1]
    L = MLSTM_CHUNK
    qc, kc, vc, ic, fc = (_to_chunks(t, L) for t in (q, k, v, i_pre, logf))
    causal = jnp.tril(jnp.ones((L, L), dtype=bool))

    def step(carry, inp):
        c_st, n_st, m_st = carry
        qj, kj, vj, ij, fj = inp
        bcum = jnp.cumsum(fj, axis=-1)
        log_intra = jnp.where(causal, bcum[..., :, None] - bcum[..., None, :] + ij[..., None, :], -jnp.inf)
        log_inter = bcum + m_st[..., None]
        m_row = jnp.maximum(log_inter, jnp.max(log_intra, axis=-1))
        w_intra = jnp.exp(log_intra - m_row[..., None])
        w_inter = jnp.exp(log_inter - m_row)
        scores = jnp.einsum('bhld,bhsd->bhls', qj, kj) * w_intra
        num = (jnp.einsum('bhls,bhsv->bhlv', scores, vj)
               + w_inter[..., None] * jnp.einsum('bhld,bhdv->bhlv', qj, c_st))
        den = jnp.sum(scores, axis=-1) + w_inter * jnp.einsum('bhld,bhd->bhl', qj, n_st)
        h_out = num / jnp.maximum(jnp.abs(den), jnp.exp(-m_row))[..., None]
        b_last = bcum[..., -1]
        log_state = b_last[..., None] - bcum + ij
        m_new = jnp.maximum(b_last + m_st, jnp.max(log_state, axis=-1))
        w_state = jnp.exp(log_state - m_new[..., None])
        decay = jnp.exp(b_last + m_st - m_new)
        c_new = decay[..., None, None] * c_st + jnp.einsum('bhs,bhsd,bhsv->bhdv', w_state, kj, vj)
        n_new = decay[..., None] * n_st + jnp.einsum('bhs,bhsd->bhd', w_state, kj)
        return (c_new, n_new, m_new), h_out

    carry0 = (jnp.zeros((b_, h_, dk, dv), q.dtype),
              jnp.zeros((b_, h_, dk), q.dtype),
              jnp.full((b_, h_), -jnp.inf, q.dtype))
    _, h_all = lax.scan(step, carry0, (qc, kc, vc, ic, fc))
    return _from_chunks(h_all)


def _gated_delta_chunk_scan(q, k, v, g, beta):
    b_, h_, s_, dk = q.shape
    dv = v.shape[-1]
    L = DELTA_CHUNK
    qc, kc, vc, gc, bc = (_to_chunks(t, L) for t in (q, k, v, g, beta))
    gc = jnp.cumsum(gc, axis=-1)
    lower = jnp.tril(jnp.ones((L, L), dtype=bool))
    strict = jnp.tril(jnp.ones((L, L), dtype=bool), -1)
    gamma = jnp.exp(jnp.where(lower, gc[..., :, None] - gc[..., None, :], -jnp.inf))
    kb = kc * bc[..., None]
    a_mat = jnp.where(strict, jnp.einsum('nbhid,nbhjd->nbhij', kb, kc) * gamma, 0.0) + jnp.eye(L, dtype=q.dtype)
    u = lax.linalg.triangular_solve(a_mat, vc * bc[..., None], left_side=True, lower=True, unit_diagonal=True)
    w = lax.linalg.triangular_solve(a_mat, kb * jnp.exp(gc)[..., None], left_side=True, lower=True, unit_diagonal=True)
    attn = jnp.einsum('nbhid,nbhjd->nbhij', qc, kc) * gamma
    qg = qc * jnp.exp(gc)[..., None]
    g_last = gc[..., -1]
    kd = kc * jnp.exp(g_last[..., None] - gc)[..., None]

    def step(state, inp):
        qg_c, kd_c, u_c, w_c, attn_c, gl = inp
        v_new = u_c - jnp.einsum('bhld,bhdv->bhlv', w_c, state)
        o = jnp.einsum('bhld,bhdv->bhlv', qg_c, state) + jnp.einsum('bhls,bhsv->bhlv', attn_c, v_new)
        state = jnp.exp(gl)[..., None, None] * state + jnp.einsum('bhld,bhlv->bhdv', kd_c, v_new)
        return state, o

    s0 = jnp.zeros((b_, h_, dk, dv), q.dtype)
    _, o_all = lax.scan(step, s0, (qg, kd, u, w, attn, g_last))
    return _from_chunks(o_all)


def _mlstm_mixer(q, k, v, o_pre, i_pre, f_pre, i_bias, f_bias, norm_g):
    f32 = jnp.float32
    qh = _heads(q, MLSTM_HEADS).astype(f32)
    kh = _heads(k, MLSTM_HEADS).astype(f32) * (MLSTM_QK_DIM ** -0.5)
    vh = _heads(v, MLSTM_HEADS).astype(f32)
    ig = _dir_heads(i_pre, MLSTM_HEADS) + i_bias.astype(f32)[:, None, :, None]
    lf = jax.nn.log_sigmoid(_dir_heads(f_pre, MLSTM_HEADS) + f_bias.astype(f32)[:, None, :, None])
    h_fwd = _mlstm_chunk_scan(qh, kh, vh, ig[0], lf[0])
    h_bwd = _flip(_mlstm_chunk_scan(_flip(qh), _flip(kh), _flip(vh), _flip(ig[1]), _flip(lf[1])))
    h = h_fwd + h_bwd
    h = h * lax.rsqrt(jnp.mean(h * h, axis=-1, keepdims=True) + RMS_EPS)
    h = h * norm_g.astype(f32).reshape(MLSTM_HEADS, 1, MLSTM_V_DIM)
    return (jax.nn.sigmoid(o_pre.astype(f32)) * _merge_heads(h)).astype(q.dtype)


def _gated_deltanet_mixer(qkv, z, a_pre, b_pre, conv_w, a_log, dt_bias, norm_g):
    f32 = jnp.float32
    ch = qkv.shape[-1]
    pad = DELTA_CONV // 2
    qkv = lax.conv_general_dilated(qkv, conv_w[:, None, :], window_strides=(1,), padding=[(pad, pad)],
                                   dimension_numbers=('NWC', 'WIO', 'NWC'), feature_group_count=ch)
    qkv = jax.nn.silu(qkv)
    q, k, v = jnp.split(qkv, 3, axis=-1)
    qh = _l2_normalize(_heads(q, DELTA_HEADS).astype(f32)) * (DELTA_HEAD_DIM ** -0.5)
    kh = _l2_normalize(_heads(k, DELTA_HEADS).astype(f32))
    vh = _heads(v, DELTA_HEADS).astype(f32)
    a = _dir_heads(a_pre, DELTA_HEADS) + dt_bias.astype(f32)[:, None, :, None]
    g = -jnp.exp(a_log.astype(f32))[:, None, :, None] * jax.nn.softplus(a)
    beta = jax.nn.sigmoid(_dir_heads(b_pre, DELTA_HEADS))
    o_fwd = _gated_delta_chunk_scan(qh, kh, vh, g[0], beta[0])
    o_bwd = _flip(_gated_delta_chunk_scan(_flip(qh), _flip(kh), _flip(vh), _flip(g[1]), _flip(beta[1])))
    o = o_fwd + o_bwd
    o = o * lax.rsqrt(jnp.mean(o * o, axis=-1, keepdims=True) + RMS_EPS) * norm_g.astype(f32)
    return (_merge_heads(o) * jax.nn.silu(z.astype(f32))).astype(z.dtype)


def setup_inputs(seed: int = 0) -> dict:
    key = jax.random.key(seed)
    ks = jax.random.split(key, 20)
    f32 = jnp.float32

    def dense(k, fan_in, fan_out):
        return jax.random.normal(k, (DEPTH, fan_in, fan_out), f32) * (fan_in ** -0.5)

    def gain(k, n):
        return 1.0 + 0.02 * jax.random.normal(k, (DEPTH, n), f32)

    dt = jnp.exp(jax.random.uniform(ks[8], (DEPTH, N_DIR, DELTA_HEADS), f32, math.log(1e-3), math.log(1e-1)))
    return {
        "x": jax.random.normal(ks[0], (BATCH, SEQ, D_MODEL), f32),
        "norm1_g": gain(ks[1], D_MODEL),
        "w_in": dense(ks[2], D_MODEL, D_IN),
        "mlstm_i_bias": 0.1 * jax.random.normal(ks[3], (DEPTH, N_DIR, MLSTM_HEADS), f32),
        "mlstm_f_bias": jax.random.uniform(ks[4], (DEPTH, N_DIR, MLSTM_HEADS), f32, 3.0, 6.0),
        "mlstm_norm_g": gain(ks[5], MLSTM_W),
        "delta_conv_w": jax.random.normal(ks[6], (DEPTH, DELTA_CONV, 3 * DELTA_W), f32) * (DELTA_CONV ** -0.5),
        "delta_a_log": jnp.log(jax.random.uniform(ks[7], (DEPTH, N_DIR, DELTA_HEADS), f32, 1.0, 16.0)),
        "delta_dt_bias": dt + jnp.log(-jnp.expm1(-dt)),
        "delta_norm_g": gain(ks[9], DELTA_HEAD_DIM),
        "w_branch_m": dense(ks[10], MLSTM_W, D_MODEL),
        "w_branch_d": dense(ks[11], DELTA_W, D_MODEL),
        "w_out": dense(ks[12], D_MODEL, D_MODEL),
        "norm2_g": gain(ks[13], D_MODEL),
        "w_ff1": dense(ks[14], D_MODEL, D_FF),
        "w_ff2": dense(ks[15], D_FF, D_MODEL),
        "norm_f_g": 1.0 + 0.02 * jax.random.normal(ks[16], (D_MODEL,), f32),
    }


def reference(x, norm1_g, w_in, mlstm_i_bias, mlstm_f_bias, mlstm_norm_g, delta_conv_w, delta_a_log,
              delta_dt_bias, delta_norm_g, w_branch_m, w_branch_d, w_out, norm2_g, w_ff1, w_ff2, norm_f_g):
    split_points = np.cumsum(IN_SPLITS)[:-1].tolist()
    for l in range(DEPTH):
        h = _rms_norm(x, norm1_g[l])
        proj = h @ w_in[l]
        (m_q, m_k, m_v, m_o, m_i, m_f, d_qkv, d_z, d_a, d_b, gates) = jnp.split(proj, split_points, axis=-1)
        y_m = _mlstm_mixer(m_q, m_k, m_v, m_o, m_i, m_f, mlstm_i_bias[l], mlstm_f_bias[l], mlstm_norm_g[l])
        y_d = _gated_deltanet_mixer(d_qkv, d_z, d_a, d_b, delta_conv_w[l], delta_a_log[l], delta_dt_bias[l],
                                    delta_norm_g[l])
        g_m, g_d = jnp.split(gates, 2, axis=-1)
        mixed = jax.nn.sigmoid(g_m) * (y_m @ w_branch_m[l]) + jax.nn.sigmoid(g_d) * (y_d @ w_branch_d[l])
        x = x + mixed @ w_out[l]
        h = _rms_norm(x, norm2_g[l])
        x = x + jnp.square(jax.nn.relu(h @ w_ff1[l])) @ w_ff2[l]
    return _rms_norm(x, norm_f_g)
```

```python
import functools

import jax
import jax.numpy as jnp
from jax import lax
from jax.experimental import pallas as pl
from jax.experimental.pallas import tpu as pltpu

F32 = jnp.float32
BF16 = jnp.bfloat16
I32 = jnp.int32

N_DIR = 2
M_HEADS = 4
M_DK = 128
M_DV = 256
D_HEADS = 8
D_DIM = 128
CONV_W = 5
RMS_EPS = 1e-6
L2_EPS = 1e-6

LANES = 128
M_CHUNK = 256
D_CHUNK = 64
D_GROUP = 256
D_PER_GROUP = D_GROUP // D_CHUNK
GATE_ROWS = 256
CONV_HALO = 8

CH_I = 0
CH_F = 8
CH_G = 16
CH_B = 32
CH_T = 48

VMEM_LIMIT = 56 * 1024 * 1024

_NT = (((1,), (1,)), ((), ()))


def _cparams(sem):
    return pltpu.CompilerParams(dimension_semantics=sem, vmem_limit_bytes=VMEM_LIMIT)


def _bdot(a, b):
    return jnp.dot(a.astype(BF16), b.astype(BF16), preferred_element_type=F32)


def _bdot_nt(a, b):
    return lax.dot_general(a.astype(BF16), b.astype(BF16), _NT, preferred_element_type=F32)


def _sigmoid(x):
    return 1.0 / (1.0 + jnp.exp(-x))


def _softplus(x):
    return jnp.maximum(x, 0.0) + jnp.log1p(jnp.exp(-jnp.abs(x)))


def _lane_pick(x, ch):
    lane = lax.broadcasted_iota(I32, x.shape, 1)
    return jnp.sum(jnp.where(lane == ch, x, 0.0), axis=1, keepdims=True)


def _inproj_kernel(x_ref, g_ref, w_ref, ws_ref, o_ref, os_ref, hn_ref):
    @pl.when(pl.program_id(1) == 0)
    def _():
        x = x_ref[...]
        ms = jnp.mean(x * x, axis=-1, keepdims=True)
        hn = (x * lax.rsqrt(ms + RMS_EPS) * g_ref[...]).astype(BF16)
        hn_ref[...] = hn
        os_ref[...] = jnp.dot(hn, ws_ref[...], preferred_element_type=F32)

    o_ref[...] = jnp.dot(hn_ref[...], w_ref[...], preferred_element_type=F32)


def _inproj(x2, g, w_main, w_small, tm, tn):
    m, d = x2.shape
    n = w_main.shape[1]
    return pl.pallas_call(
        _inproj_kernel,
        grid=(m // tm, n // tn),
        in_specs=[
            pl.BlockSpec((tm, d), lambda i, j: (i, 0)),
            pl.BlockSpec((1, d), lambda i, j: (0, 0)),
            pl.BlockSpec((d, tn), lambda i, j: (0, j)),
            pl.BlockSpec((d, LANES), lambda i, j: (0, 0)),
        ],
        out_specs=[
            pl.BlockSpec((tm, tn), lambda i, j: (i, j)),
            pl.BlockSpec((tm, LANES), lambda i, j: (i, 0)),
        ],
        out_shape=[jax.ShapeDtypeStruct((m, n), F32), jax.ShapeDtypeStruct((m, LANES), F32)],
        scratch_shapes=[pltpu.VMEM((tm, d), BF16)],
        compiler_params=_cparams(("parallel", "arbitrary")),
        name="inproj",
    )(x2, g, w_main, w_small)


def _gate_kernel(sm_ref, par_ref, col_ref, row_ref):
    r = GATE_ROWS
    x = sm_ref[0] + par_ref[0:1, :]
    lane = lax.broadcasted_iota(I32, x.shape, 1)
    logf = -_softplus(-x)
    g = -jnp.exp(par_ref[1:2, :]) * _softplus(x)
    beta = _sigmoid(x)
    xe = jnp.where(lane < CH_F, x,
                   jnp.where(lane < CH_G, logf,
                             jnp.where(lane < CH_B, g,
                                       jnp.where(lane < CH_T, beta, 0.0))))
    hi = xe.astype(BF16)
    r1 = xe - hi.astype(F32)
    mid = r1.astype(BF16)
    lo = (r1 - mid.astype(F32)).astype(BF16)
    x3 = jnp.concatenate([hi, mid, lo], axis=1)
    rr = lax.broadcasted_iota(I32, (r, r), 0)
    cc = lax.broadcasted_iota(I32, (r, r), 1)
    low = jnp.where(cc <= rr, 1.0, 0.0)
    upp = jnp.where(cc >= rr, 1.0, 0.0)
    same_d = jnp.where((rr // D_CHUNK) == (cc // D_CHUNK), 1.0, 0.0)
    same_m = jnp.where((rr // M_CHUNK) == (cc // M_CHUNK), 1.0, 0.0)
    mats = jnp.concatenate([
        low * same_m,
        upp * same_m,
        low * same_d,
        upp * same_d,
        same_d,
    ], axis=0).astype(BF16)
    y3 = jnp.dot(mats, x3, preferred_element_type=F32)
    y = y3[:, 0:LANES] + y3[:, LANES:2 * LANES] + y3[:, 2 * LANES:3 * LANES]
    pm, sm_, pd, sd, td = (y[i * r:(i + 1) * r] for i in range(5))
    half_m = CH_F + M_HEADS
    half_d = CH_G + D_HEADS
    out = jnp.where(lane < CH_F, xe,
          jnp.where(lane < half_m, pm,
          jnp.where(lane < CH_G, sm_,
          jnp.where(lane < half_d, pd,
          jnp.where(lane < CH_B, sd,
          jnp.where(lane < CH_T, xe,
          jnp.where(lane < CH_T + N_DIR * D_HEADS, pltpu.roll(td, CH_T - CH_G, axis=1), 0.0)))))))
    col_ref[0] = out
    row_ref[0] = out.T


def _gate_prep(smalls3, params):
    b, s, _ = smalls3.shape
    r = GATE_ROWS
    return pl.pallas_call(
        _gate_kernel,
        grid=(b, s // r),
        in_specs=[
            pl.BlockSpec((1, r, LANES), lambda i, j: (i, j, 0)),
            pl.BlockSpec((8, LANES), lambda i, j: (0, 0)),
        ],
        out_specs=[
            pl.BlockSpec((1, r, LANES), lambda i, j: (i, j, 0)),
            pl.BlockSpec((1, LANES, r), lambda i, j: (i, 0, j)),
        ],
        out_shape=[jax.ShapeDtypeStruct((b, s, LANES), F32), jax.ShapeDtypeStruct((b, LANES, s), F32)],
        compiler_params=_cparams(("parallel", "parallel")),
        name="gateprep",
    )(smalls3, params)


def _mlstm_kernel(q_ref, k_ref, v_ref, o_ref, gc_ref, gr_ref, ng_ref, y_ref,
                  hf_ref, hb_ref, c_ref, n_ref):
    head = pl.program_id(1)
    s = q_ref.shape[1]
    L = M_CHUNK
    nc = s // L
    scale = M_DK ** -0.5
    rr = lax.broadcasted_iota(I32, (L, L), 0)
    cc = lax.broadcasted_iota(I32, (L, L), 1)
    hbufs = (hf_ref, hb_ref)

    c_ref[...] = jnp.zeros(c_ref.shape, F32)
    n_ref[...] = jnp.zeros(n_ref.shape, F32)

    def chunk(d, c, m_st):
        r0 = pl.multiple_of(c * L, L)
        q = q_ref[0, pl.ds(r0, L), :]
        k = k_ref[0, pl.ds(r0, L), :] * scale
        v = v_ref[0, pl.ds(r0, L), :]
        qb, kb, vb = q.astype(BF16), k.astype(BF16), v.astype(BF16)
        ch_i = CH_I + d * M_HEADS + head
        ch_f = CH_F + d * M_HEADS + head
        bc_col = _lane_pick(gc_ref[0, pl.ds(r0, L), :], ch_f)
        i_row = gr_ref[0, pl.ds(ch_i, 1), pl.ds(r0, L)]
        bc_row = gr_ref[0, pl.ds(ch_f, 1), pl.ds(r0, L)]
        a_row = i_row - bc_row
        mask = (cc <= rr) if d == 0 else (cc >= rr)
        log_intra = jnp.where(mask, bc_col + a_row, -jnp.inf)
        log_inter = bc_col + m_st
        m_row = jnp.maximum(log_inter, jnp.max(log_intra, axis=1, keepdims=True))
        w_intra = jnp.exp(log_intra - m_row)
        w_inter = jnp.exp(log_inter - m_row)
        scores = lax.dot_general(qb, kb, _NT, preferred_element_type=F32) * w_intra
        c_st = c_ref[d]
        n_st = n_ref[d, 0:1, :]
        num = _bdot(scores, vb) + w_inter * _bdot(qb, c_st)
        den = (jnp.sum(scores, axis=1, keepdims=True)
               + w_inter * jnp.sum(q * n_st, axis=1, keepdims=True))
        hbufs[d][pl.ds(r0, L), :] = num / jnp.maximum(jnp.abs(den), jnp.exp(-m_row))
        b_last = bc_row[:, L - 1:L] if d == 0 else bc_row[:, 0:1]
        log_state = b_last + a_row
        m_new = jnp.maximum(b_last + m_st, jnp.max(log_state, axis=1, keepdims=True))
        w_state = jnp.exp(log_state - m_new)
        decay = jnp.exp(b_last + m_st - m_new)
        kw_t = (k.T * w_state).astype(BF16)
        c_ref[d] = decay * c_st + jnp.dot(kw_t, vb, preferred_element_type=F32)
        w8 = jnp.broadcast_to(w_state, (8, L)).astype(BF16)
        n_ref[d] = decay * n_ref[d] + jnp.dot(w8, kb, preferred_element_type=F32)
        return m_new

    def body(j, carry):
        m_f, m_b = carry
        return chunk(0, j, m_f), chunk(1, nc - 1 - j, m_b)

    m0 = jnp.full((1, 1), -jnp.inf, F32)
    lax.fori_loop(0, nc, body, (m0, m0))

    def finish(c, _):
        r0 = pl.multiple_of(c * L, L)
        hh = hf_ref[pl.ds(r0, L), :] + hb_ref[pl.ds(r0, L), :]
        hh = hh * lax.rsqrt(jnp.mean(hh * hh, axis=-1, keepdims=True) + RMS_EPS)
        hh = hh * ng_ref[0]
        y_ref[0, pl.ds(r0, L), :] = (_sigmoid(o_ref[0, pl.ds(r0, L), :]) * hh).astype(y_ref.dtype)
        return 0

    lax.fori_loop(0, nc, finish, 0)


def _mlstm(proj3, gcol, grow, norm_g, off_q, off_k, off_v, off_o):
    b, s, _ = proj3.shape
    bq, bk = off_q // M_DK, off_k // M_DK
    bv, bo = off_v // M_DV, off_o // M_DV
    return pl.pallas_call(
        _mlstm_kernel,
        grid=(b, M_HEADS),
        in_specs=[
            pl.BlockSpec((1, s, M_DK), lambda i, h: (i, 0, bq + h)),
            pl.BlockSpec((1, s, M_DK), lambda i, h: (i, 0, bk + h)),
            pl.BlockSpec((1, s, M_DV), lambda i, h: (i, 0, bv + h)),
            pl.BlockSpec((1, s, M_DV), lambda i, h: (i, 0, bo + h)),
            pl.BlockSpec((1, s, LANES), lambda i, h: (i, 0, 0)),
            pl.BlockSpec((1, LANES, s), lambda i, h: (i, 0, 0)),
            pl.BlockSpec((1, 1, M_DV), lambda i, h: (h, 0, 0)),
        ],
        out_specs=pl.BlockSpec((1, s, M_DV), lambda i, h: (i, 0, h)),
        out_shape=jax.ShapeDtypeStruct((b, s, M_HEADS * M_DV), BF16),
        scratch_shapes=[
            pltpu.VMEM((s, M_DV), F32),
            pltpu.VMEM((s, M_DV), F32),
            pltpu.VMEM((N_DIR, M_DK, M_DV), F32),
            pltpu.VMEM((N_DIR, 8, M_DK), F32),
        ],
        compiler_params=_cparams(("parallel", "arbitrary")),
        name="mlstm",
    )(proj3, proj3, proj3, proj3, gcol, grow, norm_g)


def _delta_kernel(q_ref, k_ref, v_ref, z_ref, cwq_ref, cwk_ref, cwv_ref, gc_ref, gr_ref, ng_ref, y_ref,
                  xp_ref, qs_ref, ks_ref, vs_ref, u_ref, w_ref, qg_ref, kdt_ref, at_ref, dec_ref,
                  o_ref, st_ref):
    head = pl.program_id(1)
    s = q_ref.shape[1]
    L = D_CHUNK
    G = D_GROUP
    nc = s // L
    ng = s // G
    halo = CONV_HALO
    pad = CONV_W // 2

    xp_ref[0:halo, :] = jnp.zeros((halo, D_DIM), F32)
    xp_ref[s + halo:s + 2 * halo, :] = jnp.zeros((halo, D_DIM), F32)

    def conv_into(x_ref, cw_ref, dst_ref, l2, mult):
        xp_ref[halo:s + halo, :] = x_ref[0]
        for blk in range(ng):
            base = halo + blk * G - pad
            acc = xp_ref[base:base + G, :] * cw_ref[0:1, :]
            for j in range(1, CONV_W):
                acc = acc + xp_ref[base + j:base + j + G, :] * cw_ref[j:j + 1, :]
            y = acc * _sigmoid(acc)
            if l2:
                y = y * lax.rsqrt(jnp.sum(y * y, axis=-1, keepdims=True) + L2_EPS)
            if mult != 1.0:
                y = y * mult
            dst_ref[blk * G:(blk + 1) * G, :] = y

    conv_into(q_ref, cwq_ref, qs_ref, True, D_DIM ** -0.5)
    conv_into(k_ref, cwk_ref, ks_ref, True, 1.0)
    conv_into(v_ref, cwv_ref, vs_ref, False, 1.0)

    rr = lax.broadcasted_iota(I32, (G, G), 0)
    cc = lax.broadcasted_iota(I32, (G, G), 1)
    same = (rr // L) == (cc // L)
    pr = lax.broadcasted_iota(I32, (L, G), 0)
    pc = lax.broadcasted_iota(I32, (L, G), 1)
    eye_p = jnp.where((pc % L) == pr, 1.0, 0.0).astype(F32)

    def pack(m):
        out = m[0:L]
        for i in range(1, D_PER_GROUP):
            out = out + m[i * L:(i + 1) * L]
        return out

    def unpack(p):
        return jnp.where(same, jnp.concatenate([p] * D_PER_GROUP, axis=0), 0.0)

    def group(gi, _):
        r0 = pl.multiple_of(gi * G, G)
        kk_ = ks_ref[pl.ds(r0, G), :]
        qq_ = qs_ref[pl.ds(r0, G), :]
        vv_ = vs_ref[pl.ds(r0, G), :]
        kb = kk_.astype(BF16)
        kk = lax.dot_general(kb, kb, _NT, preferred_element_type=F32)
        qk = lax.dot_general(qq_.astype(BF16), kb, _NT, preferred_element_type=F32)
        gcol = gc_ref[0, pl.ds(r0, G), :]
        for d in range(N_DIR):
            ch = d * D_HEADS + head
            g_col = _lane_pick(gcol, CH_G + ch)
            b_col = _lane_pick(gcol, CH_B + ch)
            t_col = _lane_pick(gcol, CH_T + ch)
            g_row = gr_ref[0, pl.ds(CH_G + ch, 1), pl.ds(r0, G)]
            tri = (cc <= rr) if d == 0 else (cc >= rr)
            strict = (cc < rr) if d == 0 else (cc > rr)
            gam = jnp.exp(jnp.where(tri, jnp.where(same, g_col - g_row, -jnp.inf), -jnp.inf))
            x = jnp.where(strict, -(b_col * kk * gam), 0.0)
            attn = qk * gam
            xp = pack(x)
            tp = eye_p + xp
            p = _bdot(xp, x)
            for _it in range(4):
                res = _bdot(jnp.concatenate([tp, p], axis=0), unpack(p))
                tp = tp + res[0:L]
                p = res[L:2 * L]
            tp = tp + _bdot(tp, unpack(p))
            eg = jnp.exp(g_col)
            rhs = jnp.concatenate([b_col * vv_, (b_col * eg) * kk_], axis=1)
            uw = _bdot(unpack(tp), rhs)
            u_ref[d, pl.ds(r0, G), :] = uw[:, 0:D_DIM]
            w_ref[d, pl.ds(r0, G), :] = uw[:, D_DIM:2 * D_DIM]
            qg_ref[d, pl.ds(r0, G), :] = qq_ * eg
            kdt = (kk_ * jnp.exp(t_col - g_col)).T
            dec = jnp.exp(t_col)
            for ci in range(D_PER_GROUP):
                idx = d * nc + gi * D_PER_GROUP + ci
                kdt_ref[idx] = kdt[:, ci * L:(ci + 1) * L]
                at_ref[idx] = attn[ci * L:(ci + 1) * L, ci * L:(ci + 1) * L]
                dec_ref[idx] = jnp.broadcast_to(dec[ci * L:ci * L + 8], (8, D_DIM))
        return 0

    lax.fori_loop(0, ng, group, 0)

    st_ref[...] = jnp.zeros(st_ref.shape, F32)

    def step(j, _):
        for d in range(N_DIR):
            c = j if d == 0 else nc - 1 - j
            r0 = pl.multiple_of(c * L, L)
            idx = d * nc + c
            st = st_ref[d]
            wq = jnp.concatenate([w_ref[d, pl.ds(r0, L), :], qg_ref[d, pl.ds(r0, L), :]], axis=0)
            res = _bdot(wq, st)
            v_new = (u_ref[d, pl.ds(r0, L), :] - res[0:L]).astype(BF16)
            o_ref[d, pl.ds(r0, L), :] = res[L:2 * L] + jnp.dot(
                at_ref[idx].astype(BF16), v_new, preferred_element_type=F32)
            st_ref[d] = dec_ref[idx][0:1, :] * st + jnp.dot(
                kdt_ref[idx].astype(BF16), v_new, preferred_element_type=F32)
        return 0

    lax.fori_loop(0, nc, step, 0)

    def finish(gi, _):
        r0 = pl.multiple_of(gi * G, G)
        o = o_ref[0, pl.ds(r0, G), :] + o_ref[1, pl.ds(r0, G), :]
        o = o * lax.rsqrt(jnp.mean(o * o, axis=-1, keepdims=True) + RMS_EPS) * ng_ref[...]
        z = z_ref[0, pl.ds(r0, G), :]
        y_ref[0, pl.ds(r0, G), :] = (o * (z * _sigmoid(z))).astype(y_ref.dtype)
        return 0

    lax.fori_loop(0, ng, finish, 0)


def _delta(proj3, conv_w, gcol, grow, norm_g, off_q, off_k, off_v, off_z):
    b, s, _ = proj3.shape
    w = D_HEADS * D_DIM
    bq, bk, bv, bz = (o // D_DIM for o in (off_q, off_k, off_v, off_z))
    nc = s // D_CHUNK
    return pl.pallas_call(
        _delta_kernel,
        grid=(b, D_HEADS),
        in_specs=[
            pl.BlockSpec((1, s, D_DIM), lambda i, h: (i, 0, bq + h)),
            pl.BlockSpec((1, s, D_DIM), lambda i, h: (i, 0, bk + h)),
            pl.BlockSpec((1, s, D_DIM), lambda i, h: (i, 0, bv + h)),
            pl.BlockSpec((1, s, D_DIM), lambda i, h: (i, 0, bz + h)),
            pl.BlockSpec((CONV_W, D_DIM), lambda i, h: (0, h)),
            pl.BlockSpec((CONV_W, D_DIM), lambda i, h: (0, D_HEADS + h)),
            pl.BlockSpec((CONV_W, D_DIM), lambda i, h: (0, 2 * D_HEADS + h)),
            pl.BlockSpec((1, s, LANES), lambda i, h: (i, 0, 0)),
            pl.BlockSpec((1, LANES, s), lambda i, h: (i, 0, 0)),
            pl.BlockSpec((1, D_DIM), lambda i, h: (0, 0)),
        ],
        out_specs=pl.BlockSpec((1, s, D_DIM), lambda i, h: (i, 0, h)),
        out_shape=jax.ShapeDtypeStruct((b, s, w), BF16),
        scratch_shapes=[
            pltpu.VMEM((s + 2 * CONV_HALO, D_DIM), F32),
            pltpu.VMEM((s, D_DIM), F32),
            pltpu.VMEM((s, D_DIM), F32),
            pltpu.VMEM((s, D_DIM), F32),
            pltpu.VMEM((N_DIR, s, D_DIM), F32),
            pltpu.VMEM((N_DIR, s, D_DIM), F32),
            pltpu.VMEM((N_DIR, s, D_DIM), F32),
            pltpu.VMEM((N_DIR * nc, D_DIM, D_CHUNK), F32),
            pltpu.VMEM((N_DIR * nc, D_CHUNK, D_CHUNK), F32),
            pltpu.VMEM((N_DIR * nc, 8, D_DIM), F32),
            pltpu.VMEM((N_DIR, s, D_DIM), F32),
            pltpu.VMEM((N_DIR, D_DIM, D_DIM), F32),
        ],
        compiler_params=_cparams(("parallel", "arbitrary")),
        name="delta",
    )(proj3, proj3, proj3, proj3, conv_w, conv_w, conv_w, gcol, grow, norm_g)


def _mix_kernel(x_ref, ym_ref, yd_ref, gm_ref, gd_ref, wbm_ref, wbd_ref, wo_ref, o_ref):
    a = jnp.dot(ym_ref[...], wbm_ref[...], preferred_element_type=F32)
    b = jnp.dot(yd_ref[...], wbd_ref[...], preferred_element_type=F32)
    mixed = _sigmoid(gm_ref[...]) * a + _sigmoid(gd_ref[...]) * b
    o_ref[...] = x_ref[...] + jnp.dot(mixed.astype(BF16), wo_ref[...], preferred_element_type=F32)


def _mix(x2, ym2, yd2, proj2, wbm, wbd, wo, tm):
    m, d = x2.shape
    const = lambda i: (0, 0)
    return pl.pallas_call(
        _mix_kernel,
        grid=(m // tm,),
        in_specs=[
            pl.BlockSpec((tm, d), lambda i: (i, 0)),
            pl.BlockSpec((tm, ym2.shape[1]), lambda i: (i, 0)),
            pl.BlockSpec((tm, yd2.shape[1]), lambda i: (i, 0)),
            pl.BlockSpec((tm, d), lambda i: (i, 0)),
            pl.BlockSpec((tm, d), lambda i: (i, 1)),
            pl.BlockSpec(wbm.shape, const),
            pl.BlockSpec(wbd.shape, const),
            pl.BlockSpec(wo.shape, const),
        ],
        out_specs=pl.BlockSpec((tm, d), lambda i: (i, 0)),
        out_shape=jax.ShapeDtypeStruct((m, d), F32),
        compiler_params=_cparams(("parallel",)),
        name="mix",
    )(x2, ym2, yd2, proj2, proj2, wbm, wbd, wo)


def _ffn_kernel(x_ref, g2_ref, gf_ref, w1_ref, w2_ref, o_ref, h_ref, acc_ref):
    f = pl.program_id(1)

    @pl.when(f == 0)
    def _():
        x = x_ref[...]
        ms = jnp.mean(x * x, axis=-1, keepdims=True)
        h_ref[...] = (x * lax.rsqrt(ms + RMS_EPS) * g2_ref[...]).astype(BF16)
        acc_ref[...] = jnp.zeros(acc_ref.shape, F32)

    hid = jnp.dot(h_ref[...], w1_ref[...], preferred_element_type=F32)
    act = jnp.square(jnp.maximum(hid, 0.0)).astype(BF16)
    acc_ref[...] += jnp.dot(act, w2_ref[...], preferred_element_type=F32)

    @pl.when(f == pl.num_programs(1) - 1)
    def _():
        x = x_ref[...] + acc_ref[...]
        ms = jnp.mean(x * x, axis=-1, keepdims=True)
        o_ref[...] = x * lax.rsqrt(ms + RMS_EPS) * gf_ref[...]


def _ffn(x2, g2, gf, w1, w2, tm, tf):
    m, d = x2.shape
    dff = w1.shape[1]
    return pl.pallas_call(
        _ffn_kernel,
        grid=(m // tm, dff // tf),
        in_specs=[
            pl.BlockSpec((tm, d), lambda i, f: (i, 0)),
            pl.BlockSpec((1, d), lambda i, f: (0, 0)),
            pl.BlockSpec((1, d), lambda i, f: (0, 0)),
            pl.BlockSpec((d, tf), lambda i, f: (0, f)),
            pl.BlockSpec((tf, d), lambda i, f: (f, 0)),
        ],
        out_specs=pl.BlockSpec((tm, d), lambda i, f: (i, 0)),
        out_shape=jax.ShapeDtypeStruct((m, d), F32),
        scratch_shapes=[pltpu.VMEM((tm, d), BF16), pltpu.VMEM((tm, d), F32)],
        compiler_params=_cparams(("parallel", "arbitrary")),
        name="ffn",
    )(x2, g2, gf, w1, w2)


def _pick_tile(n, candidates):
    for c in candidates:
        if n % c == 0:
            return c
    raise ValueError(f"no tile in {candidates} divides {n}")


def _layer(x, norm1_g, w_in, i_bias, f_bias, m_norm_g, conv_w, a_log, dt_bias, d_norm_g,
           w_bm, w_bd, w_out, norm2_g, w_ff1, w_ff2, out_g, apply_out_norm_g):
    b, s, d = x.shape
    m = b * s
    mw, mqk, dw = M_HEADS * M_DV, M_HEADS * M_DK, D_HEADS * D_DIM
    nm, nd = N_DIR * M_HEADS, N_DIR * D_HEADS
    splits = (mqk, mqk, mw, mw, nm, nm, 3 * dw, dw, nd, nd, 2 * d)
    bounds = [0]
    for w_ in splits:
        bounds.append(bounds[-1] + w_)
    cols = [w_in[:, bounds[i]:bounds[i + 1]] for i in range(len(splits))]
    (c_mq, c_mk, c_mv, c_mo, c_mi, c_mf, c_dqkv, c_dz, c_da, c_db, c_gates) = cols
    w_main = jnp.concatenate([c_gates, c_mq, c_mk, c_mv, c_mo, c_dqkv, c_dz], axis=1).astype(BF16)
    off_mq = 2 * d
    off_mk = off_mq + mqk
    off_mv = off_mk + mqk
    off_mo = off_mv + mw
    off_dq = off_mo + mw
    off_dk = off_dq + dw
    off_dv = off_dk + dw
    off_dz = off_dv + dw
    n_small = 2 * nm + 2 * nd
    w_small = jnp.concatenate([c_mi, c_mf, c_da, c_db, jnp.zeros((d, LANES - n_small), F32)], axis=1).astype(BF16)
    zpad = lambda n: jnp.zeros((n,), F32)
    bias_row = jnp.concatenate([i_bias.reshape(-1), f_bias.reshape(-1), dt_bias.reshape(-1), zpad(LANES - CH_B)])
    alog_row = jnp.concatenate([zpad(CH_G), a_log.reshape(-1), zpad(LANES - CH_B)])
    gate_params = jnp.concatenate([bias_row[None], alog_row[None], jnp.zeros((6, LANES), F32)], axis=0)

    x2 = x.reshape(m, d)
    tm = _pick_tile(m, (512, 256, 128))
    tn = _pick_tile(w_main.shape[1], (1024, 512, 256))
    proj2, smalls2 = _inproj(x2, norm1_g.reshape(1, d), w_main, w_small, tm, tn)
    proj3 = proj2.reshape(b, s, -1)
    gcol, grow = _gate_prep(smalls2.reshape(b, s, LANES), gate_params)

    ym = _mlstm(proj3, gcol, grow, m_norm_g.reshape(M_HEADS, 1, M_DV), off_mq, off_mk, off_mv, off_mo)
    yd = _delta(proj3, conv_w, gcol, grow, d_norm_g.reshape(1, D_DIM), off_dq, off_dk, off_dv, off_dz)

    x1 = _mix(x2, ym.reshape(m, mw), yd.reshape(m, dw), proj2,
              w_bm.astype(BF16), w_bd.astype(BF16), w_out.astype(BF16), _pick_tile(m, (256, 128)))
    tf = _pick_tile(w_ff1.shape[1], (1024, 512, 256))
    out = _ffn(x1, norm2_g.reshape(1, d), out_g.reshape(1, d), w_ff1.astype(BF16), w_ff2.astype(BF16), tm, tf)
    return out.reshape(b, s, d)


def kernel(x, norm1_g, w_in, mlstm_i_bias, mlstm_f_bias, mlstm_norm_g, delta_conv_w, delta_a_log,
           delta_dt_bias, delta_norm_g, w_branch_m, w_branch_d, w_out, norm2_g, w_ff1, w_ff2, norm_f_g):
    depth = w_in.shape[0]
    assert depth == 1, "the fused FFN + final-norm epilogue assumes a single layer"
    return _layer(x, norm1_g[0], w_in[0], mlstm_i_bias[0], mlstm_f_bias[0], mlstm_norm_g[0], delta_conv_w[0],
                  delta_a_log[0], delta_dt_bias[0], delta_norm_g[0], w_branch_m[0], w_branch_d[0], w_out[0],
                  norm2_g[0], w_ff1[0], w_ff2[0], norm_f_g, True)
```

```python
import functools

import jax
import jax.numpy as jnp
from jax import lax
from jax.experimental import pallas as pl
from jax.experimental.pallas import tpu as pltpu

F32 = jnp.float32
BF16 = jnp.bfloat16
I32 = jnp.int32

N_DIR = 2
M_HEADS = 4
M_DK = 128
M_DV = 256
D_HEADS = 8
D_DIM = 128
CONV_W = 5
RMS_EPS = 1e-6
L2_EPS = 1e-6

LANES = 128
M_CHUNK = 256
D_CHUNK = 64
D_GROUP = 256
D_PER_GROUP = D_GROUP // D_CHUNK
D_GROUPS_PER_ITER = 4
GATE_ROWS = 256
CONV_HALO = 8

CH_I = 0
CH_F = 8
CH_G = 16
CH_B = 32
CH_T = 48

VMEM_LIMIT = 56 * 1024 * 1024

_NT = (((1,), (1,)), ((), ()))


def _cparams(sem):
    return pltpu.CompilerParams(dimension_semantics=sem, vmem_limit_bytes=VMEM_LIMIT)


def _bdot(a, b):
    return jnp.dot(a.astype(BF16), b.astype(BF16), preferred_element_type=F32)


def _bdot_nt(a, b):
    return lax.dot_general(a.astype(BF16), b.astype(BF16), _NT, preferred_element_type=F32)


def _sigmoid(x):
    return 1.0 / (1.0 + jnp.exp(-x))


def _softplus(x):
    return jnp.maximum(x, 0.0) + jnp.log1p(jnp.exp(-jnp.abs(x)))


def _lane_pick(x, ch):
    lane = lax.broadcasted_iota(I32, x.shape, 1)
    return jnp.sum(jnp.where(lane == ch, x, 0.0), axis=1, keepdims=True)


def _inproj_kernel(x_ref, g_ref, w_ref, ws_ref, o_ref, os_ref, hn_ref):
    @pl.when(pl.program_id(1) == 0)
    def _():
        x = x_ref[...]
        ms = jnp.mean(x * x, axis=-1, keepdims=True)
        hn = (x * lax.rsqrt(ms + RMS_EPS) * g_ref[...]).astype(BF16)
        hn_ref[...] = hn
        os_ref[...] = jnp.dot(hn, ws_ref[...], preferred_element_type=F32)

    o_ref[...] = jnp.dot(hn_ref[...], w_ref[...], preferred_element_type=F32)


def _inproj(x2, g, w_main, w_small, tm, tn):
    m, d = x2.shape
    n = w_main.shape[1]
    return pl.pallas_call(
        _inproj_kernel,
        grid=(m // tm, n // tn),
        in_specs=[
            pl.BlockSpec((tm, d), lambda i, j: (i, 0)),
            pl.BlockSpec((1, d), lambda i, j: (0, 0)),
            pl.BlockSpec((d, tn), lambda i, j: (0, j)),
            pl.BlockSpec((d, LANES), lambda i, j: (0, 0)),
        ],
        out_specs=[
            pl.BlockSpec((tm, tn), lambda i, j: (i, j)),
            pl.BlockSpec((tm, LANES), lambda i, j: (i, 0)),
        ],
        out_shape=[jax.ShapeDtypeStruct((m, n), F32), jax.ShapeDtypeStruct((m, LANES), F32)],
        scratch_shapes=[pltpu.VMEM((tm, d), BF16)],
        compiler_params=_cparams(("parallel", "arbitrary")),
        name="inproj",
    )(x2, g, w_main, w_small)


def _gate_kernel(sm_ref, par_ref, col_ref, row_ref):
    r = GATE_ROWS
    x = sm_ref[0] + par_ref[0:1, :]
    lane = lax.broadcasted_iota(I32, x.shape, 1)
    logf = -_softplus(-x)
    g = -jnp.exp(par_ref[1:2, :]) * _softplus(x)
    beta = _sigmoid(x)
    xe = jnp.where(lane < CH_F, x,
                   jnp.where(lane < CH_G, logf,
                             jnp.where(lane < CH_B, g,
                                       jnp.where(lane < CH_T, beta, 0.0))))
    hi = xe.astype(BF16)
    r1 = xe - hi.astype(F32)
    mid = r1.astype(BF16)
    lo = (r1 - mid.astype(F32)).astype(BF16)
    x3 = jnp.concatenate([hi, mid, lo], axis=1)
    rr = lax.broadcasted_iota(I32, (r, r), 0)
    cc = lax.broadcasted_iota(I32, (r, r), 1)
    low = jnp.where(cc <= rr, 1.0, 0.0)
    upp = jnp.where(cc >= rr, 1.0, 0.0)
    same_d = jnp.where((rr // D_CHUNK) == (cc // D_CHUNK), 1.0, 0.0)
    same_m = jnp.where((rr // M_CHUNK) == (cc // M_CHUNK), 1.0, 0.0)
    mats = jnp.concatenate([
        low * same_m,
        upp * same_m,
        low * same_d,
        upp * same_d,
        same_d,
    ], axis=0).astype(BF16)
    y3 = jnp.dot(mats, x3, preferred_element_type=F32)
    y = y3[:, 0:LANES] + y3[:, LANES:2 * LANES] + y3[:, 2 * LANES:3 * LANES]
    pm, sm_, pd, sd, td = (y[i * r:(i + 1) * r] for i in range(5))
    half_m = CH_F + M_HEADS
    half_d = CH_G + D_HEADS
    out = jnp.where(lane < CH_F, xe,
          jnp.where(lane < half_m, pm,
          jnp.where(lane < CH_G, sm_,
          jnp.where(lane < half_d, pd,
          jnp.where(lane < CH_B, sd,
          jnp.where(lane < CH_T, xe,
          jnp.where(lane < CH_T + N_DIR * D_HEADS, pltpu.roll(td, CH_T - CH_G, axis=1), 0.0)))))))
    col_ref[0] = out
    row_ref[0] = out.T


def _gate_prep(smalls3, params):
    b, s, _ = smalls3.shape
    r = GATE_ROWS
    return pl.pallas_call(
        _gate_kernel,
        grid=(b, s // r),
        in_specs=[
            pl.BlockSpec((1, r, LANES), lambda i, j: (i, j, 0)),
            pl.BlockSpec((8, LANES), lambda i, j: (0, 0)),
        ],
        out_specs=[
            pl.BlockSpec((1, r, LANES), lambda i, j: (i, j, 0)),
            pl.BlockSpec((1, LANES, r), lambda i, j: (i, 0, j)),
        ],
        out_shape=[jax.ShapeDtypeStruct((b, s, LANES), F32), jax.ShapeDtypeStruct((b, LANES, s), F32)],
        compiler_params=_cparams(("parallel", "parallel")),
        name="gateprep",
    )(smalls3, params)


def _mlstm_kernel(q_ref, k_ref, v_ref, o_ref, gc_ref, gr_ref, ng_ref, y_ref,
                  hf_ref, hb_ref, c_ref, n_ref):
    head = pl.program_id(1)
    s = q_ref.shape[1]
    L = M_CHUNK
    nc = s // L
    scale = M_DK ** -0.5
    rr = lax.broadcasted_iota(I32, (L, L), 0)
    cc = lax.broadcasted_iota(I32, (L, L), 1)
    hbufs = (hf_ref, hb_ref)

    c_ref[...] = jnp.zeros(c_ref.shape, F32)
    n_ref[...] = jnp.zeros(n_ref.shape, F32)

    def chunk(d, c, m_st):
        r0 = pl.multiple_of(c * L, L)
        q = q_ref[0, pl.ds(r0, L), :]
        k = k_ref[0, pl.ds(r0, L), :] * scale
        v = v_ref[0, pl.ds(r0, L), :]
        qb, kb, vb = q.astype(BF16), k.astype(BF16), v.astype(BF16)
        ch_i = CH_I + d * M_HEADS + head
        ch_f = CH_F + d * M_HEADS + head
        bc_col = _lane_pick(gc_ref[0, pl.ds(r0, L), :], ch_f)
        i_row = gr_ref[0, pl.ds(ch_i, 1), pl.ds(r0, L)]
        bc_row = gr_ref[0, pl.ds(ch_f, 1), pl.ds(r0, L)]
        a_row = i_row - bc_row
        mask = (cc <= rr) if d == 0 else (cc >= rr)
        log_intra = jnp.where(mask, bc_col + a_row, -jnp.inf)
        log_inter = bc_col + m_st
        m_row = jnp.maximum(log_inter, jnp.max(log_intra, axis=1, keepdims=True))
        w_intra = jnp.exp(log_intra - m_row)
        w_inter = jnp.exp(log_inter - m_row)
        scores = lax.dot_general(qb, kb, _NT, preferred_element_type=F32) * w_intra
        c_st = c_ref[d]
        n_st = n_ref[d, 0:1, :]
        num = _bdot(scores, vb) + w_inter * _bdot(qb, c_st)
        den = (jnp.sum(scores, axis=1, keepdims=True)
               + w_inter * jnp.sum(q * n_st, axis=1, keepdims=True))
        hbufs[d][pl.ds(r0, L), :] = num / jnp.maximum(jnp.abs(den), jnp.exp(-m_row))
        b_last = bc_row[:, L - 1:L] if d == 0 else bc_row[:, 0:1]
        log_state = b_last + a_row
        m_new = jnp.maximum(b_last + m_st, jnp.max(log_state, axis=1, keepdims=True))
        w_state = jnp.exp(log_state - m_new)
        decay = jnp.exp(b_last + m_st - m_new)
        kw_t = (k.T * w_state).astype(BF16)
        c_ref[d] = decay * c_st + jnp.dot(kw_t, vb, preferred_element_type=F32)
        w8 = jnp.broadcast_to(w_state, (8, L)).astype(BF16)
        n_ref[d] = decay * n_ref[d] + jnp.dot(w8, kb, preferred_element_type=F32)
        return m_new

    def body(j, carry):
        m_f, m_b = carry
        return chunk(0, j, m_f), chunk(1, nc - 1 - j, m_b)

    m0 = jnp.full((1, 1), -jnp.inf, F32)
    lax.fori_loop(0, nc, body, (m0, m0))

    def finish(c, _):
        r0 = pl.multiple_of(c * L, L)
        hh = hf_ref[pl.ds(r0, L), :] + hb_ref[pl.ds(r0, L), :]
        hh = hh * lax.rsqrt(jnp.mean(hh * hh, axis=-1, keepdims=True) + RMS_EPS)
        hh = hh * ng_ref[0]
        y_ref[0, pl.ds(r0, L), :] = (_sigmoid(o_ref[0, pl.ds(r0, L), :]) * hh).astype(y_ref.dtype)
        return 0

    lax.fori_loop(0, nc, finish, 0)


def _mlstm(proj3, gcol, grow, norm_g, off_q, off_k, off_v, off_o):
    b, s, _ = proj3.shape
    bq, bk = off_q // M_DK, off_k // M_DK
    bv, bo = off_v // M_DV, off_o // M_DV
    return pl.pallas_call(
        _mlstm_kernel,
        grid=(b, M_HEADS),
        in_specs=[
            pl.BlockSpec((1, s, M_DK), lambda i, h: (i, 0, bq + h)),
            pl.BlockSpec((1, s, M_DK), lambda i, h: (i, 0, bk + h)),
            pl.BlockSpec((1, s, M_DV), lambda i, h: (i, 0, bv + h)),
            pl.BlockSpec((1, s, M_DV), lambda i, h: (i, 0, bo + h)),
            pl.BlockSpec((1, s, LANES), lambda i, h: (i, 0, 0)),
            pl.BlockSpec((1, LANES, s), lambda i, h: (i, 0, 0)),
            pl.BlockSpec((1, 1, M_DV), lambda i, h: (h, 0, 0)),
        ],
        out_specs=pl.BlockSpec((1, s, M_DV), lambda i, h: (i, 0, h)),
        out_shape=jax.ShapeDtypeStruct((b, s, M_HEADS * M_DV), BF16),
        scratch_shapes=[
            pltpu.VMEM((s, M_DV), F32),
            pltpu.VMEM((s, M_DV), F32),
            pltpu.VMEM((N_DIR, M_DK, M_DV), F32),
            pltpu.VMEM((N_DIR, 8, M_DK), F32),
        ],
        compiler_params=_cparams(("parallel", "arbitrary")),
        name="mlstm",
    )(proj3, proj3, proj3, proj3, gcol, grow, norm_g)


def _delta_kernel(q_ref, k_ref, v_ref, z_ref, cwq_ref, cwk_ref, cwv_ref, gc_ref, gr_ref, ng_ref, y_ref,
                  xp_ref, qs_ref, ks_ref, vs_ref, kq_ref, n_ref, dec_ref, o_ref, st_ref):
    head = pl.program_id(1)
    s = q_ref.shape[1]
    L = D_CHUNK
    G = D_GROUP
    nc = s // L
    ng = s // G
    halo = CONV_HALO
    pad = CONV_W // 2

    xp_ref[0:halo, :] = jnp.zeros((halo, D_DIM), F32)
    xp_ref[s + halo:s + 2 * halo, :] = jnp.zeros((halo, D_DIM), F32)

    def conv_into(x_ref, cw_ref, dst_ref, l2, mult):
        xp_ref[halo:s + halo, :] = x_ref[0]
        for blk in range(ng):
            base = halo + blk * G - pad
            acc = xp_ref[base:base + G, :] * cw_ref[0:1, :]
            for j in range(1, CONV_W):
                acc = acc + xp_ref[base + j:base + j + G, :] * cw_ref[j:j + 1, :]
            y = acc * _sigmoid(acc)
            if l2:
                y = y * lax.rsqrt(jnp.sum(y * y, axis=-1, keepdims=True) + L2_EPS)
            if mult != 1.0:
                y = y * mult
            dst_ref[blk * G:(blk + 1) * G, :] = y

    conv_into(q_ref, cwq_ref, qs_ref, True, D_DIM ** -0.5)
    conv_into(k_ref, cwk_ref, ks_ref, True, 1.0)
    conv_into(v_ref, cwv_ref, vs_ref, False, 1.0)

    tchunk = lax.broadcasted_iota(I32, (D_DIM, D_GROUP), 1) // D_CHUNK
    rr = lax.broadcasted_iota(I32, (G, G), 0)
    cc = lax.broadcasted_iota(I32, (G, G), 1)
    same = (rr // L) == (cc // L)
    pr = lax.broadcasted_iota(I32, (L, G), 0)
    pc = lax.broadcasted_iota(I32, (L, G), 1)
    eye_p = jnp.where((pc % L) == pr, 1.0, 0.0).astype(F32)

    def pack(m):
        out = m[0:L]
        for i in range(1, D_PER_GROUP):
            out = out + m[i * L:(i + 1) * L]
        return out

    def unpack(p):
        return jnp.where(same, jnp.concatenate([p] * D_PER_GROUP, axis=0), 0.0)

    def groups(gis):
        chains = []
        for gi in gis:
            r0 = pl.multiple_of(gi * G, G)
            kk_ = ks_ref[pl.ds(r0, G), :]
            qq_ = qs_ref[pl.ds(r0, G), :]
            vv_ = vs_ref[pl.ds(r0, G), :]
            kb = kk_.astype(BF16)
            kk = lax.dot_general(kb, kb, _NT, preferred_element_type=F32)
            qk = lax.dot_general(qq_.astype(BF16), kb, _NT, preferred_element_type=F32)
            gcol = gc_ref[0, pl.ds(r0, G), :]
            for d in range(N_DIR):
                ch = d * D_HEADS + head
                g_col = _lane_pick(gcol, CH_G + ch)
                b_col = _lane_pick(gcol, CH_B + ch)
                t_col = _lane_pick(gcol, CH_T + ch)
                g_row = gr_ref[0, pl.ds(CH_G + ch, 1), pl.ds(r0, G)]
                tri = (cc <= rr) if d == 0 else (cc >= rr)
                strict = (cc < rr) if d == 0 else (cc > rr)
                gam = jnp.exp(jnp.where(tri, jnp.where(same, g_col - g_row, -jnp.inf), -jnp.inf))
                x = jnp.where(strict, -(b_col * kk * gam), 0.0)
                eg = jnp.exp(g_col)
                chains.append(dict(
                    gi=gi, d=d, r0=r0, x=x, xp=pack(x), attn=(qk * gam).astype(BF16),
                    rhs=jnp.concatenate([b_col * vv_, (b_col * eg) * kk_], axis=1).astype(BF16),
                    qg=qq_ * eg, kdt=(kk_ * jnp.exp(t_col - g_col)).T, dec=jnp.exp(t_col)))
        for c in chains:
            c["tp"] = eye_p + c["xp"]
            c["p"] = _bdot(c["xp"], c["x"])
        for _it in range(4):
            for c in chains:
                res = _bdot(jnp.concatenate([c["tp"], c["p"]], axis=0), unpack(c["p"]))
                c["tp"] = c["tp"] + res[0:L]
                c["p"] = res[L:2 * L]
        for c in chains:
            c["tp"] = c["tp"] + _bdot(c["tp"], unpack(c["p"]))
        for c in chains:
            c["uw"] = _bdot(unpack(c["tp"]), c["rhs"]).astype(BF16)
        for c in chains:
            au = jnp.dot(c["attn"], c["uw"], preferred_element_type=F32)
            o_ref[c["d"], pl.ds(c["r0"], G), :] = au[:, 0:D_DIM]
            c["qe"] = (c["qg"] - au[:, D_DIM:2 * D_DIM]).astype(BF16)
        for c in chains:
            for ci in range(D_PER_GROUP):
                idx = c["d"] * nc + c["gi"] * D_PER_GROUP + ci
                ku = _bdot(jnp.where(tchunk == ci, c["kdt"], 0.0), c["uw"])
                n_ref[idx] = ku[:, 0:D_DIM]
                kq_ref[idx, 0:D_DIM, :] = (-ku[:, D_DIM:2 * D_DIM]).astype(BF16)
                kq_ref[idx, D_DIM:D_DIM + L, :] = c["qe"][ci * L:(ci + 1) * L]
                dec_ref[idx] = jnp.broadcast_to(c["dec"][ci * L:ci * L + 8], (8, D_DIM))

    def group_batch(i, _):
        groups([D_GROUPS_PER_ITER * i + t for t in range(D_GROUPS_PER_ITER)])
        return 0

    lax.fori_loop(0, ng // D_GROUPS_PER_ITER, group_batch, 0)

    st_ref[...] = jnp.zeros(st_ref.shape, F32)

    def step(j, _):
        for d in range(N_DIR):
            c = j if d == 0 else nc - 1 - j
            r0 = pl.multiple_of(c * L, L)
            idx = d * nc + c
            st = st_ref[d]
            res = jnp.dot(kq_ref[idx], st.astype(BF16), preferred_element_type=F32)
            st_ref[d] = dec_ref[idx][0:1, :] * st + res[0:D_DIM] + n_ref[idx]
            o_ref[d, pl.ds(r0, L), :] = o_ref[d, pl.ds(r0, L), :] + res[D_DIM:D_DIM + L]
        return 0

    lax.fori_loop(0, nc, step, 0)

    def finish(gi, _):
        r0 = pl.multiple_of(gi * G, G)
        o = o_ref[0, pl.ds(r0, G), :] + o_ref[1, pl.ds(r0, G), :]
        o = o * lax.rsqrt(jnp.mean(o * o, axis=-1, keepdims=True) + RMS_EPS) * ng_ref[...]
        z = z_ref[0, pl.ds(r0, G), :]
        y_ref[0, pl.ds(r0, G), :] = (o * (z * _sigmoid(z))).astype(y_ref.dtype)
        return 0

    lax.fori_loop(0, ng, finish, 0)


def _delta(proj3, conv_w, gcol, grow, norm_g, off_q, off_k, off_v, off_z):
    b, s, _ = proj3.shape
    w = D_HEADS * D_DIM
    bq, bk, bv, bz = (o // D_DIM for o in (off_q, off_k, off_v, off_z))
    nc = s // D_CHUNK
    return pl.pallas_call(
        _delta_kernel,
        grid=(b, D_HEADS),
        in_specs=[
            pl.BlockSpec((1, s, D_DIM), lambda i, h: (i, 0, bq + h)),
            pl.BlockSpec((1, s, D_DIM), lambda i, h: (i, 0, bk + h)),
            pl.BlockSpec((1, s, D_DIM), lambda i, h: (i, 0, bv + h)),
            pl.BlockSpec((1, s, D_DIM), lambda i, h: (i, 0, bz + h)),
            pl.BlockSpec((CONV_W, D_DIM), lambda i, h: (0, h)),
            pl.BlockSpec((CONV_W, D_DIM), lambda i, h: (0, D_HEADS + h)),
            pl.BlockSpec((CONV_W, D_DIM), lambda i, h: (0, 2 * D_HEADS + h)),
            pl.BlockSpec((1, s, LANES), lambda i, h: (i, 0, 0)),
            pl.BlockSpec((1, LANES, s), lambda i, h: (i, 0, 0)),
            pl.BlockSpec((1, D_DIM), lambda i, h: (0, 0)),
        ],
        out_specs=pl.BlockSpec((1, s, D_DIM), lambda i, h: (i, 0, h)),
        out_shape=jax.ShapeDtypeStruct((b, s, w), BF16),
        scratch_shapes=[
            pltpu.VMEM((s + 2 * CONV_HALO, D_DIM), F32),
            pltpu.VMEM((s, D_DIM), F32),
            pltpu.VMEM((s, D_DIM), F32),
            pltpu.VMEM((s, D_DIM), F32),
            pltpu.VMEM((N_DIR * nc, D_DIM + D_CHUNK, D_DIM), BF16),
            pltpu.VMEM((N_DIR * nc, D_DIM, D_DIM), F32),
            pltpu.VMEM((N_DIR * nc, 8, D_DIM), F32),
            pltpu.VMEM((N_DIR, s, D_DIM), F32),
            pltpu.VMEM((N_DIR, D_DIM, D_DIM), F32),
        ],
        compiler_params=_cparams(("parallel", "arbitrary")),
        name="delta",
    )(proj3, proj3, proj3, proj3, conv_w, conv_w, conv_w, gcol, grow, norm_g)


def _mix_kernel(x_ref, ym_ref, yd_ref, gm_ref, gd_ref, wbm_ref, wbd_ref, wo_ref, o_ref):
    a = jnp.dot(ym_ref[...], wbm_ref[...], preferred_element_type=F32)
    b = jnp.dot(yd_ref[...], wbd_ref[...], preferred_element_type=F32)
    mixed = _sigmoid(gm_ref[...]) * a + _sigmoid(gd_ref[...]) * b
    o_ref[...] = x_ref[...] + jnp.dot(mixed.astype(BF16), wo_ref[...], preferred_element_type=F32)


def _mix(x2, ym2, yd2, proj2, wbm, wbd, wo, tm):
    m, d = x2.shape
    const = lambda i: (0, 0)
    return pl.pallas_call(
        _mix_kernel,
        grid=(m // tm,),
        in_specs=[
            pl.BlockSpec((tm, d), lambda i: (i, 0)),
            pl.BlockSpec((tm, ym2.shape[1]), lambda i: (i, 0)),
            pl.BlockSpec((tm, yd2.shape[1]), lambda i: (i, 0)),
            pl.BlockSpec((tm, d), lambda i: (i, 0)),
            pl.BlockSpec((tm, d), lambda i: (i, 1)),
            pl.BlockSpec(wbm.shape, const),
            pl.BlockSpec(wbd.shape, const),
            pl.BlockSpec(wo.shape, const),
        ],
        out_specs=pl.BlockSpec((tm, d), lambda i: (i, 0)),
        out_shape=jax.ShapeDtypeStruct((m, d), F32),
        compiler_params=_cparams(("parallel",)),
        name="mix",
    )(x2, ym2, yd2, proj2, proj2, wbm, wbd, wo)


def _ffn_kernel(x_ref, g2_ref, gf_ref, w1_ref, w2_ref, o_ref, h_ref, acc_ref):
    f = pl.program_id(1)

    @pl.when(f == 0)
    def _():
        x = x_ref[...]
        ms = jnp.mean(x * x, axis=-1, keepdims=True)
        h_ref[...] = (x * lax.rsqrt(ms + RMS_EPS) * g2_ref[...]).astype(BF16)
        acc_ref[...] = jnp.zeros(acc_ref.shape, F32)

    hid = jnp.dot(h_ref[...], w1_ref[...], preferred_element_type=F32)
    act = jnp.square(jnp.maximum(hid, 0.0)).astype(BF16)
    acc_ref[...] += jnp.dot(act, w2_ref[...], preferred_element_type=F32)

    @pl.when(f == pl.num_programs(1) - 1)
    def _():
        x = x_ref[...] + acc_ref[...]
        ms = jnp.mean(x * x, axis=-1, keepdims=True)
        o_ref[...] = x * lax.rsqrt(ms + RMS_EPS) * gf_ref[...]


def _ffn(x2, g2, gf, w1, w2, tm, tf):
    m, d = x2.shape
    dff = w1.shape[1]
    return pl.pallas_call(
        _ffn_kernel,
        grid=(m // tm, dff // tf),
        in_specs=[
            pl.BlockSpec((tm, d), lambda i, f: (i, 0)),
            pl.BlockSpec((1, d), lambda i, f: (0, 0)),
            pl.BlockSpec((1, d), lambda i, f: (0, 0)),
            pl.BlockSpec((d, tf), lambda i, f: (0, f)),
            pl.BlockSpec((tf, d), lambda i, f: (f, 0)),
        ],
        out_specs=pl.BlockSpec((tm, d), lambda i, f: (i, 0)),
        out_shape=jax.ShapeDtypeStruct((m, d), F32),
        scratch_shapes=[pltpu.VMEM((tm, d), BF16), pltpu.VMEM((tm, d), F32)],
        compiler_params=_cparams(("parallel", "arbitrary")),
        name="ffn",
    )(x2, g2, gf, w1, w2)


def _pick_tile(n, candidates):
    for c in candidates:
        if n % c == 0:
            return c
    raise ValueError(f"no tile in {candidates} divides {n}")


def _layer(x, norm1_g, w_in, i_bias, f_bias, m_norm_g, conv_w, a_log, dt_bias, d_norm_g,
           w_bm, w_bd, w_out, norm2_g, w_ff1, w_ff2, out_g):
    b, s, d = x.shape
    m = b * s
    mw, mqk, dw = M_HEADS * M_DV, M_HEADS * M_DK, D_HEADS * D_DIM
    nm, nd = N_DIR * M_HEADS, N_DIR * D_HEADS
    splits = (mqk, mqk, mw, mw, nm, nm, 3 * dw, dw, nd, nd, 2 * d)
    bounds = [0]
    for w_ in splits:
        bounds.append(bounds[-1] + w_)
    cols = [w_in[:, bounds[i]:bounds[i + 1]] for i in range(len(splits))]
    (c_mq, c_mk, c_mv, c_mo, c_mi, c_mf, c_dqkv, c_dz, c_da, c_db, c_gates) = cols
    w_main = jnp.concatenate([c_gates, c_mq, c_mk, c_mv, c_mo, c_dqkv, c_dz], axis=1).astype(BF16)
    off_mq = 2 * d
    off_mk = off_mq + mqk
    off_mv = off_mk + mqk
    off_mo = off_mv + mw
    off_dq = off_mo + mw
    off_dk = off_dq + dw
    off_dv = off_dk + dw
    off_dz = off_dv + dw
    n_small = 2 * nm + 2 * nd
    w_small = jnp.concatenate([c_mi, c_mf, c_da, c_db, jnp.zeros((d, LANES - n_small), F32)], axis=1).astype(BF16)
    zpad = lambda n: jnp.zeros((n,), F32)
    bias_row = jnp.concatenate([i_bias.reshape(-1), f_bias.reshape(-1), dt_bias.reshape(-1), zpad(LANES - CH_B)])
    alog_row = jnp.concatenate([zpad(CH_G), a_log.reshape(-1), zpad(LANES - CH_B)])
    gate_params = jnp.concatenate([bias_row[None], alog_row[None], jnp.zeros((6, LANES), F32)], axis=0)

    x2 = x.reshape(m, d)
    tm = _pick_tile(m, (512, 256, 128))
    tn = _pick_tile(w_main.shape[1], (1024, 512, 256))
    proj2, smalls2 = _inproj(x2, norm1_g.reshape(1, d), w_main, w_small, _pick_tile(m, (1024, 512, 256)), tn)
    proj3 = proj2.reshape(b, s, -1)
    gcol, grow = _gate_prep(smalls2.reshape(b, s, LANES), gate_params)

    ym = _mlstm(proj3, gcol, grow, m_norm_g.reshape(M_HEADS, 1, M_DV), off_mq, off_mk, off_mv, off_mo)
    yd = _delta(proj3, conv_w, gcol, grow, d_norm_g.reshape(1, D_DIM), off_dq, off_dk, off_dv, off_dz)

    x1 = _mix(x2, ym.reshape(m, mw), yd.reshape(m, dw), proj2,
              w_bm.astype(BF16), w_bd.astype(BF16), w_out.astype(BF16), _pick_tile(m, (256, 128)))
    tf = _pick_tile(w_ff1.shape[1], (1024, 512, 256))
    out = _ffn(x1, norm2_g.reshape(1, d), out_g.reshape(1, d), w_ff1.astype(BF16), w_ff2.astype(BF16), tm, tf)
    return out.reshape(b, s, d)


def kernel(x, norm1_g, w_in, mlstm_i_bias, mlstm_f_bias, mlstm_norm_g, delta_conv_w, delta_a_log,
           delta_dt_bias, delta_norm_g, w_branch_m, w_branch_d, w_out, norm2_g, w_ff1, w_ff2, norm_f_g):
    depth = w_in.shape[0]
    assert depth == 1, "the fused FFN + final-norm epilogue assumes a single layer"
    return _layer(x, norm1_g[0], w_in[0], mlstm_i_bias[0], mlstm_f_bias[0], mlstm_norm_g[0], delta_conv_w[0],
                  delta_a_log[0], delta_dt_bias[0], delta_norm_g[0], w_branch_m[0], w_branch_d[0], w_out[0],
                  norm2_g[0], w_ff1[0], w_ff2[0], norm_f_g)
```

```python
import functools

import jax
import jax.numpy as jnp
from jax import lax
from jax.experimental import pallas as pl
from jax.experimental.pallas import tpu as pltpu

F32 = jnp.float32
BF16 = jnp.bfloat16
I32 = jnp.int32

N_DIR = 2
M_HEADS = 4
M_DK = 128
M_DV = 256
D_HEADS = 8
D_DIM = 128
CONV_W = 5
RMS_EPS = 1e-6
L2_EPS = 1e-6

LANES = 128
M_CHUNK = 256
M_CHUNKS_PER_ITER = 4
D_CHUNK = 64
D_GROUP = 256
D_PER_GROUP = D_GROUP // D_CHUNK
D_HEADS_PER_STEP = 2
D_GROUPS_PER_ITER = 4
GATE_ROWS = 256
CONV_HALO = 8

CH_I = 0
CH_F = 8
CH_G = 16
CH_B = 32
CH_T = 48

VMEM_LIMIT = 56 * 1024 * 1024

_NT = (((1,), (1,)), ((), ()))


def _cparams(sem):
    return pltpu.CompilerParams(dimension_semantics=sem, vmem_limit_bytes=VMEM_LIMIT)


def _bdot(a, b):
    return jnp.dot(a.astype(BF16), b.astype(BF16), preferred_element_type=F32)


def _bdot_nt(a, b):
    return lax.dot_general(a.astype(BF16), b.astype(BF16), _NT, preferred_element_type=F32)


def _sigmoid(x):
    return 1.0 / (1.0 + jnp.exp(-x))


def _softplus(x):
    return jnp.maximum(x, 0.0) + jnp.log1p(jnp.exp(-jnp.abs(x)))


def _lane_pick(x, ch):
    lane = lax.broadcasted_iota(I32, x.shape, 1)
    return jnp.sum(jnp.where(lane == ch, x, 0.0), axis=1, keepdims=True)


def _inproj_kernel(x_ref, g_ref, w_ref, ws_ref, o_ref, os_ref, hn_ref):
    @pl.when(pl.program_id(1) == 0)
    def _():
        x = x_ref[...]
        ms = jnp.mean(x * x, axis=-1, keepdims=True)
        hn = (x * lax.rsqrt(ms + RMS_EPS) * g_ref[...]).astype(BF16)
        hn_ref[...] = hn
        os_ref[...] = jnp.dot(hn, ws_ref[...], preferred_element_type=F32)

    o_ref[...] = jnp.dot(hn_ref[...], w_ref[...], preferred_element_type=F32)


def _inproj(x2, g, w_main, w_small, tm, tn):
    m, d = x2.shape
    n = w_main.shape[1]
    return pl.pallas_call(
        _inproj_kernel,
        grid=(m // tm, n // tn),
        in_specs=[
            pl.BlockSpec((tm, d), lambda i, j: (i, 0)),
            pl.BlockSpec((1, d), lambda i, j: (0, 0)),
            pl.BlockSpec((d, tn), lambda i, j: (0, j)),
            pl.BlockSpec((d, LANES), lambda i, j: (0, 0)),
        ],
        out_specs=[
            pl.BlockSpec((tm, tn), lambda i, j: (i, j)),
            pl.BlockSpec((tm, LANES), lambda i, j: (i, 0)),
        ],
        out_shape=[jax.ShapeDtypeStruct((m, n), F32), jax.ShapeDtypeStruct((m, LANES), F32)],
        scratch_shapes=[pltpu.VMEM((tm, d), BF16)],
        compiler_params=_cparams(("parallel", "arbitrary")),
        name="inproj",
    )(x2, g, w_main, w_small)


def _gate_kernel(sm_ref, par_ref, col_ref, row_ref):
    r = GATE_ROWS
    x = sm_ref[0] + par_ref[0:1, :]
    lane = lax.broadcasted_iota(I32, x.shape, 1)
    logf = -_softplus(-x)
    g = -jnp.exp(par_ref[1:2, :]) * _softplus(x)
    beta = _sigmoid(x)
    xe = jnp.where(lane < CH_F, x,
                   jnp.where(lane < CH_G, logf,
                             jnp.where(lane < CH_B, g,
                                       jnp.where(lane < CH_T, beta, 0.0))))
    hi = xe.astype(BF16)
    r1 = xe - hi.astype(F32)
    mid = r1.astype(BF16)
    lo = (r1 - mid.astype(F32)).astype(BF16)
    x3 = jnp.concatenate([hi, mid, lo], axis=1)
    rr = lax.broadcasted_iota(I32, (r, r), 0)
    cc = lax.broadcasted_iota(I32, (r, r), 1)
    low = jnp.where(cc <= rr, 1.0, 0.0)
    upp = jnp.where(cc >= rr, 1.0, 0.0)
    same_d = jnp.where((rr // D_CHUNK) == (cc // D_CHUNK), 1.0, 0.0)
    same_m = jnp.where((rr // M_CHUNK) == (cc // M_CHUNK), 1.0, 0.0)
    mats = jnp.concatenate([
        low * same_m,
        upp * same_m,
        low * same_d,
        upp * same_d,
        same_d,
    ], axis=0).astype(BF16)
    y3 = jnp.dot(mats, x3, preferred_element_type=F32)
    y = y3[:, 0:LANES] + y3[:, LANES:2 * LANES] + y3[:, 2 * LANES:3 * LANES]
    pm, sm_, pd, sd, td = (y[i * r:(i + 1) * r] for i in range(5))
    half_m = CH_F + M_HEADS
    half_d = CH_G + D_HEADS
    out = jnp.where(lane < CH_F, xe,
          jnp.where(lane < half_m, pm,
          jnp.where(lane < CH_G, sm_,
          jnp.where(lane < half_d, pd,
          jnp.where(lane < CH_B, sd,
          jnp.where(lane < CH_T, xe,
          jnp.where(lane < CH_T + N_DIR * D_HEADS, pltpu.roll(td, CH_T - CH_G, axis=1), 0.0)))))))
    col_ref[0] = out
    row_ref[0] = out.T


def _gate_prep(smalls3, params):
    b, s, _ = smalls3.shape
    r = GATE_ROWS
    return pl.pallas_call(
        _gate_kernel,
        grid=(b, s // r),
        in_specs=[
            pl.BlockSpec((1, r, LANES), lambda i, j: (i, j, 0)),
            pl.BlockSpec((8, LANES), lambda i, j: (0, 0)),
        ],
        out_specs=[
            pl.BlockSpec((1, r, LANES), lambda i, j: (i, j, 0)),
            pl.BlockSpec((1, LANES, r), lambda i, j: (i, 0, j)),
        ],
        out_shape=[jax.ShapeDtypeStruct((b, s, LANES), F32), jax.ShapeDtypeStruct((b, LANES, s), F32)],
        compiler_params=_cparams(("parallel", "parallel")),
        name="gateprep",
    )(smalls3, params)


def _mlstm_kernel(q_ref, k_ref, v_ref, o_ref, gc_ref, gr_ref, ng_ref, y_ref,
                  hn_ref, den_ref, mi_ref, cu_ref, nu_ref, bl_ref, ml_ref, cs_ref, ns_ref, ms_ref, c_ref):
    head = pl.program_id(1)
    s = q_ref.shape[1]
    L = M_CHUNK
    nc = s // L
    scale = M_DK ** -0.5
    rr = lax.broadcasted_iota(I32, (L, L), 0)
    cc = lax.broadcasted_iota(I32, (L, L), 1)

    def bcast8(x11):
        return jnp.broadcast_to(x11, (8, LANES))

    def local(cs):
        chains = []
        for c in cs:
            r0 = pl.multiple_of(c * L, L)
            q = q_ref[0, pl.ds(r0, L), :]
            k = k_ref[0, pl.ds(r0, L), :] * scale
            vb = v_ref[0, pl.ds(r0, L), :].astype(BF16)
            kb = k.astype(BF16)
            qk = lax.dot_general(q.astype(BF16), kb, _NT, preferred_element_type=F32)
            k_t = k.T
            gcol = gc_ref[0, pl.ds(r0, L), :]
            for d in range(N_DIR):
                ch_i = CH_I + d * M_HEADS + head
                ch_f = CH_F + d * M_HEADS + head
                bc_col = _lane_pick(gcol, ch_f)
                i_row = gr_ref[0, pl.ds(ch_i, 1), pl.ds(r0, L)]
                bc_row = gr_ref[0, pl.ds(ch_f, 1), pl.ds(r0, L)]
                a_row = i_row - bc_row
                mask = (cc <= rr) if d == 0 else (cc >= rr)
                log_intra = jnp.where(mask, bc_col + a_row, -jnp.inf)
                m_intra = jnp.max(log_intra, axis=1, keepdims=True)
                p = qk * jnp.exp(log_intra - m_intra)
                b_last = bc_row[:, L - 1:L] if d == 0 else bc_row[:, 0:1]
                log_state = b_last + a_row
                m_loc = jnp.max(log_state, axis=1, keepdims=True)
                w_state = jnp.exp(log_state - m_loc)
                chains.append(dict(
                    idx=d * nc + c, d=d, r0=r0, vb=vb, kb=kb, pb=p.astype(BF16),
                    den=jnp.sum(p, axis=1, keepdims=True), m_intra=m_intra,
                    kw=(k_t * w_state).astype(BF16), w8=jnp.broadcast_to(w_state, (8, L)).astype(BF16),
                    b_last=b_last, m_loc=m_loc))
        for c in chains:
            hn_ref[c["d"], pl.ds(c["r0"], L), :] = jnp.dot(c["pb"], c["vb"], preferred_element_type=F32)
            den_ref[c["d"], pl.ds(c["r0"], L), :] = jnp.broadcast_to(c["den"], (L, LANES))
            mi_ref[c["d"], pl.ds(c["r0"], L), :] = jnp.broadcast_to(c["m_intra"], (L, LANES))
        for c in chains:
            cu_ref[c["idx"]] = jnp.dot(c["kw"], c["vb"], preferred_element_type=F32)
            nu_ref[c["idx"]] = jnp.dot(c["w8"], c["kb"], preferred_element_type=F32)
            bl_ref[c["idx"]] = bcast8(c["b_last"])
            ml_ref[c["idx"]] = bcast8(c["m_loc"])

    def local_trip(i, _):
        local([M_CHUNKS_PER_ITER * i + t for t in range(M_CHUNKS_PER_ITER)])
        return 0

    lax.fori_loop(0, nc // M_CHUNKS_PER_ITER, local_trip, 0)

    c_ref[...] = jnp.zeros(c_ref.shape, F32)

    def scan(j, carry):
        out = []
        for d in range(N_DIR):
            m_st, n_st = carry[d]
            idx = d * nc + (j if d == 0 else nc - 1 - j)
            c_st = c_ref[d]
            cs_ref[idx] = c_st.astype(BF16)
            ns_ref[idx] = n_st
            ms_ref[idx] = m_st
            bl = bl_ref[idx]
            ml = ml_ref[idx]
            m_new = jnp.maximum(bl + m_st, ml)
            decay = jnp.exp(bl + m_st - m_new)
            gain = jnp.exp(ml - m_new)
            c_ref[d] = decay[0:1, 0:1] * c_st + gain[0:1, 0:1] * cu_ref[idx]
            out.append((m_new, decay * n_st + gain * nu_ref[idx]))
        return tuple(out)

    m0 = jnp.full((8, LANES), -jnp.inf, F32)
    n0 = jnp.zeros((8, M_DK), F32)
    lax.fori_loop(0, nc, scan, ((m0, n0), (m0, n0)))

    def wide(x):
        return jnp.concatenate([x] * (M_DV // LANES), axis=1)

    def combine(c, _):
        r0 = pl.multiple_of(c * L, L)
        q = q_ref[0, pl.ds(r0, L), :]
        qb = q.astype(BF16)
        gcol = gc_ref[0, pl.ds(r0, L), :]
        hh = None
        for d in range(N_DIR):
            idx = d * nc + c
            bc = jnp.broadcast_to(_lane_pick(gcol, CH_F + d * M_HEADS + head), (L, LANES))
            den_i = den_ref[d, pl.ds(r0, L), :]
            m_i = mi_ref[d, pl.ds(r0, L), :]
            log_inter = bc + ms_ref[idx][0:1, :]
            m_row = jnp.maximum(log_inter, m_i)
            w_i = jnp.exp(m_i - m_row)
            w_x = jnp.exp(log_inter - m_row)
            q_c = jnp.dot(qb, cs_ref[idx], preferred_element_type=F32)
            q_n = jnp.broadcast_to(jnp.sum(q * ns_ref[idx][0:1, :], axis=1, keepdims=True), (L, LANES))
            num = wide(w_i) * hn_ref[d, pl.ds(r0, L), :] + wide(w_x) * q_c
            den = w_i * den_i + w_x * q_n
            h_d = num * wide(1.0 / jnp.maximum(jnp.abs(den), jnp.exp(-m_row)))
            hh = h_d if hh is None else hh + h_d
        hh = hh * lax.rsqrt(jnp.mean(hh * hh, axis=-1, keepdims=True) + RMS_EPS)
        hh = hh * ng_ref[0]
        y_ref[0, pl.ds(r0, L), :] = (_sigmoid(o_ref[0, pl.ds(r0, L), :]) * hh).astype(y_ref.dtype)
        return 0

    lax.fori_loop(0, nc, combine, 0)


def _mlstm(proj3, gcol, grow, norm_g, off_q, off_k, off_v, off_o):
    b, s, _ = proj3.shape
    assert s % (M_CHUNK * M_CHUNKS_PER_ITER) == 0
    nc = s // M_CHUNK
    bq, bk = off_q // M_DK, off_k // M_DK
    bv, bo = off_v // M_DV, off_o // M_DV
    return pl.pallas_call(
        _mlstm_kernel,
        grid=(b, M_HEADS),
        in_specs=[
            pl.BlockSpec((1, s, M_DK), lambda i, h: (i, 0, bq + h)),
            pl.BlockSpec((1, s, M_DK), lambda i, h: (i, 0, bk + h)),
            pl.BlockSpec((1, s, M_DV), lambda i, h: (i, 0, bv + h)),
            pl.BlockSpec((1, s, M_DV), lambda i, h: (i, 0, bo + h)),
            pl.BlockSpec((1, s, LANES), lambda i, h: (i, 0, 0)),
            pl.BlockSpec((1, LANES, s), lambda i, h: (i, 0, 0)),
            pl.BlockSpec((1, 1, M_DV), lambda i, h: (h, 0, 0)),
        ],
        out_specs=pl.BlockSpec((1, s, M_DV), lambda i, h: (i, 0, h)),
        out_shape=jax.ShapeDtypeStruct((b, s, M_HEADS * M_DV), BF16),
        scratch_shapes=[
            pltpu.VMEM((N_DIR, s, M_DV), F32),
            pltpu.VMEM((N_DIR, s, LANES), F32),
            pltpu.VMEM((N_DIR, s, LANES), F32),
            pltpu.VMEM((N_DIR * nc, M_DK, M_DV), F32),
            pltpu.VMEM((N_DIR * nc, 8, M_DK), F32),
            pltpu.VMEM((N_DIR * nc, 8, LANES), F32),
            pltpu.VMEM((N_DIR * nc, 8, LANES), F32),
            pltpu.VMEM((N_DIR * nc, M_DK, M_DV), BF16),
            pltpu.VMEM((N_DIR * nc, 8, M_DK), F32),
            pltpu.VMEM((N_DIR * nc, 8, LANES), F32),
            pltpu.VMEM((N_DIR, M_DK, M_DV), F32),
        ],
        compiler_params=_cparams(("parallel", "arbitrary")),
        name="mlstm",
    )(proj3, proj3, proj3, proj3, gcol, grow, norm_g)


def _delta_kernel(q_ref, k_ref, v_ref, z_ref, cwq_ref, cwk_ref, cwv_ref, gc_ref, gr_ref, ng_ref, y_ref,
                  xp_ref, qs_ref, ks_ref, vs_ref, kq_ref, n_ref, dec_ref, o_ref, st_ref):
    head0 = pl.program_id(1) * D_HEADS_PER_STEP
    s = q_ref.shape[1]
    L = D_CHUNK
    G = D_GROUP
    nc = s // L
    ng = s // G
    halo = CONV_HALO
    pad = CONV_W // 2

    xp_ref[0:halo, :] = jnp.zeros((halo, D_DIM), F32)
    xp_ref[s + halo:s + 2 * halo, :] = jnp.zeros((halo, D_DIM), F32)

    def conv_into(hh, x_ref, cw_ref, dst_ref, l2, mult):
        lanes = slice(hh * D_DIM, (hh + 1) * D_DIM)
        xp_ref[halo:s + halo, :] = x_ref[0, :, lanes]
        for blk in range(ng):
            base = halo + blk * G - pad
            acc = xp_ref[base:base + G, :] * cw_ref[0:1, lanes]
            for j in range(1, CONV_W):
                acc = acc + xp_ref[base + j:base + j + G, :] * cw_ref[j:j + 1, lanes]
            y = acc * _sigmoid(acc)
            if l2:
                y = y * lax.rsqrt(jnp.sum(y * y, axis=-1, keepdims=True) + L2_EPS)
            if mult != 1.0:
                y = y * mult
            dst_ref[blk * G:(blk + 1) * G, :] = y

    tchunk = lax.broadcasted_iota(I32, (D_DIM, D_GROUP), 1) // D_CHUNK
    rr = lax.broadcasted_iota(I32, (G, G), 0)
    cc = lax.broadcasted_iota(I32, (G, G), 1)
    same = (rr // L) == (cc // L)
    pr = lax.broadcasted_iota(I32, (L, G), 0)
    pc = lax.broadcasted_iota(I32, (L, G), 1)
    eye_p = jnp.where((pc % L) == pr, 1.0, 0.0).astype(F32)

    def pack(m):
        out = m[0:L]
        for i in range(1, D_PER_GROUP):
            out = out + m[i * L:(i + 1) * L]
        return out

    def unpack(p):
        return jnp.where(same, jnp.concatenate([p] * D_PER_GROUP, axis=0), 0.0)

    def groups(hh, gis):
        chains = []
        for gi in gis:
            r0 = pl.multiple_of(gi * G, G)
            kk_ = ks_ref[pl.ds(r0, G), :]
            qq_ = qs_ref[pl.ds(r0, G), :]
            vv_ = vs_ref[pl.ds(r0, G), :]
            kb = kk_.astype(BF16)
            kk = lax.dot_general(kb, kb, _NT, preferred_element_type=F32)
            qk = lax.dot_general(qq_.astype(BF16), kb, _NT, preferred_element_type=F32)
            gcol = gc_ref[0, pl.ds(r0, G), :]
            for d in range(N_DIR):
                ch = d * D_HEADS + head0 + hh
                g_col = _lane_pick(gcol, CH_G + ch)
                b_col = _lane_pick(gcol, CH_B + ch)
                t_col = _lane_pick(gcol, CH_T + ch)
                g_row = gr_ref[0, pl.ds(CH_G + ch, 1), pl.ds(r0, G)]
                tri = (cc <= rr) if d == 0 else (cc >= rr)
                strict = (cc < rr) if d == 0 else (cc > rr)
                gam = jnp.exp(jnp.where(tri, jnp.where(same, g_col - g_row, -jnp.inf), -jnp.inf))
                x = jnp.where(strict, -(b_col * kk * gam), 0.0)
                eg = jnp.exp(g_col)
                chains.append(dict(
                    gi=gi, hd=hh * N_DIR + d, r0=r0, x=x, xp=pack(x), attn=(qk * gam).astype(BF16),
                    rhs=jnp.concatenate([b_col * vv_, (b_col * eg) * kk_], axis=1).astype(BF16),
                    qg=qq_ * eg, kdt=(kk_ * jnp.exp(t_col - g_col)).T, dec=jnp.exp(t_col)))
        for c in chains:
            c["tp"] = eye_p + c["xp"]
            c["p"] = _bdot(c["xp"], c["x"])
        for _it in range(4):
            for c in chains:
                res = _bdot(jnp.concatenate([c["tp"], c["p"]], axis=0), unpack(c["p"]))
                c["tp"] = c["tp"] + res[0:L]
                c["p"] = res[L:2 * L]
        for c in chains:
            c["tp"] = c["tp"] + _bdot(c["tp"], unpack(c["p"]))
        for c in chains:
            c["uw"] = _bdot(unpack(c["tp"]), c["rhs"]).astype(BF16)
        for c in chains:
            au = jnp.dot(c["attn"], c["uw"], preferred_element_type=F32)
            o_ref[c["hd"], pl.ds(c["r0"], G), :] = au[:, 0:D_DIM]
            c["qe"] = (c["qg"] - au[:, D_DIM:2 * D_DIM]).astype(BF16)
        for c in chains:
            for ci in range(D_PER_GROUP):
                idx = c["hd"] * nc + c["gi"] * D_PER_GROUP + ci
                ku = _bdot(jnp.where(tchunk == ci, c["kdt"], 0.0), c["uw"])
                n_ref[idx] = ku[:, 0:D_DIM]
                kq_ref[idx, 0:D_DIM, :] = (-ku[:, D_DIM:2 * D_DIM]).astype(BF16)
                kq_ref[idx, D_DIM:D_DIM + L, :] = c["qe"][ci * L:(ci + 1) * L]
                dec_ref[idx] = jnp.broadcast_to(c["dec"][ci * L:ci * L + 8], (8, D_DIM))

    for hh in range(D_HEADS_PER_STEP):
        conv_into(hh, q_ref, cwq_ref, qs_ref, True, D_DIM ** -0.5)
        conv_into(hh, k_ref, cwk_ref, ks_ref, True, 1.0)
        conv_into(hh, v_ref, cwv_ref, vs_ref, False, 1.0)

        def group_batch(i, _, hh=hh):
            groups(hh, [D_GROUPS_PER_ITER * i + t for t in range(D_GROUPS_PER_ITER)])
            return 0

        lax.fori_loop(0, ng // D_GROUPS_PER_ITER, group_batch, 0)

    st_ref[...] = jnp.zeros(st_ref.shape, F32)

    def step(j, _):
        for hd in range(D_HEADS_PER_STEP * N_DIR):
            c = j if hd % N_DIR == 0 else nc - 1 - j
            r0 = pl.multiple_of(c * L, L)
            idx = hd * nc + c
            st = st_ref[hd]
            res = jnp.dot(kq_ref[idx], st.astype(BF16), preferred_element_type=F32)
            st_ref[hd] = dec_ref[idx][0:1, :] * st + res[0:D_DIM] + n_ref[idx]
            o_ref[hd, pl.ds(r0, L), :] = o_ref[hd, pl.ds(r0, L), :] + res[D_DIM:D_DIM + L]
        return 0

    lax.fori_loop(0, nc, step, 0)

    def finish(gi, _):
        r0 = pl.multiple_of(gi * G, G)
        for hh in range(D_HEADS_PER_STEP):
            lanes = slice(hh * D_DIM, (hh + 1) * D_DIM)
            o = o_ref[hh * N_DIR, pl.ds(r0, G), :] + o_ref[hh * N_DIR + 1, pl.ds(r0, G), :]
            o = o * lax.rsqrt(jnp.mean(o * o, axis=-1, keepdims=True) + RMS_EPS) * ng_ref[...]
            z = z_ref[0, pl.ds(r0, G), lanes]
            y_ref[0, pl.ds(r0, G), lanes] = (o * (z * _sigmoid(z))).astype(y_ref.dtype)
        return 0

    lax.fori_loop(0, ng, finish, 0)


def _delta(proj3, conv_w, gcol, grow, norm_g, off_q, off_k, off_v, off_z):
    b, s, _ = proj3.shape
    w = D_HEADS * D_DIM
    hb = D_HEADS_PER_STEP
    wb = hb * D_DIM
    assert s % (D_GROUP * D_GROUPS_PER_ITER) == 0 and D_HEADS % hb == 0
    bq, bk, bv, bz = (o // wb for o in (off_q, off_k, off_v, off_z))
    nsteps = D_HEADS // hb
    nc = s // D_CHUNK
    nch = hb * N_DIR
    return pl.pallas_call(
        _delta_kernel,
        grid=(b, nsteps),
        in_specs=[
            pl.BlockSpec((1, s, wb), lambda i, h: (i, 0, bq + h)),
            pl.BlockSpec((1, s, wb), lambda i, h: (i, 0, bk + h)),
            pl.BlockSpec((1, s, wb), lambda i, h: (i, 0, bv + h)),
            pl.BlockSpec((1, s, wb), lambda i, h: (i, 0, bz + h)),
            pl.BlockSpec((CONV_W, wb), lambda i, h: (0, h)),
            pl.BlockSpec((CONV_W, wb), lambda i, h: (0, nsteps + h)),
            pl.BlockSpec((CONV_W, wb), lambda i, h: (0, 2 * nsteps + h)),
            pl.BlockSpec((1, s, LANES), lambda i, h: (i, 0, 0)),
            pl.BlockSpec((1, LANES, s), lambda i, h: (i, 0, 0)),
            pl.BlockSpec((1, D_DIM), lambda i, h: (0, 0)),
        ],
        out_specs=pl.BlockSpec((1, s, wb), lambda i, h: (i, 0, h)),
        out_shape=jax.ShapeDtypeStruct((b, s, w), BF16),
        scratch_shapes=[
            pltpu.VMEM((s + 2 * CONV_HALO, D_DIM), F32),
            pltpu.VMEM((s, D_DIM), F32),
            pltpu.VMEM((s, D_DIM), F32),
            pltpu.VMEM((s, D_DIM), F32),
            pltpu.VMEM((nch * nc, D_DIM + D_CHUNK, D_DIM), BF16),
            pltpu.VMEM((nch * nc, D_DIM, D_DIM), F32),
            pltpu.VMEM((nch * nc, 8, D_DIM), F32),
            pltpu.VMEM((nch, s, D_DIM), F32),
            pltpu.VMEM((nch, D_DIM, D_DIM), F32),
        ],
        compiler_params=_cparams(("parallel", "arbitrary")),
        name="delta",
    )(proj3, proj3, proj3, proj3, conv_w, conv_w, conv_w, gcol, grow, norm_g)


def _mix_kernel(x_ref, ym_ref, yd_ref, gm_ref, gd_ref, wbm_ref, wbd_ref, wo_ref, o_ref):
    a = jnp.dot(ym_ref[...], wbm_ref[...], preferred_element_type=F32)
    b = jnp.dot(yd_ref[...], wbd_ref[...], preferred_element_type=F32)
    mixed = _sigmoid(gm_ref[...]) * a + _sigmoid(gd_ref[...]) * b
    o_ref[...] = x_ref[...] + jnp.dot(mixed.astype(BF16), wo_ref[...], preferred_element_type=F32)


def _mix(x2, ym2, yd2, proj2, wbm, wbd, wo, tm):
    m, d = x2.shape
    const = lambda i: (0, 0)
    return pl.pallas_call(
        _mix_kernel,
        grid=(m // tm,),
        in_specs=[
            pl.BlockSpec((tm, d), lambda i: (i, 0)),
            pl.BlockSpec((tm, ym2.shape[1]), lambda i: (i, 0)),
            pl.BlockSpec((tm, yd2.shape[1]), lambda i: (i, 0)),
            pl.BlockSpec((tm, d), lambda i: (i, 0)),
            pl.BlockSpec((tm, d), lambda i: (i, 1)),
            pl.BlockSpec(wbm.shape, const),
            pl.BlockSpec(wbd.shape, const),
            pl.BlockSpec(wo.shape, const),
        ],
        out_specs=pl.BlockSpec((tm, d), lambda i: (i, 0)),
        out_shape=jax.ShapeDtypeStruct((m, d), F32),
        compiler_params=_cparams(("parallel",)),
        name="mix",
    )(x2, ym2, yd2, proj2, proj2, wbm, wbd, wo)


def _ffn_kernel(x_ref, g2_ref, gf_ref, w1_ref, w2_ref, o_ref, h_ref, acc_ref):
    f = pl.program_id(1)

    @pl.when(f == 0)
    def _():
        x = x_ref[...]
        ms = jnp.mean(x * x, axis=-1, keepdims=True)
        h_ref[...] = (x * lax.rsqrt(ms + RMS_EPS) * g2_ref[...]).astype(BF16)
        acc_ref[...] = jnp.zeros(acc_ref.shape, F32)

    hid = jnp.dot(h_ref[...], w1_ref[...], preferred_element_type=F32)
    act = jnp.square(jnp.maximum(hid, 0.0)).astype(BF16)
    acc_ref[...] += jnp.dot(act, w2_ref[...], preferred_element_type=F32)

    @pl.when(f == pl.num_programs(1) - 1)
    def _():
        x = x_ref[...] + acc_ref[...]
        ms = jnp.mean(x * x, axis=-1, keepdims=True)
        o_ref[...] = x * lax.rsqrt(ms + RMS_EPS) * gf_ref[...]


def _ffn(x2, g2, gf, w1, w2, tm, tf):
    m, d = x2.shape
    dff = w1.shape[1]
    return pl.pallas_call(
        _ffn_kernel,
        grid=(m // tm, dff // tf),
        in_specs=[
            pl.BlockSpec((tm, d), lambda i, f: (i, 0)),
            pl.BlockSpec((1, d), lambda i, f: (0, 0)),
            pl.BlockSpec((1, d), lambda i, f: (0, 0)),
            pl.BlockSpec((d, tf), lambda i, f: (0, f)),
            pl.BlockSpec((tf, d), lambda i, f: (f, 0)),
        ],
        out_specs=pl.BlockSpec((tm, d), lambda i, f: (i, 0)),
        out_shape=jax.ShapeDtypeStruct((m, d), F32),
        scratch_shapes=[pltpu.VMEM((tm, d), BF16), pltpu.VMEM((tm, d), F32)],
        compiler_params=_cparams(("parallel", "arbitrary")),
        name="ffn",
    )(x2, g2, gf, w1, w2)


def _pick_tile(n, candidates):
    for c in candidates:
        if n % c == 0:
            return c
    raise ValueError(f"no tile in {candidates} divides {n}")


def _layer(x, norm1_g, w_in, i_bias, f_bias, m_norm_g, conv_w, a_log, dt_bias, d_norm_g,
           w_bm, w_bd, w_out, norm2_g, w_ff1, w_ff2, out_g):
    b, s, d = x.shape
    m = b * s
    mw, mqk, dw = M_HEADS * M_DV, M_HEADS * M_DK, D_HEADS * D_DIM
    nm, nd = N_DIR * M_HEADS, N_DIR * D_HEADS
    splits = (mqk, mqk, mw, mw, nm, nm, 3 * dw, dw, nd, nd, 2 * d)
    bounds = [0]
    for w_ in splits:
        bounds.append(bounds[-1] + w_)
    cols = [w_in[:, bounds[i]:bounds[i + 1]] for i in range(len(splits))]
    (c_mq, c_mk, c_mv, c_mo, c_mi, c_mf, c_dqkv, c_dz, c_da, c_db, c_gates) = cols
    w_main = jnp.concatenate([c_gates, c_mq, c_mk, c_mv, c_mo, c_dqkv, c_dz], axis=1).astype(BF16)
    off_mq = 2 * d
    off_mk = off_mq + mqk
    off_mv = off_mk + mqk
    off_mo = off_mv + mw
    off_dq = off_mo + mw
    off_dk = off_dq + dw
    off_dv = off_dk + dw
    off_dz = off_dv + dw
    n_small = 2 * nm + 2 * nd
    w_small = jnp.concatenate([c_mi, c_mf, c_da, c_db, jnp.zeros((d, LANES - n_small), F32)], axis=1).astype(BF16)
    zpad = lambda n: jnp.zeros((n,), F32)
    bias_row = jnp.concatenate([i_bias.reshape(-1), f_bias.reshape(-1), dt_bias.reshape(-1), zpad(LANES - CH_B)])
    alog_row = jnp.concatenate([zpad(CH_G), a_log.reshape(-1), zpad(LANES - CH_B)])
    gate_params = jnp.concatenate([bias_row[None], alog_row[None], jnp.zeros((6, LANES), F32)], axis=0)

    x2 = x.reshape(m, d)
    tm = _pick_tile(m, (512, 256, 128))
    tn = _pick_tile(w_main.shape[1], (1024, 512, 256))
    proj2, smalls2 = _inproj(x2, norm1_g.reshape(1, d), w_main, w_small, _pick_tile(m, (1024, 512, 256)), tn)
    proj3 = proj2.reshape(b, s, -1)
    gcol, grow = _gate_prep(smalls2.reshape(b, s, LANES), gate_params)

    ym = _mlstm(proj3, gcol, grow, m_norm_g.reshape(M_HEADS, 1, M_DV), off_mq, off_mk, off_mv, off_mo)
    yd = _delta(proj3, conv_w, gcol, grow, d_norm_g.reshape(1, D_DIM), off_dq, off_dk, off_dv, off_dz)

    x1 = _mix(x2, ym.reshape(m, mw), yd.reshape(m, dw), proj2,
              w_bm.astype(BF16), w_bd.astype(BF16), w_out.astype(BF16), _pick_tile(m, (256, 128)))
    tf = _pick_tile(w_ff1.shape[1], (1024, 512, 256))
    out = _ffn(x1, norm2_g.reshape(1, d), out_g.reshape(1, d), w_ff1.astype(BF16), w_ff2.astype(BF16), tm, tf)
    return out.reshape(b, s, d)


def kernel(x, norm1_g, w_in, mlstm_i_bias, mlstm_f_bias, mlstm_norm_g, delta_conv_w, delta_a_log,
           delta_dt_bias, delta_norm_g, w_branch_m, w_branch_d, w_out, norm2_g, w_ff1, w_ff2, norm_f_g):
    depth = w_in.shape[0]
    assert depth == 1, "the fused FFN + final-norm epilogue assumes a single layer"
    return _layer(x, norm1_g[0], w_in[0], mlstm_i_bias[0], mlstm_f_bias[0], mlstm_norm_g[0], delta_conv_w[0],
                  delta_a_log[0], delta_dt_bias[0], delta_norm_g[0], w_branch_m[0], w_branch_d[0], w_out[0],
                  norm2_g[0], w_ff1[0], w_ff2[0], norm_f_g)
```

```python
import functools
import math

import jax
import jax.numpy as jnp
from jax import lax
from jax.experimental import pallas as pl
from jax.experimental.pallas import tpu as pltpu

F32 = jnp.float32
BF16 = jnp.bfloat16
I32 = jnp.int32

N_DIR = 2
M_HEADS = 4
M_DK = 128
M_DV = 256
D_HEADS = 8
D_DIM = 128
CONV_W = 5
RMS_EPS = 1e-6
L2_EPS = 1e-6

LANES = 128
BF16_SUBLANES = 16
M_CHUNK = 256
M_CHUNKS_PER_ITER = 4
D_CHUNK = 64
D_GROUP = 256
D_PER_GROUP = D_GROUP // D_CHUNK
D_HEADS_PER_STEP = 2
D_GROUPS_PER_ITER = 4
GATE_ROWS = 256
CONV_HALO = 8

CH_I = 0
CH_F = 8
CH_G = 16
CH_B = 32
CH_T = 48

VMEM_LIMIT = 56 * 1024 * 1024

_NT = (((1,), (1,)), ((), ()))


def _cparams(sem):
    return pltpu.CompilerParams(dimension_semantics=sem, vmem_limit_bytes=VMEM_LIMIT)


def _bdot(a, b):
    return jnp.dot(a.astype(BF16), b.astype(BF16), preferred_element_type=F32)


def _bdot_nt(a, b):
    return lax.dot_general(a.astype(BF16), b.astype(BF16), _NT, preferred_element_type=F32)


def _sigmoid(x):
    return 1.0 / (1.0 + jnp.exp(-x))


def _softplus(x):
    return jnp.maximum(x, 0.0) + jnp.log1p(jnp.exp(-jnp.abs(x)))


def _lane_pick(x, ch):
    lane = lax.broadcasted_iota(I32, x.shape, 1)
    return jnp.sum(jnp.where(lane == ch, x, 0.0), axis=1, keepdims=True)


def _inproj_kernel(x_ref, g_ref, w_ref, ws_ref, o_ref, os_ref, hn_ref):
    @pl.when(pl.program_id(1) == 0)
    def _():
        x = x_ref[...]
        ms = jnp.mean(x * x, axis=-1, keepdims=True)
        hn = (x * lax.rsqrt(ms + RMS_EPS) * g_ref[...]).astype(BF16)
        hn_ref[...] = hn
        os_ref[...] = lax.dot_general(hn, ws_ref[...], _NT, preferred_element_type=F32)

    o_ref[...] = lax.dot_general(hn_ref[...], w_ref[...], _NT, preferred_element_type=F32)


def _inproj(x2, g, wt, wt_small, seg_bounds, tm, tn):
    m, d = x2.shape
    starts = [0]
    for lo, hi in seg_bounds:
        starts.append(starts[-1] + (hi - lo) // tn)

    def w_row(i, j):
        row = seg_bounds[0][0] + j * tn
        for k in range(1, len(seg_bounds)):
            row = jnp.where(j >= starts[k], seg_bounds[k][0] + (j - starts[k]) * tn, row)
        return pl.multiple_of(row, BF16_SUBLANES), 0

    return pl.pallas_call(
        _inproj_kernel,
        grid=(m // tm, starts[-1]),
        in_specs=[
            pl.BlockSpec((tm, d), lambda i, j: (i, 0)),
            pl.BlockSpec((1, d), lambda i, j: (0, 0)),
            pl.BlockSpec((pl.Element(tn), pl.Element(d)), w_row),
            pl.BlockSpec((LANES, d), lambda i, j: (0, 0)),
        ],
        out_specs=[
            pl.BlockSpec((tm, tn), lambda i, j: (i, j)),
            pl.BlockSpec((tm, LANES), lambda i, j: (i, 0)),
        ],
        out_shape=[jax.ShapeDtypeStruct((m, starts[-1] * tn), F32), jax.ShapeDtypeStruct((m, LANES), F32)],
        scratch_shapes=[pltpu.VMEM((tm, d), BF16)],
        compiler_params=_cparams(("parallel", "arbitrary")),
        name="inproj",
    )(x2, g, wt, wt_small)


def _gate_kernel(sm_ref, par_ref, col_ref, row_ref):
    r = GATE_ROWS
    x = sm_ref[0] + par_ref[0:1, :]
    lane = lax.broadcasted_iota(I32, x.shape, 1)
    logf = -_softplus(-x)
    g = -jnp.exp(par_ref[1:2, :]) * _softplus(x)
    beta = _sigmoid(x)
    xe = jnp.where(lane < CH_F, x,
                   jnp.where(lane < CH_G, logf,
                             jnp.where(lane < CH_B, g,
                                       jnp.where(lane < CH_T, beta, 0.0))))
    hi = xe.astype(BF16)
    r1 = xe - hi.astype(F32)
    mid = r1.astype(BF16)
    lo = (r1 - mid.astype(F32)).astype(BF16)
    x3 = jnp.concatenate([hi, mid, lo], axis=1)
    rr = lax.broadcasted_iota(I32, (r, r), 0)
    cc = lax.broadcasted_iota(I32, (r, r), 1)
    low = jnp.where(cc <= rr, 1.0, 0.0)
    upp = jnp.where(cc >= rr, 1.0, 0.0)
    same_d = jnp.where((rr // D_CHUNK) == (cc // D_CHUNK), 1.0, 0.0)
    same_m = jnp.where((rr // M_CHUNK) == (cc // M_CHUNK), 1.0, 0.0)
    mats = jnp.concatenate([
        low * same_m,
        upp * same_m,
        low * same_d,
        upp * same_d,
        same_d,
    ], axis=0).astype(BF16)
    y3 = jnp.dot(mats, x3, preferred_element_type=F32)
    y = y3[:, 0:LANES] + y3[:, LANES:2 * LANES] + y3[:, 2 * LANES:3 * LANES]
    pm, sm_, pd, sd, td = (y[i * r:(i + 1) * r] for i in range(5))
    half_m = CH_F + M_HEADS
    half_d = CH_G + D_HEADS
    out = jnp.where(lane < CH_F, xe,
          jnp.where(lane < half_m, pm,
          jnp.where(lane < CH_G, sm_,
          jnp.where(lane < half_d, pd,
          jnp.where(lane < CH_B, sd,
          jnp.where(lane < CH_T, xe,
          jnp.where(lane < CH_T + N_DIR * D_HEADS, pltpu.roll(td, CH_T - CH_G, axis=1), 0.0)))))))
    col_ref[0] = out
    row_ref[0] = out.T


def _gate_prep(smalls3, params):
    b, s, _ = smalls3.shape
    r = GATE_ROWS
    return pl.pallas_call(
        _gate_kernel,
        grid=(b, s // r),
        in_specs=[
            pl.BlockSpec((1, r, LANES), lambda i, j: (i, j, 0)),
            pl.BlockSpec((8, LANES), lambda i, j: (0, 0)),
        ],
        out_specs=[
            pl.BlockSpec((1, r, LANES), lambda i, j: (i, j, 0)),
            pl.BlockSpec((1, LANES, r), lambda i, j: (i, 0, j)),
        ],
        out_shape=[jax.ShapeDtypeStruct((b, s, LANES), F32), jax.ShapeDtypeStruct((b, LANES, s), F32)],
        compiler_params=_cparams(("parallel", "parallel")),
        name="gateprep",
    )(smalls3, params)


def _mlstm_kernel(q_ref, k_ref, v_ref, o_ref, gc_ref, gr_ref, ng_ref, y_ref,
                  hn_ref, den_ref, mi_ref, cu_ref, nu_ref, bl_ref, ml_ref, cs_ref, ns_ref, ms_ref, c_ref):
    head = pl.program_id(1)
    s = q_ref.shape[1]
    L = M_CHUNK
    nc = s // L
    scale = M_DK ** -0.5
    rr = lax.broadcasted_iota(I32, (L, L), 0)
    cc = lax.broadcasted_iota(I32, (L, L), 1)

    def bcast8(x11):
        return jnp.broadcast_to(x11, (8, LANES))

    def local(cs):
        chains = []
        for c in cs:
            r0 = pl.multiple_of(c * L, L)
            q = q_ref[0, pl.ds(r0, L), :]
            k = k_ref[0, pl.ds(r0, L), :] * scale
            vb = v_ref[0, pl.ds(r0, L), :].astype(BF16)
            kb = k.astype(BF16)
            qk = lax.dot_general(q.astype(BF16), kb, _NT, preferred_element_type=F32)
            k_t = k.T
            gcol = gc_ref[0, pl.ds(r0, L), :]
            for d in range(N_DIR):
                ch_i = CH_I + d * M_HEADS + head
                ch_f = CH_F + d * M_HEADS + head
                bc_col = _lane_pick(gcol, ch_f)
                i_row = gr_ref[0, pl.ds(ch_i, 1), pl.ds(r0, L)]
                bc_row = gr_ref[0, pl.ds(ch_f, 1), pl.ds(r0, L)]
                a_row = i_row - bc_row
                mask = (cc <= rr) if d == 0 else (cc >= rr)
                log_intra = jnp.where(mask, bc_col + a_row, -jnp.inf)
                m_intra = jnp.max(log_intra, axis=1, keepdims=True)
                p = qk * jnp.exp(log_intra - m_intra)
                b_last = bc_row[:, L - 1:L] if d == 0 else bc_row[:, 0:1]
                log_state = b_last + a_row
                m_loc = jnp.max(log_state, axis=1, keepdims=True)
                w_state = jnp.exp(log_state - m_loc)
                chains.append(dict(
                    idx=d * nc + c, d=d, r0=r0, vb=vb, kb=kb, pb=p.astype(BF16),
                    den=jnp.sum(p, axis=1, keepdims=True), m_intra=m_intra,
                    kw=(k_t * w_state).astype(BF16), w8=jnp.broadcast_to(w_state, (8, L)).astype(BF16),
                    b_last=b_last, m_loc=m_loc))
        for c in chains:
            hn_ref[c["d"], pl.ds(c["r0"], L), :] = jnp.dot(c["pb"], c["vb"], preferred_element_type=F32)
            den_ref[c["d"], pl.ds(c["r0"], L), :] = jnp.broadcast_to(c["den"], (L, LANES))
            mi_ref[c["d"], pl.ds(c["r0"], L), :] = jnp.broadcast_to(c["m_intra"], (L, LANES))
        for c in chains:
            cu_ref[c["idx"]] = jnp.dot(c["kw"], c["vb"], preferred_element_type=F32)
            nu_ref[c["idx"]] = jnp.dot(c["w8"], c["kb"], preferred_element_type=F32)
            bl_ref[c["idx"]] = bcast8(c["b_last"])
            ml_ref[c["idx"]] = bcast8(c["m_loc"])

    def local_trip(i, _):
        local([M_CHUNKS_PER_ITER * i + t for t in range(M_CHUNKS_PER_ITER)])
        return 0

    lax.fori_loop(0, nc // M_CHUNKS_PER_ITER, local_trip, 0)

    c_ref[...] = jnp.zeros(c_ref.shape, F32)

    def scan(j, carry):
        out = []
        for d in range(N_DIR):
            m_st, n_st = carry[d]
            idx = d * nc + (j if d == 0 else nc - 1 - j)
            c_st = c_ref[d]
            cs_ref[idx] = c_st.astype(BF16)
            ns_ref[idx] = n_st
            ms_ref[idx] = m_st
            bl = bl_ref[idx]
            ml = ml_ref[idx]
            m_new = jnp.maximum(bl + m_st, ml)
            decay = jnp.exp(bl + m_st - m_new)
            gain = jnp.exp(ml - m_new)
            c_ref[d] = decay[0:1, 0:1] * c_st + gain[0:1, 0:1] * cu_ref[idx]
            out.append((m_new, decay * n_st + gain * nu_ref[idx]))
        return tuple(out)

    m0 = jnp.full((8, LANES), -jnp.inf, F32)
    n0 = jnp.zeros((8, M_DK), F32)
    lax.fori_loop(0, nc, scan, ((m0, n0), (m0, n0)))

    def wide(x):
        return jnp.concatenate([x] * (M_DV // LANES), axis=1)

    def combine(c, _):
        r0 = pl.multiple_of(c * L, L)
        q = q_ref[0, pl.ds(r0, L), :]
        qb = q.astype(BF16)
        gcol = gc_ref[0, pl.ds(r0, L), :]
        hh = None
        for d in range(N_DIR):
            idx = d * nc + c
            bc = jnp.broadcast_to(_lane_pick(gcol, CH_F + d * M_HEADS + head), (L, LANES))
            den_i = den_ref[d, pl.ds(r0, L), :]
            m_i = mi_ref[d, pl.ds(r0, L), :]
            log_inter = bc + ms_ref[idx][0:1, :]
            m_row = jnp.maximum(log_inter, m_i)
            w_i = jnp.exp(m_i - m_row)
            w_x = jnp.exp(log_inter - m_row)
            q_c = jnp.dot(qb, cs_ref[idx], preferred_element_type=F32)
            q_n = jnp.broadcast_to(jnp.sum(q * ns_ref[idx][0:1, :], axis=1, keepdims=True), (L, LANES))
            num = wide(w_i) * hn_ref[d, pl.ds(r0, L), :] + wide(w_x) * q_c
            den = w_i * den_i + w_x * q_n
            h_d = num * wide(1.0 / jnp.maximum(jnp.abs(den), jnp.exp(-m_row)))
            hh = h_d if hh is None else hh + h_d
        hh = hh * lax.rsqrt(jnp.mean(hh * hh, axis=-1, keepdims=True) + RMS_EPS)
        hh = hh * ng_ref[0]
        y_ref[0, pl.ds(r0, L), :] = (_sigmoid(o_ref[0, pl.ds(r0, L), :]) * hh).astype(y_ref.dtype)
        return 0

    lax.fori_loop(0, nc, combine, 0)


def _mlstm(proj3, gcol, grow, norm_g, off_q, off_k, off_v, off_o):
    b, s, _ = proj3.shape
    assert s % (M_CHUNK * M_CHUNKS_PER_ITER) == 0
    nc = s // M_CHUNK
    bq, bk = off_q // M_DK, off_k // M_DK
    bv, bo = off_v // M_DV, off_o // M_DV
    return pl.pallas_call(
        _mlstm_kernel,
        grid=(b, M_HEADS),
        in_specs=[
            pl.BlockSpec((1, s, M_DK), lambda i, h: (i, 0, bq + h)),
            pl.BlockSpec((1, s, M_DK), lambda i, h: (i, 0, bk + h)),
            pl.BlockSpec((1, s, M_DV), lambda i, h: (i, 0, bv + h)),
            pl.BlockSpec((1, s, M_DV), lambda i, h: (i, 0, bo + h)),
            pl.BlockSpec((1, s, LANES), lambda i, h: (i, 0, 0)),
            pl.BlockSpec((1, LANES, s), lambda i, h: (i, 0, 0)),
            pl.BlockSpec((1, 1, M_DV), lambda i, h: (h, 0, 0)),
        ],
        out_specs=pl.BlockSpec((1, s, M_DV), lambda i, h: (i, 0, h)),
        out_shape=jax.ShapeDtypeStruct((b, s, M_HEADS * M_DV), BF16),
        scratch_shapes=[
            pltpu.VMEM((N_DIR, s, M_DV), F32),
            pltpu.VMEM((N_DIR, s, LANES), F32),
            pltpu.VMEM((N_DIR, s, LANES), F32),
            pltpu.VMEM((N_DIR * nc, M_DK, M_DV), F32),
            pltpu.VMEM((N_DIR * nc, 8, M_DK), F32),
            pltpu.VMEM((N_DIR * nc, 8, LANES), F32),
            pltpu.VMEM((N_DIR * nc, 8, LANES), F32),
            pltpu.VMEM((N_DIR * nc, M_DK, M_DV), BF16),
            pltpu.VMEM((N_DIR * nc, 8, M_DK), F32),
            pltpu.VMEM((N_DIR * nc, 8, LANES), F32),
            pltpu.VMEM((N_DIR, M_DK, M_DV), F32),
        ],
        compiler_params=_cparams(("parallel", "arbitrary")),
        name="mlstm",
    )(proj3, proj3, proj3, proj3, gcol, grow, norm_g)


def _delta_kernel(q_ref, k_ref, v_ref, z_ref, cwq_ref, cwk_ref, cwv_ref, gc_ref, gr_ref, ng_ref, y_ref,
                  xp_ref, qs_ref, ks_ref, vs_ref, kq_ref, n_ref, dec_ref, o_ref, st_ref):
    head0 = pl.program_id(1) * D_HEADS_PER_STEP
    s = q_ref.shape[1]
    L = D_CHUNK
    G = D_GROUP
    nc = s // L
    ng = s // G
    halo = CONV_HALO
    pad = CONV_W // 2

    xp_ref[0:halo, :] = jnp.zeros((halo, D_DIM), F32)
    xp_ref[s + halo:s + 2 * halo, :] = jnp.zeros((halo, D_DIM), F32)

    def conv_into(hh, x_ref, cw_ref, dst_ref, l2, mult):
        lanes = slice(hh * D_DIM, (hh + 1) * D_DIM)
        xp_ref[halo:s + halo, :] = x_ref[0, :, lanes]
        for blk in range(ng):
            base = halo + blk * G - pad
            acc = xp_ref[base:base + G, :] * cw_ref[0:1, lanes]
            for j in range(1, CONV_W):
                acc = acc + xp_ref[base + j:base + j + G, :] * cw_ref[j:j + 1, lanes]
            y = acc * _sigmoid(acc)
            if l2:
                y = y * lax.rsqrt(jnp.sum(y * y, axis=-1, keepdims=True) + L2_EPS)
            if mult != 1.0:
                y = y * mult
            dst_ref[blk * G:(blk + 1) * G, :] = y

    tchunk = lax.broadcasted_iota(I32, (D_DIM, D_GROUP), 1) // D_CHUNK
    rr = lax.broadcasted_iota(I32, (G, G), 0)
    cc = lax.broadcasted_iota(I32, (G, G), 1)
    same = (rr // L) == (cc // L)
    pr = lax.broadcasted_iota(I32, (L, G), 0)
    pc = lax.broadcasted_iota(I32, (L, G), 1)
    eye_p = jnp.where((pc % L) == pr, 1.0, 0.0).astype(F32)

    def pack(m):
        out = m[0:L]
        for i in range(1, D_PER_GROUP):
            out = out + m[i * L:(i + 1) * L]
        return out

    def unpack(p):
        return jnp.where(same, jnp.concatenate([p] * D_PER_GROUP, axis=0), 0.0)

    def groups(hh, gis):
        chains = []
        for gi in gis:
            r0 = pl.multiple_of(gi * G, G)
            kk_ = ks_ref[pl.ds(r0, G), :]
            qq_ = qs_ref[pl.ds(r0, G), :]
            vv_ = vs_ref[pl.ds(r0, G), :]
            kb = kk_.astype(BF16)
            kk = lax.dot_general(kb, kb, _NT, preferred_element_type=F32)
            qk = lax.dot_general(qq_.astype(BF16), kb, _NT, preferred_element_type=F32)
            gcol = gc_ref[0, pl.ds(r0, G), :]
            for d in range(N_DIR):
                ch = d * D_HEADS + head0 + hh
                g_col = _lane_pick(gcol, CH_G + ch)
                b_col = _lane_pick(gcol, CH_B + ch)
                t_col = _lane_pick(gcol, CH_T + ch)
                g_row = gr_ref[0, pl.ds(CH_G + ch, 1), pl.ds(r0, G)]
                tri = (cc <= rr) if d == 0 else (cc >= rr)
                strict = (cc < rr) if d == 0 else (cc > rr)
                gam = jnp.exp(jnp.where(tri, jnp.where(same, g_col - g_row, -jnp.inf), -jnp.inf))
                x = jnp.where(strict, -(b_col * kk * gam), 0.0)
                eg = jnp.exp(g_col)
                chains.append(dict(
                    gi=gi, hd=hh * N_DIR + d, r0=r0, x=x, xp=pack(x), attn=(qk * gam).astype(BF16),
                    rhs=jnp.concatenate([b_col * vv_, (b_col * eg) * kk_], axis=1).astype(BF16),
                    qg=qq_ * eg, kdt=(kk_ * jnp.exp(t_col - g_col)).T, dec=jnp.exp(t_col)))
        for c in chains:
            c["tp"] = eye_p + c["xp"]
            c["p"] = _bdot(c["xp"], c["x"])
        for _it in range(4):
            for c in chains:
                res = _bdot(jnp.concatenate([c["tp"], c["p"]], axis=0), unpack(c["p"]))
                c["tp"] = c["tp"] + res[0:L]
                c["p"] = res[L:2 * L]
        for c in chains:
            c["tp"] = c["tp"] + _bdot(c["tp"], unpack(c["p"]))
        for c in chains:
            c["uw"] = _bdot(unpack(c["tp"]), c["rhs"]).astype(BF16)
        for c in chains:
            au = jnp.dot(c["attn"], c["uw"], preferred_element_type=F32)
            o_ref[c["hd"], pl.ds(c["r0"], G), :] = au[:, 0:D_DIM]
            c["qe"] = (c["qg"] - au[:, D_DIM:2 * D_DIM]).astype(BF16)
        for c in chains:
            for ci in range(D_PER_GROUP):
                idx = c["hd"] * nc + c["gi"] * D_PER_GROUP + ci
                ku = _bdot(jnp.where(tchunk == ci, c["kdt"], 0.0), c["uw"])
                n_ref[idx] = ku[:, 0:D_DIM]
                kq_ref[idx, 0:D_DIM, :] = (-ku[:, D_DIM:2 * D_DIM]).astype(BF16)
                kq_ref[idx, D_DIM:D_DIM + L, :] = c["qe"][ci * L:(ci + 1) * L]
                dec_ref[idx] = jnp.broadcast_to(c["dec"][ci * L:ci * L + 8], (8, D_DIM))

    for hh in range(D_HEADS_PER_STEP):
        conv_into(hh, q_ref, cwq_ref, qs_ref, True, D_DIM ** -0.5)
        conv_into(hh, k_ref, cwk_ref, ks_ref, True, 1.0)
        conv_into(hh, v_ref, cwv_ref, vs_ref, False, 1.0)

        def group_batch(i, _, hh=hh):
            groups(hh, [D_GROUPS_PER_ITER * i + t for t in range(D_GROUPS_PER_ITER)])
            return 0

        lax.fori_loop(0, ng // D_GROUPS_PER_ITER, group_batch, 0)

    st_ref[...] = jnp.zeros(st_ref.shape, F32)

    def step(j, _):
        for hd in range(D_HEADS_PER_STEP * N_DIR):
            c = j if hd % N_DIR == 0 else nc - 1 - j
            r0 = pl.multiple_of(c * L, L)
            idx = hd * nc + c
            st = st_ref[hd]
            res = jnp.dot(kq_ref[idx], st.astype(BF16), preferred_element_type=F32)
            st_ref[hd] = dec_ref[idx][0:1, :] * st + res[0:D_DIM] + n_ref[idx]
            o_ref[hd, pl.ds(r0, L), :] = o_ref[hd, pl.ds(r0, L), :] + res[D_DIM:D_DIM + L]
        return 0

    lax.fori_loop(0, nc, step, 0)

    def finish(gi, _):
        r0 = pl.multiple_of(gi * G, G)
        for hh in range(D_HEADS_PER_STEP):
            lanes = slice(hh * D_DIM, (hh + 1) * D_DIM)
            o = o_ref[hh * N_DIR, pl.ds(r0, G), :] + o_ref[hh * N_DIR + 1, pl.ds(r0, G), :]
            o = o * lax.rsqrt(jnp.mean(o * o, axis=-1, keepdims=True) + RMS_EPS) * ng_ref[...]
            z = z_ref[0, pl.ds(r0, G), lanes]
            y_ref[0, pl.ds(r0, G), lanes] = (o * (z * _sigmoid(z))).astype(y_ref.dtype)
        return 0

    lax.fori_loop(0, ng, finish, 0)


def _delta(proj3, conv_w, gcol, grow, norm_g, off_q, off_k, off_v, off_z):
    b, s, _ = proj3.shape
    w = D_HEADS * D_DIM
    hb = D_HEADS_PER_STEP
    wb = hb * D_DIM
    assert s % (D_GROUP * D_GROUPS_PER_ITER) == 0 and D_HEADS % hb == 0
    bq, bk, bv, bz = (o // wb for o in (off_q, off_k, off_v, off_z))
    nsteps = D_HEADS // hb
    nc = s // D_CHUNK
    nch = hb * N_DIR
    return pl.pallas_call(
        _delta_kernel,
        grid=(b, nsteps),
        in_specs=[
            pl.BlockSpec((1, s, wb), lambda i, h: (i, 0, bq + h)),
            pl.BlockSpec((1, s, wb), lambda i, h: (i, 0, bk + h)),
            pl.BlockSpec((1, s, wb), lambda i, h: (i, 0, bv + h)),
            pl.BlockSpec((1, s, wb), lambda i, h: (i, 0, bz + h)),
            pl.BlockSpec((CONV_W, wb), lambda i, h: (0, h)),
            pl.BlockSpec((CONV_W, wb), lambda i, h: (0, nsteps + h)),
            pl.BlockSpec((CONV_W, wb), lambda i, h: (0, 2 * nsteps + h)),
            pl.BlockSpec((1, s, LANES), lambda i, h: (i, 0, 0)),
            pl.BlockSpec((1, LANES, s), lambda i, h: (i, 0, 0)),
            pl.BlockSpec((1, D_DIM), lambda i, h: (0, 0)),
        ],
        out_specs=pl.BlockSpec((1, s, wb), lambda i, h: (i, 0, h)),
        out_shape=jax.ShapeDtypeStruct((b, s, w), BF16),
        scratch_shapes=[
            pltpu.VMEM((s + 2 * CONV_HALO, D_DIM), F32),
            pltpu.VMEM((s, D_DIM), F32),
            pltpu.VMEM((s, D_DIM), F32),
            pltpu.VMEM((s, D_DIM), F32),
            pltpu.VMEM((nch * nc, D_DIM + D_CHUNK, D_DIM), BF16),
            pltpu.VMEM((nch * nc, D_DIM, D_DIM), F32),
            pltpu.VMEM((nch * nc, 8, D_DIM), F32),
            pltpu.VMEM((nch, s, D_DIM), F32),
            pltpu.VMEM((nch, D_DIM, D_DIM), F32),
        ],
        compiler_params=_cparams(("parallel", "arbitrary")),
        name="delta",
    )(proj3, proj3, proj3, proj3, conv_w, conv_w, conv_w, gcol, grow, norm_g)


def _mix_kernel(n_gate_blocks, x_ref, ym_ref, yd_ref, *refs):
    gate_refs, (wbm_ref, wbd_ref, wo_ref, o_ref) = refs[:2 * n_gate_blocks], refs[2 * n_gate_blocks:]
    gate = lambda rs: _sigmoid(jnp.concatenate([r[...] for r in rs], axis=1))
    a = jnp.dot(ym_ref[...], wbm_ref[...], preferred_element_type=F32)
    b = jnp.dot(yd_ref[...], wbd_ref[...], preferred_element_type=F32)
    mixed = gate(gate_refs[:n_gate_blocks]) * a + gate(gate_refs[n_gate_blocks:]) * b
    o_ref[...] = x_ref[...] + jnp.dot(mixed.astype(BF16), wo_ref[...], preferred_element_type=F32)


def _mix(x2, ym2, yd2, proj2, off_gates, wbm, wbd, wo, tm):
    m, d = x2.shape
    const = lambda i: (0, 0)
    gw = math.gcd(off_gates, d)
    gate_specs = [pl.BlockSpec((tm, gw), functools.partial(lambda i, c: (i, c), c=(off_gates + t * gw) // gw))
                  for t in range(2 * d // gw)]
    return pl.pallas_call(
        functools.partial(_mix_kernel, d // gw),
        grid=(m // tm,),
        in_specs=[
            pl.BlockSpec((tm, d), lambda i: (i, 0)),
            pl.BlockSpec((tm, ym2.shape[1]), lambda i: (i, 0)),
            pl.BlockSpec((tm, yd2.shape[1]), lambda i: (i, 0)),
            *gate_specs,
            pl.BlockSpec(wbm.shape, const),
            pl.BlockSpec(wbd.shape, const),
            pl.BlockSpec(wo.shape, const),
        ],
        out_specs=pl.BlockSpec((tm, d), lambda i: (i, 0)),
        out_shape=jax.ShapeDtypeStruct((m, d), F32),
        compiler_params=_cparams(("parallel",)),
        name="mix",
    )(x2, ym2, yd2, *([proj2] * len(gate_specs)), wbm, wbd, wo)


def _ffn_kernel(x_ref, g2_ref, gf_ref, w1_ref, w2_ref, o_ref, h_ref, acc_ref):
    f = pl.program_id(1)

    @pl.when(f == 0)
    def _():
        x = x_ref[...]
        ms = jnp.mean(x * x, axis=-1, keepdims=True)
        h_ref[...] = (x * lax.rsqrt(ms + RMS_EPS) * g2_ref[...]).astype(BF16)
        acc_ref[...] = jnp.zeros(acc_ref.shape, F32)

    hid = jnp.dot(h_ref[...], w1_ref[...], preferred_element_type=F32)
    act = jnp.square(jnp.maximum(hid, 0.0)).astype(BF16)
    acc_ref[...] += jnp.dot(act, w2_ref[...], preferred_element_type=F32)

    @pl.when(f == pl.num_programs(1) - 1)
    def _():
        x = x_ref[...] + acc_ref[...]
        ms = jnp.mean(x * x, axis=-1, keepdims=True)
        o_ref[...] = x * lax.rsqrt(ms + RMS_EPS) * gf_ref[...]


def _ffn(x2, g2, gf, w1, w2, tm, tf):
    m, d = x2.shape
    dff = w1.shape[1]
    return pl.pallas_call(
        _ffn_kernel,
        grid=(m // tm, dff // tf),
        in_specs=[
            pl.BlockSpec((tm, d), lambda i, f: (i, 0)),
            pl.BlockSpec((1, d), lambda i, f: (0, 0)),
            pl.BlockSpec((1, d), lambda i, f: (0, 0)),
            pl.BlockSpec((d, tf), lambda i, f: (0, f)),
            pl.BlockSpec((tf, d), lambda i, f: (f, 0)),
        ],
        out_specs=pl.BlockSpec((tm, d), lambda i, f: (i, 0)),
        out_shape=jax.ShapeDtypeStruct((m, d), F32),
        scratch_shapes=[pltpu.VMEM((tm, d), BF16), pltpu.VMEM((tm, d), F32)],
        compiler_params=_cparams(("parallel", "arbitrary")),
        name="ffn",
    )(x2, g2, gf, w1, w2)


def _pick_tile(n, candidates):
    for c in candidates:
        if n % c == 0:
            return c
    raise ValueError(f"no tile in {candidates} divides {n}")


def _layer(x, norm1_g, w_in, i_bias, f_bias, m_norm_g, conv_w, a_log, dt_bias, d_norm_g,
           w_bm, w_bd, w_out, norm2_g, w_ff1, w_ff2, out_g):
    b, s, d = x.shape
    m = b * s
    mw, mqk, dw = M_HEADS * M_DV, M_HEADS * M_DK, D_HEADS * D_DIM
    nm, nd = N_DIR * M_HEADS, N_DIR * D_HEADS
    splits = (mqk, mqk, mw, mw, nm, nm, 3 * dw, dw, nd, nd, 2 * d)
    bounds = [0]
    for w_ in splits:
        bounds.append(bounds[-1] + w_)
    wt = w_in.T.astype(BF16)
    rows = lambda first, last: wt[bounds[first]:bounds[last + 1]]
    seg_bounds = [(bounds[0], bounds[4]), (bounds[6], bounds[8]), (bounds[10], bounds[11])]
    n_small = 2 * nm + 2 * nd
    wt_small = jnp.concatenate([rows(4, 5), rows(8, 9), jnp.zeros((LANES - n_small, d), BF16)], axis=0)
    zpad = lambda n: jnp.zeros((n,), F32)
    bias_row = jnp.concatenate([i_bias.reshape(-1), f_bias.reshape(-1), dt_bias.reshape(-1), zpad(LANES - CH_B)])
    alog_row = jnp.concatenate([zpad(CH_G), a_log.reshape(-1), zpad(LANES - CH_B)])
    gate_params = jnp.concatenate([bias_row[None], alog_row[None], jnp.zeros((6, LANES), F32)], axis=0)

    x2 = x.reshape(m, d)
    tm = _pick_tile(m, (512, 256, 128))
    tn = next(c for c in (1024, 512, 256) if all((hi - lo) % c == 0 for lo, hi in seg_bounds))
    proj2, smalls2 = _inproj(x2, norm1_g.reshape(1, d), wt, wt_small, seg_bounds,
                             _pick_tile(m, (1024, 512, 256)), tn)
    proj3 = proj2.reshape(b, s, -1)
    gcol, grow = _gate_prep(smalls2.reshape(b, s, LANES), gate_params)

    off_d = 2 * mqk + 2 * mw
    off_g = off_d + 4 * dw
    ym = _mlstm(proj3, gcol, grow, m_norm_g.reshape(M_HEADS, 1, M_DV), 0, mqk, 2 * mqk, 2 * mqk + mw)
    yd = _delta(proj3, conv_w, gcol, grow, d_norm_g.reshape(1, D_DIM), off_d, off_d + dw, off_d + 2 * dw, off_d + 3 * dw)

    x1 = _mix(x2, ym.reshape(m, mw), yd.reshape(m, dw), proj2, off_g,
              w_bm.astype(BF16), w_bd.astype(BF16), w_out.astype(BF16), _pick_tile(m, (256, 128)))
    tf = _pick_tile(w_ff1.shape[1], (1024, 512, 256))
    out = _ffn(x1, norm2_g.reshape(1, d), out_g.reshape(1, d), w_ff1.astype(BF16), w_ff2.astype(BF16), tm, tf)
    return out.reshape(b, s, d)


def kernel(x, norm1_g, w_in, mlstm_i_bias, mlstm_f_bias, mlstm_norm_g, delta_conv_w, delta_a_log,
           delta_dt_bias, delta_norm_g, w_branch_m, w_branch_d, w_out, norm2_g, w_ff1, w_ff2, norm_f_g):
    depth = w_in.shape[0]
    assert depth == 1, "the fused FFN + final-norm epilogue assumes a single layer"
    return _layer(x, norm1_g[0], w_in[0], mlstm_i_bias[0], mlstm_f_bias[0], mlstm_norm_g[0], delta_conv_w[0],
                  delta_a_log[0], delta_dt_bias[0], delta_norm_g[0], w_branch_m[0], w_branch_d[0], w_out[0],
                  norm2_g[0], w_ff1[0], w_ff2[0], norm_f_g)
```

```python
import functools
import math

import jax
import jax.numpy as jnp
from jax import lax
from jax.experimental import pallas as pl
from jax.experimental.pallas import tpu as pltpu

F32 = jnp.float32
BF16 = jnp.bfloat16
I32 = jnp.int32

N_DIR = 2
M_HEADS = 4
M_DK = 128
M_DV = 256
D_HEADS = 8
D_DIM = 128
CONV_W = 5
RMS_EPS = 1e-6
L2_EPS = 1e-6

LANES = 128
BF16_SUBLANES = 16
M_CHUNK = 256
M_CHUNKS_PER_ITER = 4
D_CHUNK = 64
D_GROUP = 256
D_PER_GROUP = D_GROUP // D_CHUNK
D_GROUPS_PER_ITER = 4
D_SCAN_STEPS_BETWEEN_STAGES = 8
GATE_ROWS = 256
CONV_HALO = 8

CH_I = 0
CH_F = 8
CH_G = 16
CH_B = 32
CH_T = 48

VMEM_LIMIT = 56 * 1024 * 1024

_NT = (((1,), (1,)), ((), ()))


def _cparams(sem):
    return pltpu.CompilerParams(dimension_semantics=sem, vmem_limit_bytes=VMEM_LIMIT)


def _bdot(a, b):
    return jnp.dot(a.astype(BF16), b.astype(BF16), preferred_element_type=F32)


def _bdot_nt(a, b):
    return lax.dot_general(a.astype(BF16), b.astype(BF16), _NT, preferred_element_type=F32)


def _sigmoid(x):
    return 1.0 / (1.0 + jnp.exp(-x))


def _softplus(x):
    return jnp.maximum(x, 0.0) + jnp.log1p(jnp.exp(-jnp.abs(x)))


def _lane_pick(x, ch):
    lane = lax.broadcasted_iota(I32, x.shape, 1)
    return jnp.sum(jnp.where(lane == ch, x, 0.0), axis=1, keepdims=True)


def _inproj_kernel(x_ref, g_ref, w_ref, ws_ref, o_ref, os_ref, hn_ref):
    @pl.when(pl.program_id(1) == 0)
    def _():
        x = x_ref[...]
        ms = jnp.mean(x * x, axis=-1, keepdims=True)
        hn = (x * lax.rsqrt(ms + RMS_EPS) * g_ref[...]).astype(BF16)
        hn_ref[...] = hn
        os_ref[...] = lax.dot_general(hn, ws_ref[...], _NT, preferred_element_type=F32)

    o_ref[...] = lax.dot_general(hn_ref[...], w_ref[...], _NT, preferred_element_type=F32)


def _inproj(x2, g, wt, wt_small, seg_bounds, tm, tn):
    m, d = x2.shape
    starts = [0]
    for lo, hi in seg_bounds:
        starts.append(starts[-1] + (hi - lo) // tn)

    def w_row(i, j):
        row = seg_bounds[0][0] + j * tn
        for k in range(1, len(seg_bounds)):
            row = jnp.where(j >= starts[k], seg_bounds[k][0] + (j - starts[k]) * tn, row)
        return pl.multiple_of(row, BF16_SUBLANES), 0

    return pl.pallas_call(
        _inproj_kernel,
        grid=(m // tm, starts[-1]),
        in_specs=[
            pl.BlockSpec((tm, d), lambda i, j: (i, 0)),
            pl.BlockSpec((1, d), lambda i, j: (0, 0)),
            pl.BlockSpec((pl.Element(tn), pl.Element(d)), w_row),
            pl.BlockSpec((LANES, d), lambda i, j: (0, 0)),
        ],
        out_specs=[
            pl.BlockSpec((tm, tn), lambda i, j: (i, j)),
            pl.BlockSpec((tm, LANES), lambda i, j: (i, 0)),
        ],
        out_shape=[jax.ShapeDtypeStruct((m, starts[-1] * tn), F32), jax.ShapeDtypeStruct((m, LANES), F32)],
        scratch_shapes=[pltpu.VMEM((tm, d), BF16)],
        compiler_params=_cparams(("parallel", "arbitrary")),
        name="inproj",
    )(x2, g, wt, wt_small)


def _gate_kernel(sm_ref, par_ref, col_ref, row_ref):
    r = GATE_ROWS
    x = sm_ref[0] + par_ref[0:1, :]
    lane = lax.broadcasted_iota(I32, x.shape, 1)
    logf = -_softplus(-x)
    g = -jnp.exp(par_ref[1:2, :]) * _softplus(x)
    beta = _sigmoid(x)
    xe = jnp.where(lane < CH_F, x,
                   jnp.where(lane < CH_G, logf,
                             jnp.where(lane < CH_B, g,
                                       jnp.where(lane < CH_T, beta, 0.0))))
    hi = xe.astype(BF16)
    r1 = xe - hi.astype(F32)
    mid = r1.astype(BF16)
    lo = (r1 - mid.astype(F32)).astype(BF16)
    x3 = jnp.concatenate([hi, mid, lo], axis=1)
    rr = lax.broadcasted_iota(I32, (r, r), 0)
    cc = lax.broadcasted_iota(I32, (r, r), 1)
    low = jnp.where(cc <= rr, 1.0, 0.0)
    upp = jnp.where(cc >= rr, 1.0, 0.0)
    same_d = jnp.where((rr // D_CHUNK) == (cc // D_CHUNK), 1.0, 0.0)
    same_m = jnp.where((rr // M_CHUNK) == (cc // M_CHUNK), 1.0, 0.0)
    mats = jnp.concatenate([
        low * same_m,
        upp * same_m,
        low * same_d,
        upp * same_d,
        same_d,
    ], axis=0).astype(BF16)
    y3 = jnp.dot(mats, x3, preferred_element_type=F32)
    y = y3[:, 0:LANES] + y3[:, LANES:2 * LANES] + y3[:, 2 * LANES:3 * LANES]
    pm, sm_, pd, sd, td = (y[i * r:(i + 1) * r] for i in range(5))
    half_m = CH_F + M_HEADS
    half_d = CH_G + D_HEADS
    out = jnp.where(lane < CH_F, xe,
          jnp.where(lane < half_m, pm,
          jnp.where(lane < CH_G, sm_,
          jnp.where(lane < half_d, pd,
          jnp.where(lane < CH_B, sd,
          jnp.where(lane < CH_T, xe,
          jnp.where(lane < CH_T + N_DIR * D_HEADS, pltpu.roll(td, CH_T - CH_G, axis=1), 0.0)))))))
    col_ref[0] = out
    row_ref[0] = out.T


def _gate_prep(smalls3, params):
    b, s, _ = smalls3.shape
    r = GATE_ROWS
    return pl.pallas_call(
        _gate_kernel,
        grid=(b, s // r),
        in_specs=[
            pl.BlockSpec((1, r, LANES), lambda i, j: (i, j, 0)),
            pl.BlockSpec((8, LANES), lambda i, j: (0, 0)),
        ],
        out_specs=[
            pl.BlockSpec((1, r, LANES), lambda i, j: (i, j, 0)),
            pl.BlockSpec((1, LANES, r), lambda i, j: (i, 0, j)),
        ],
        out_shape=[jax.ShapeDtypeStruct((b, s, LANES), F32), jax.ShapeDtypeStruct((b, LANES, s), F32)],
        compiler_params=_cparams(("parallel", "parallel")),
        name="gateprep",
    )(smalls3, params)


def _mlstm_kernel(q_ref, k_ref, v_ref, o_ref, gc_ref, gr_ref, ng_ref, y_ref,
                  hn_ref, den_ref, mi_ref, cu_ref, nu_ref, bl_ref, ml_ref, cs_ref, ns_ref, ms_ref, c_ref):
    head = pl.program_id(1)
    s = q_ref.shape[1]
    L = M_CHUNK
    nc = s // L
    scale = M_DK ** -0.5
    rr = lax.broadcasted_iota(I32, (L, L), 0)
    cc = lax.broadcasted_iota(I32, (L, L), 1)

    def bcast8(x11):
        return jnp.broadcast_to(x11, (8, LANES))

    def local(cs):
        chains = []
        for c in cs:
            r0 = pl.multiple_of(c * L, L)
            q = q_ref[0, pl.ds(r0, L), :]
            k = k_ref[0, pl.ds(r0, L), :] * scale
            vb = v_ref[0, pl.ds(r0, L), :].astype(BF16)
            kb = k.astype(BF16)
            qk = lax.dot_general(q.astype(BF16), kb, _NT, preferred_element_type=F32)
            k_t = k.T
            gcol = gc_ref[0, pl.ds(r0, L), :]
            for d in range(N_DIR):
                ch_i = CH_I + d * M_HEADS + head
                ch_f = CH_F + d * M_HEADS + head
                bc_col = _lane_pick(gcol, ch_f)
                i_row = gr_ref[0, pl.ds(ch_i, 1), pl.ds(r0, L)]
                bc_row = gr_ref[0, pl.ds(ch_f, 1), pl.ds(r0, L)]
                a_row = i_row - bc_row
                mask = (cc <= rr) if d == 0 else (cc >= rr)
                log_intra = jnp.where(mask, bc_col + a_row, -jnp.inf)
                m_intra = jnp.max(log_intra, axis=1, keepdims=True)
                p = qk * jnp.exp(log_intra - m_intra)
                b_last = bc_row[:, L - 1:L] if d == 0 else bc_row[:, 0:1]
                log_state = b_last + a_row
                m_loc = jnp.max(log_state, axis=1, keepdims=True)
                w_state = jnp.exp(log_state - m_loc)
                chains.append(dict(
                    idx=d * nc + c, d=d, r0=r0, vb=vb, kb=kb, pb=p.astype(BF16),
                    den=jnp.sum(p, axis=1, keepdims=True), m_intra=m_intra,
                    kw=(k_t * w_state).astype(BF16), w8=jnp.broadcast_to(w_state, (8, L)).astype(BF16),
                    b_last=b_last, m_loc=m_loc))
        for c in chains:
            hn_ref[c["d"], pl.ds(c["r0"], L), :] = jnp.dot(c["pb"], c["vb"], preferred_element_type=F32)
            den_ref[c["d"], pl.ds(c["r0"], L), :] = jnp.broadcast_to(c["den"], (L, LANES))
            mi_ref[c["d"], pl.ds(c["r0"], L), :] = jnp.broadcast_to(c["m_intra"], (L, LANES))
        for c in chains:
            cu_ref[c["idx"]] = jnp.dot(c["kw"], c["vb"], preferred_element_type=F32)
            nu_ref[c["idx"]] = jnp.dot(c["w8"], c["kb"], preferred_element_type=F32)
            bl_ref[c["idx"]] = bcast8(c["b_last"])
            ml_ref[c["idx"]] = bcast8(c["m_loc"])

    def local_trip(i, _):
        local([M_CHUNKS_PER_ITER * i + t for t in range(M_CHUNKS_PER_ITER)])
        return 0

    lax.fori_loop(0, nc // M_CHUNKS_PER_ITER, local_trip, 0)

    c_ref[...] = jnp.zeros(c_ref.shape, F32)

    def scan(j, carry):
        out = []
        for d in range(N_DIR):
            m_st, n_st = carry[d]
            idx = d * nc + (j if d == 0 else nc - 1 - j)
            c_st = c_ref[d]
            cs_ref[idx] = c_st.astype(BF16)
            ns_ref[idx] = n_st
            ms_ref[idx] = m_st
            bl = bl_ref[idx]
            ml = ml_ref[idx]
            m_new = jnp.maximum(bl + m_st, ml)
            decay = jnp.exp(bl + m_st - m_new)
            gain = jnp.exp(ml - m_new)
            c_ref[d] = decay[0:1, 0:1] * c_st + gain[0:1, 0:1] * cu_ref[idx]
            out.append((m_new, decay * n_st + gain * nu_ref[idx]))
        return tuple(out)

    m0 = jnp.full((8, LANES), -jnp.inf, F32)
    n0 = jnp.zeros((8, M_DK), F32)
    lax.fori_loop(0, nc, scan, ((m0, n0), (m0, n0)))

    def wide(x):
        return jnp.concatenate([x] * (M_DV // LANES), axis=1)

    def combine(c, _):
        r0 = pl.multiple_of(c * L, L)
        q = q_ref[0, pl.ds(r0, L), :]
        qb = q.astype(BF16)
        gcol = gc_ref[0, pl.ds(r0, L), :]
        hh = None
        for d in range(N_DIR):
            idx = d * nc + c
            bc = jnp.broadcast_to(_lane_pick(gcol, CH_F + d * M_HEADS + head), (L, LANES))
            den_i = den_ref[d, pl.ds(r0, L), :]
            m_i = mi_ref[d, pl.ds(r0, L), :]
            log_inter = bc + ms_ref[idx][0:1, :]
            m_row = jnp.maximum(log_inter, m_i)
            w_i = jnp.exp(m_i - m_row)
            w_x = jnp.exp(log_inter - m_row)
            q_c = jnp.dot(qb, cs_ref[idx], preferred_element_type=F32)
            q_n = jnp.broadcast_to(jnp.sum(q * ns_ref[idx][0:1, :], axis=1, keepdims=True), (L, LANES))
            num = wide(w_i) * hn_ref[d, pl.ds(r0, L), :] + wide(w_x) * q_c
            den = w_i * den_i + w_x * q_n
            h_d = num * wide(1.0 / jnp.maximum(jnp.abs(den), jnp.exp(-m_row)))
            hh = h_d if hh is None else hh + h_d
        hh = hh * lax.rsqrt(jnp.mean(hh * hh, axis=-1, keepdims=True) + RMS_EPS)
        hh = hh * ng_ref[0]
        y_ref[0, pl.ds(r0, L), :] = (_sigmoid(o_ref[0, pl.ds(r0, L), :]) * hh).astype(y_ref.dtype)
        return 0

    lax.fori_loop(0, nc, combine, 0)


def _mlstm(proj3, gcol, grow, norm_g, off_q, off_k, off_v, off_o):
    b, s, _ = proj3.shape
    assert s % (M_CHUNK * M_CHUNKS_PER_ITER) == 0
    nc = s // M_CHUNK
    bq, bk = off_q // M_DK, off_k // M_DK
    bv, bo = off_v // M_DV, off_o // M_DV
    return pl.pallas_call(
        _mlstm_kernel,
        grid=(b, M_HEADS),
        in_specs=[
            pl.BlockSpec((1, s, M_DK), lambda i, h: (i, 0, bq + h)),
            pl.BlockSpec((1, s, M_DK), lambda i, h: (i, 0, bk + h)),
            pl.BlockSpec((1, s, M_DV), lambda i, h: (i, 0, bv + h)),
            pl.BlockSpec((1, s, M_DV), lambda i, h: (i, 0, bo + h)),
            pl.BlockSpec((1, s, LANES), lambda i, h: (i, 0, 0)),
            pl.BlockSpec((1, LANES, s), lambda i, h: (i, 0, 0)),
            pl.BlockSpec((1, 1, M_DV), lambda i, h: (h, 0, 0)),
        ],
        out_specs=pl.BlockSpec((1, s, M_DV), lambda i, h: (i, 0, h)),
        out_shape=jax.ShapeDtypeStruct((b, s, M_HEADS * M_DV), BF16),
        scratch_shapes=[
            pltpu.VMEM((N_DIR, s, M_DV), F32),
            pltpu.VMEM((N_DIR, s, LANES), F32),
            pltpu.VMEM((N_DIR, s, LANES), F32),
            pltpu.VMEM((N_DIR * nc, M_DK, M_DV), F32),
            pltpu.VMEM((N_DIR * nc, 8, M_DK), F32),
            pltpu.VMEM((N_DIR * nc, 8, LANES), F32),
            pltpu.VMEM((N_DIR * nc, 8, LANES), F32),
            pltpu.VMEM((N_DIR * nc, M_DK, M_DV), BF16),
            pltpu.VMEM((N_DIR * nc, 8, M_DK), F32),
            pltpu.VMEM((N_DIR * nc, 8, LANES), F32),
            pltpu.VMEM((N_DIR, M_DK, M_DV), F32),
        ],
        compiler_params=_cparams(("parallel", "arbitrary")),
        name="mlstm",
    )(proj3, proj3, proj3, proj3, gcol, grow, norm_g)


def _delta_kernel(q_ref, k_ref, v_ref, z_ref, cwq_ref, cwk_ref, cwv_ref, gc_ref, gr_ref, ng_ref, y_ref,
                  xp_ref, qs_ref, ks_ref, vs_ref, kq_ref, n_ref, dec_ref, o_ref, st_ref):
    t = pl.program_id(0)
    head = jnp.minimum(t, pl.num_programs(0) - 2) % D_HEADS
    s = q_ref.shape[1]
    L = D_CHUNK
    G = D_GROUP
    nc = s // L
    ng = s // G
    halo = CONV_HALO
    pad = CONV_W // 2

    @pl.when(t == 0)
    def _():
        for r in (kq_ref, n_ref, dec_ref, o_ref):
            r[...] = jnp.zeros(r.shape, r.dtype)

    xp_ref[0:halo, :] = jnp.zeros((halo, D_DIM), F32)
    xp_ref[s + halo:s + 2 * halo, :] = jnp.zeros((halo, D_DIM), F32)

    def conv_into(x_ref, cw_ref, dst_ref, l2, mult):
        xp_ref[halo:s + halo, :] = x_ref[0]
        for blk in range(ng):
            base = halo + blk * G - pad
            acc = xp_ref[base:base + G, :] * cw_ref[0:1, :]
            for j in range(1, CONV_W):
                acc = acc + xp_ref[base + j:base + j + G, :] * cw_ref[j:j + 1, :]
            y = acc * _sigmoid(acc)
            if l2:
                y = y * (lax.rsqrt(jnp.sum(y * y, axis=-1, keepdims=True) + L2_EPS) * mult)
            dst_ref[blk * G:(blk + 1) * G, :] = y

    conv_into(q_ref, cwq_ref, qs_ref, True, D_DIM ** -0.5)
    conv_into(k_ref, cwk_ref, ks_ref, True, 1.0)
    conv_into(v_ref, cwv_ref, vs_ref, False, 1.0)

    st_ref[...] = jnp.zeros(st_ref.shape, F32)

    def scan_step(scan, j):
        for d in range(N_DIR):
            c = j if d == 0 else nc - 1 - j
            r0 = pl.multiple_of(c * L, L)
            idx = (scan + d) * nc + c
            st = st_ref[d]
            res = jnp.dot(kq_ref[idx], st.astype(BF16), preferred_element_type=F32)
            st_ref[d] = dec_ref[idx][0:1, :] * st + res[0:D_DIM] + n_ref[idx]
            o_ref[scan + d, pl.ds(r0, L), :] = o_ref[scan + d, pl.ds(r0, L), :] + res[D_DIM:D_DIM + L]

    tchunk = lax.broadcasted_iota(I32, (D_DIM, D_GROUP), 1) // D_CHUNK
    rr = lax.broadcasted_iota(I32, (G, G), 0)
    cc = lax.broadcasted_iota(I32, (G, G), 1)
    same = (rr // L) == (cc // L)
    pr = lax.broadcasted_iota(I32, (L, G), 0)
    pc = lax.broadcasted_iota(I32, (L, G), 1)
    eye_p = jnp.where((pc % L) == pr, 1.0, 0.0).astype(F32)

    def pack(m):
        out = m[0:L]
        for i in range(1, D_PER_GROUP):
            out = out + m[i * L:(i + 1) * L]
        return out

    def unpack(p):
        return jnp.where(same, jnp.concatenate([p] * D_PER_GROUP, axis=0), 0.0)

    scan_per_trip = nc // (ng // D_GROUPS_PER_ITER)

    def groups(fill, scan, trip):
        pending = iter(range(scan_per_trip))

        def scan_some(n):
            for _ in range(n):
                j = next(pending, None)
                if j is not None:
                    scan_step(scan, trip * scan_per_trip + j)

        chains = []
        for gi in [D_GROUPS_PER_ITER * trip + g for g in range(D_GROUPS_PER_ITER)]:
            r0 = pl.multiple_of(gi * G, G)
            kk_ = ks_ref[pl.ds(r0, G), :]
            qq_ = qs_ref[pl.ds(r0, G), :]
            vv_ = vs_ref[pl.ds(r0, G), :]
            kb = kk_.astype(BF16)
            kk = lax.dot_general(kb, kb, _NT, preferred_element_type=F32)
            qk = lax.dot_general(qq_.astype(BF16), kb, _NT, preferred_element_type=F32)
            gcol = gc_ref[0, pl.ds(r0, G), :]
            for d in range(N_DIR):
                ch = d * D_HEADS + head
                g_col = _lane_pick(gcol, CH_G + ch)
                b_col = _lane_pick(gcol, CH_B + ch)
                t_col = _lane_pick(gcol, CH_T + ch)
                g_row = gr_ref[0, pl.ds(CH_G + ch, 1), pl.ds(r0, G)]
                tri = (cc <= rr) if d == 0 else (cc >= rr)
                strict = (cc < rr) if d == 0 else (cc > rr)
                gam = jnp.exp(jnp.where(tri, jnp.where(same, g_col - g_row, -jnp.inf), -jnp.inf))
                x = jnp.where(strict, -(b_col * kk * gam), 0.0)
                eg = jnp.exp(g_col)
                chains.append(dict(
                    gi=gi, slot=fill + d, r0=r0, x=x, xp=pack(x), attn=(qk * gam).astype(BF16),
                    rhs=jnp.concatenate([b_col * vv_, (b_col * eg) * kk_], axis=1).astype(BF16),
                    qg=qq_ * eg, kdt=(kk_ * jnp.exp(t_col - g_col)).T, dec=jnp.exp(t_col)))
            scan_some(-(-(scan_per_trip - D_SCAN_STEPS_BETWEEN_STAGES) // D_GROUPS_PER_ITER))
        for c in chains:
            c["tp"] = eye_p + c["xp"]
            c["p"] = _bdot(c["xp"], c["x"])
        scan_some(1)
        for _it in range(4):
            for c in chains:
                res = _bdot(jnp.concatenate([c["tp"], c["p"]], axis=0), unpack(c["p"]))
                c["tp"] = c["tp"] + res[0:L]
                c["p"] = res[L:2 * L]
            scan_some(1)
        for c in chains:
            c["tp"] = c["tp"] + _bdot(c["tp"], unpack(c["p"]))
        scan_some(1)
        for c in chains:
            c["uw"] = _bdot(unpack(c["tp"]), c["rhs"]).astype(BF16)
        scan_some(1)
        for c in chains:
            au = jnp.dot(c["attn"], c["uw"], preferred_element_type=F32)
            o_ref[c["slot"], pl.ds(c["r0"], G), :] = au[:, 0:D_DIM]
            c["qe"] = (c["qg"] - au[:, D_DIM:2 * D_DIM]).astype(BF16)
        scan_some(1)
        for c in chains:
            for ci in range(D_PER_GROUP):
                idx = c["slot"] * nc + c["gi"] * D_PER_GROUP + ci
                ku = _bdot(jnp.where(tchunk == ci, c["kdt"], 0.0), c["uw"])
                n_ref[idx] = ku[:, 0:D_DIM]
                kq_ref[idx, 0:D_DIM, :] = (-ku[:, D_DIM:2 * D_DIM]).astype(BF16)
                kq_ref[idx, D_DIM:D_DIM + L, :] = c["qe"][ci * L:(ci + 1) * L]
                dec_ref[idx] = jnp.broadcast_to(c["dec"][ci * L:ci * L + 8], (8, D_DIM))
        scan_some(scan_per_trip)

    def finish(scan, gi):
        r0 = pl.multiple_of(gi * G, G)
        o = o_ref[scan, pl.ds(r0, G), :] + o_ref[scan + 1, pl.ds(r0, G), :]
        o = o * lax.rsqrt(jnp.mean(o * o, axis=-1, keepdims=True) + RMS_EPS) * ng_ref[...]
        z = z_ref[0, pl.ds(r0, G), :]
        y_ref[0, pl.ds(r0, G), :] = (o * (z * _sigmoid(z))).astype(y_ref.dtype)

    fill = (t % 2) * N_DIR
    scan = N_DIR - fill

    def group_batch(i, carry):
        groups(fill, scan, i)
        return carry

    def finish_batch(gi, carry):
        finish(scan, gi)
        return carry

    lax.fori_loop(0, ng // D_GROUPS_PER_ITER, group_batch, 0)
    lax.fori_loop(0, ng, finish_batch, 0)


def _delta(proj3, conv_w, gcol, grow, norm_g, off_q, off_k, off_v, off_z):
    b, s, _ = proj3.shape
    w = D_HEADS * D_DIM
    nc = s // D_CHUNK
    trips = s // (D_GROUP * D_GROUPS_PER_ITER)
    assert s % (D_GROUP * D_GROUPS_PER_ITER) == 0 and nc % trips == 0
    bq, bk, bv, bz = (o // D_DIM for o in (off_q, off_k, off_v, off_z))
    n_units = b * D_HEADS
    cur = lambda t: jnp.minimum(t, n_units - 1)
    prev = lambda t: jnp.maximum(t - 1, 0)
    col = lambda unit, blk: lambda t: (unit(t) // D_HEADS, 0, blk + unit(t) % D_HEADS)
    return pl.pallas_call(
        _delta_kernel,
        grid=(n_units + 1,),
        in_specs=[
            pl.BlockSpec((1, s, D_DIM), col(cur, bq)),
            pl.BlockSpec((1, s, D_DIM), col(cur, bk)),
            pl.BlockSpec((1, s, D_DIM), col(cur, bv)),
            pl.BlockSpec((1, s, D_DIM), col(prev, bz)),
            pl.BlockSpec((CONV_W, D_DIM), lambda t: (0, cur(t) % D_HEADS)),
            pl.BlockSpec((CONV_W, D_DIM), lambda t: (0, D_HEADS + cur(t) % D_HEADS)),
            pl.BlockSpec((CONV_W, D_DIM), lambda t: (0, 2 * D_HEADS + cur(t) % D_HEADS)),
            pl.BlockSpec((1, s, LANES), lambda t: (cur(t) // D_HEADS, 0, 0)),
            pl.BlockSpec((1, LANES, s), lambda t: (cur(t) // D_HEADS, 0, 0)),
            pl.BlockSpec((1, D_DIM), lambda t: (0, 0)),
        ],
        out_specs=pl.BlockSpec((1, s, D_DIM), col(prev, 0)),
        out_shape=jax.ShapeDtypeStruct((b, s, w), BF16),
        scratch_shapes=[
            pltpu.VMEM((s + 2 * CONV_HALO, D_DIM), F32),
            pltpu.VMEM((s, D_DIM), F32),
            pltpu.VMEM((s, D_DIM), F32),
            pltpu.VMEM((s, D_DIM), F32),
            pltpu.VMEM((2 * N_DIR * nc, D_DIM + D_CHUNK, D_DIM), BF16),
            pltpu.VMEM((2 * N_DIR * nc, D_DIM, D_DIM), F32),
            pltpu.VMEM((2 * N_DIR * nc, 8, D_DIM), F32),
            pltpu.VMEM((2 * N_DIR, s, D_DIM), F32),
            pltpu.VMEM((N_DIR, D_DIM, D_DIM), F32),
        ],
        compiler_params=_cparams(("arbitrary",)),
        name="delta",
    )(proj3, proj3, proj3, proj3, conv_w, conv_w, conv_w, gcol, grow, norm_g)


def _mix_kernel(n_gate_blocks, x_ref, ym_ref, yd_ref, *refs):
    gate_refs, (wbm_ref, wbd_ref, wo_ref, o_ref) = refs[:2 * n_gate_blocks], refs[2 * n_gate_blocks:]
    gate = lambda rs: _sigmoid(jnp.concatenate([r[...] for r in rs], axis=1))
    a = jnp.dot(ym_ref[...], wbm_ref[...], preferred_element_type=F32)
    b = jnp.dot(yd_ref[...], wbd_ref[...], preferred_element_type=F32)
    mixed = gate(gate_refs[:n_gate_blocks]) * a + gate(gate_refs[n_gate_blocks:]) * b
    o_ref[...] = x_ref[...] + jnp.dot(mixed.astype(BF16), wo_ref[...], preferred_element_type=F32)


def _mix(x2, ym2, yd2, proj2, off_gates, wbm, wbd, wo, tm):
    m, d = x2.shape
    const = lambda i: (0, 0)
    gw = math.gcd(off_gates, d)
    gate_specs = [pl.BlockSpec((tm, gw), functools.partial(lambda i, c: (i, c), c=(off_gates + t * gw) // gw))
                  for t in range(2 * d // gw)]
    return pl.pallas_call(
        functools.partial(_mix_kernel, d // gw),
        grid=(m // tm,),
        in_specs=[
            pl.BlockSpec((tm, d), lambda i: (i, 0)),
            pl.BlockSpec((tm, ym2.shape[1]), lambda i: (i, 0)),
            pl.BlockSpec((tm, yd2.shape[1]), lambda i: (i, 0)),
            *gate_specs,
            pl.BlockSpec(wbm.shape, const),
            pl.BlockSpec(wbd.shape, const),
            pl.BlockSpec(wo.shape, const),
        ],
        out_specs=pl.BlockSpec((tm, d), lambda i: (i, 0)),
        out_shape=jax.ShapeDtypeStruct((m, d), F32),
        compiler_params=_cparams(("parallel",)),
        name="mix",
    )(x2, ym2, yd2, *([proj2] * len(gate_specs)), wbm, wbd, wo)


def _ffn_kernel(x_ref, g2_ref, gf_ref, w1_ref, w2_ref, o_ref, h_ref, acc_ref):
    f = pl.program_id(1)

    @pl.when(f == 0)
    def _():
        x = x_ref[...]
        ms = jnp.mean(x * x, axis=-1, keepdims=True)
        h_ref[...] = (x * lax.rsqrt(ms + RMS_EPS) * g2_ref[...]).astype(BF16)
        acc_ref[...] = jnp.zeros(acc_ref.shape, F32)

    hid = jnp.dot(h_ref[...], w1_ref[...], preferred_element_type=F32)
    act = jnp.square(jnp.maximum(hid, 0.0)).astype(BF16)
    acc_ref[...] += jnp.dot(act, w2_ref[...], preferred_element_type=F32)

    @pl.when(f == pl.num_programs(1) - 1)
    def _():
        x = x_ref[...] + acc_ref[...]
        ms = jnp.mean(x * x, axis=-1, keepdims=True)
        o_ref[...] = x * lax.rsqrt(ms + RMS_EPS) * gf_ref[...]


def _ffn(x2, g2, gf, w1, w2, tm, tf):
    m, d = x2.shape
    dff = w1.shape[1]
    return pl.pallas_call(
        _ffn_kernel,
        grid=(m // tm, dff // tf),
        in_specs=[
            pl.BlockSpec((tm, d), lambda i, f: (i, 0)),
            pl.BlockSpec((1, d), lambda i, f: (0, 0)),
            pl.BlockSpec((1, d), lambda i, f: (0, 0)),
            pl.BlockSpec((d, tf), lambda i, f: (0, f)),
            pl.BlockSpec((tf, d), lambda i, f: (f, 0)),
        ],
        out_specs=pl.BlockSpec((tm, d), lambda i, f: (i, 0)),
        out_shape=jax.ShapeDtypeStruct((m, d), F32),
        scratch_shapes=[pltpu.VMEM((tm, d), BF16), pltpu.VMEM((tm, d), F32)],
        compiler_params=_cparams(("parallel", "arbitrary")),
        name="ffn",
    )(x2, g2, gf, w1, w2)


def _pick_tile(n, candidates):
    for c in candidates:
        if n % c == 0:
            return c
    raise ValueError(f"no tile in {candidates} divides {n}")


def _layer(x, norm1_g, w_in, i_bias, f_bias, m_norm_g, conv_w, a_log, dt_bias, d_norm_g,
           w_bm, w_bd, w_out, norm2_g, w_ff1, w_ff2, out_g):
    b, s, d = x.shape
    m = b * s
    mw, mqk, dw = M_HEADS * M_DV, M_HEADS * M_DK, D_HEADS * D_DIM
    nm, nd = N_DIR * M_HEADS, N_DIR * D_HEADS
    splits = (mqk, mqk, mw, mw, nm, nm, 3 * dw, dw, nd, nd, 2 * d)
    bounds = [0]
    for w_ in splits:
        bounds.append(bounds[-1] + w_)
    wt = w_in.T.astype(BF16)
    rows = lambda first, last: wt[bounds[first]:bounds[last + 1]]
    seg_bounds = [(bounds[0], bounds[4]), (bounds[6], bounds[8]), (bounds[10], bounds[11])]
    n_small = 2 * nm + 2 * nd
    wt_small = jnp.concatenate([rows(4, 5), rows(8, 9), jnp.zeros((LANES - n_small, d), BF16)], axis=0)
    zpad = lambda n: jnp.zeros((n,), F32)
    bias_row = jnp.concatenate([i_bias.reshape(-1), f_bias.reshape(-1), dt_bias.reshape(-1), zpad(LANES - CH_B)])
    alog_row = jnp.concatenate([zpad(CH_G), a_log.reshape(-1), zpad(LANES - CH_B)])
    gate_params = jnp.concatenate([bias_row[None], alog_row[None], jnp.zeros((6, LANES), F32)], axis=0)

    x2 = x.reshape(m, d)
    tm = _pick_tile(m, (512, 256, 128))
    tn = next(c for c in (1024, 512, 256) if all((hi - lo) % c == 0 for lo, hi in seg_bounds))
    proj2, smalls2 = _inproj(x2, norm1_g.reshape(1, d), wt, wt_small, seg_bounds,
                             _pick_tile(m, (1024, 512, 256)), tn)
    proj3 = proj2.reshape(b, s, -1)
    gcol, grow = _gate_prep(smalls2.reshape(b, s, LANES), gate_params)

    off_d = 2 * mqk + 2 * mw
    off_g = off_d + 4 * dw
    ym = _mlstm(proj3, gcol, grow, m_norm_g.reshape(M_HEADS, 1, M_DV), 0, mqk, 2 * mqk, 2 * mqk + mw)
    yd = _delta(proj3, conv_w, gcol, grow, d_norm_g.reshape(1, D_DIM), off_d, off_d + dw, off_d + 2 * dw, off_d + 3 * dw)

    x1 = _mix(x2, ym.reshape(m, mw), yd.reshape(m, dw), proj2, off_g,
              w_bm.astype(BF16), w_bd.astype(BF16), w_out.astype(BF16), _pick_tile(m, (256, 128)))
    tf = _pick_tile(w_ff1.shape[1], (1024, 512, 256))
    out = _ffn(x1, norm2_g.reshape(1, d), out_g.reshape(1, d), w_ff1.astype(BF16), w_ff2.astype(BF16), tm, tf)
    return out.reshape(b, s, d)


def kernel(x, norm1_g, w_in, mlstm_i_bias, mlstm_f_bias, mlstm_norm_g, delta_conv_w, delta_a_log,
           delta_dt_bias, delta_norm_g, w_branch_m, w_branch_d, w_out, norm2_g, w_ff1, w_ff2, norm_f_g):
    depth = w_in.shape[0]
    assert depth == 1, "the fused FFN + final-norm epilogue assumes a single layer"
    return _layer(x, norm1_g[0], w_in[0], mlstm_i_bias[0], mlstm_f_bias[0], mlstm_norm_g[0], delta_conv_w[0],
                  delta_a_log[0], delta_dt_bias[0], delta_norm_g[0], w_branch_m[0], w_branch_d[0], w_out[0],
                  norm2_g[0], w_ff1[0], w_ff2[0], norm_f_g)
```

```python
import functools
import math

import jax
import jax.numpy as jnp
import numpy as np
from jax import lax
from jax.experimental import pallas as pl
from jax.experimental.pallas import tpu as pltpu

F32 = jnp.float32
BF16 = jnp.bfloat16
I32 = jnp.int32

N_DIR = 2
M_HEADS = 4
M_DK = 128
M_DV = 256
D_HEADS = 8
D_DIM = 128
CONV_W = 5
RMS_EPS = 1e-6
L2_EPS = 1e-6

LANES = 128
BF16_SUBLANES = 16
M_CHUNK = 256
M_CHUNKS_PER_ITER = 4
D_CHUNK = 64
D_GROUP = 256
D_PER_GROUP = D_GROUP // D_CHUNK
D_GROUPS_PER_ITER = 4
D_SCAN_STEPS_BETWEEN_STAGES = 8
GATE_ROWS = 256
GATE_BLOCKS_PER_STEP = 4
CONV_HALO = 8

CH_I = 0
CH_F = 8
CH_G = 16
CH_B = 32
CH_T = 48

VMEM_LIMIT = 56 * 1024 * 1024

_NT = (((1,), (1,)), ((), ()))


def _cparams(sem):
    return pltpu.CompilerParams(dimension_semantics=sem, vmem_limit_bytes=VMEM_LIMIT)


def _bdot(a, b):
    return jnp.dot(a.astype(BF16), b.astype(BF16), preferred_element_type=F32)


def _bdot_nt(a, b):
    return lax.dot_general(a.astype(BF16), b.astype(BF16), _NT, preferred_element_type=F32)


def _sigmoid(x):
    return 1.0 / (1.0 + jnp.exp(-x))


def _softplus(x):
    return jnp.maximum(x, 0.0) + jnp.log1p(jnp.exp(-jnp.abs(x)))


def _lane_pick(x, ch):
    lane = lax.broadcasted_iota(I32, x.shape, 1)
    return jnp.sum(jnp.where(lane == ch, x, 0.0), axis=1, keepdims=True)


def _inproj_kernel(x_ref, g_ref, w_ref, ws_ref, o_ref, os_ref, hn_ref):
    @pl.when(pl.program_id(1) == 0)
    def _():
        x = x_ref[...]
        ms = jnp.mean(x * x, axis=-1, keepdims=True)
        hn = (x * lax.rsqrt(ms + RMS_EPS) * g_ref[...]).astype(BF16)
        hn_ref[...] = hn
        os_ref[...] = lax.dot_general(hn, ws_ref[...], _NT, preferred_element_type=F32)

    o_ref[...] = lax.dot_general(hn_ref[...], w_ref[...], _NT, preferred_element_type=F32)


def _inproj(x2, g, wt, wt_small, seg_bounds, tm, tn):
    m, d = x2.shape
    starts = [0]
    for lo, hi in seg_bounds:
        starts.append(starts[-1] + (hi - lo) // tn)

    def w_row(i, j):
        row = seg_bounds[0][0] + j * tn
        for k in range(1, len(seg_bounds)):
            row = jnp.where(j >= starts[k], seg_bounds[k][0] + (j - starts[k]) * tn, row)
        return pl.multiple_of(row, BF16_SUBLANES), 0

    return pl.pallas_call(
        _inproj_kernel,
        grid=(m // tm, starts[-1]),
        in_specs=[
            pl.BlockSpec((tm, d), lambda i, j: (i, 0)),
            pl.BlockSpec((1, d), lambda i, j: (0, 0)),
            pl.BlockSpec((pl.Element(tn), pl.Element(d)), w_row),
            pl.BlockSpec((LANES, d), lambda i, j: (0, 0)),
        ],
        out_specs=[
            pl.BlockSpec((tm, tn), lambda i, j: (i, j)),
            pl.BlockSpec((tm, LANES), lambda i, j: (i, 0)),
        ],
        out_shape=[jax.ShapeDtypeStruct((m, starts[-1] * tn), F32), jax.ShapeDtypeStruct((m, LANES), F32)],
        scratch_shapes=[pltpu.VMEM((tm, d), BF16)],
        compiler_params=_cparams(("parallel", "arbitrary")),
        name="inproj",
    )(x2, g, wt, wt_small)


def _cumsum_matrices():
    rr, cc = np.indices((GATE_ROWS, GATE_ROWS))
    low, upp = cc <= rr, cc >= rr
    same_d = (rr // D_CHUNK) == (cc // D_CHUNK)
    same_m = (rr // M_CHUNK) == (cc // M_CHUNK)
    mats = np.concatenate([
        low & same_m,
        upp & same_m,
        low & same_d,
        upp & same_d,
        same_d,
    ], axis=0)
    return jnp.asarray(mats.astype(np.float32), dtype=BF16)


def _gate_kernel(sm_ref, par_ref, mats_ref, col_ref, row_ref):
    r = GATE_ROWS
    lane = lax.broadcasted_iota(I32, (r, LANES), 1)
    neg_a = -jnp.exp(par_ref[1:2, :])
    xes, x3s = [], []
    for blk in range(GATE_BLOCKS_PER_STEP):
        x = sm_ref[0, blk * r:(blk + 1) * r, :] + par_ref[0:1, :]
        logf = -_softplus(-x)
        g = neg_a * _softplus(x)
        beta = _sigmoid(x)
        xe = jnp.where(lane < CH_F, x,
                       jnp.where(lane < CH_G, logf,
                                 jnp.where(lane < CH_B, g,
                                           jnp.where(lane < CH_T, beta, 0.0))))
        hi = xe.astype(BF16)
        r1 = xe - hi.astype(F32)
        mid = r1.astype(BF16)
        lo = (r1 - mid.astype(F32)).astype(BF16)
        xes.append(xe)
        x3s.append(jnp.concatenate([hi, mid, lo], axis=1))
    y3s = [jnp.dot(mats_ref[...], x3, preferred_element_type=F32) for x3 in x3s]
    half_m = CH_F + M_HEADS
    half_d = CH_G + D_HEADS
    for blk, (xe, y3) in enumerate(zip(xes, y3s)):
        y = y3[:, 0:LANES] + y3[:, LANES:2 * LANES] + y3[:, 2 * LANES:3 * LANES]
        pm, sm_, pd, sd, td = (y[i * r:(i + 1) * r] for i in range(5))
        out = jnp.where(lane < CH_F, xe,
              jnp.where(lane < half_m, pm,
              jnp.where(lane < CH_G, sm_,
              jnp.where(lane < half_d, pd,
              jnp.where(lane < CH_B, sd,
              jnp.where(lane < CH_T, xe,
              jnp.where(lane < CH_T + N_DIR * D_HEADS, pltpu.roll(td, CH_T - CH_G, axis=1), 0.0)))))))
        col_ref[0, blk * r:(blk + 1) * r, :] = out
        row_ref[0, :, blk * r:(blk + 1) * r] = out.T


def _gate_prep(smalls3, params):
    b, s, _ = smalls3.shape
    r = GATE_ROWS * GATE_BLOCKS_PER_STEP
    assert s % r == 0
    return pl.pallas_call(
        _gate_kernel,
        grid=(b, s // r),
        in_specs=[
            pl.BlockSpec((1, r, LANES), lambda i, j: (i, j, 0)),
            pl.BlockSpec((8, LANES), lambda i, j: (0, 0)),
            pl.BlockSpec((5 * GATE_ROWS, GATE_ROWS), lambda i, j: (0, 0)),
        ],
        out_specs=[
            pl.BlockSpec((1, r, LANES), lambda i, j: (i, j, 0)),
            pl.BlockSpec((1, LANES, r), lambda i, j: (i, 0, j)),
        ],
        out_shape=[jax.ShapeDtypeStruct((b, s, LANES), F32), jax.ShapeDtypeStruct((b, LANES, s), F32)],
        compiler_params=_cparams(("parallel", "parallel")),
        name="gateprep",
    )(smalls3, params, _cumsum_matrices())


def _mlstm_kernel(q_ref, k_ref, v_ref, o_ref, gc_ref, gr_ref, ng_ref, y_ref,
                  hn_ref, den_ref, mi_ref, cu_ref, nu_ref, bl_ref, ml_ref, cs_ref, ns_ref, ms_ref, c_ref):
    head = pl.program_id(1)
    s = q_ref.shape[1]
    L = M_CHUNK
    nc = s // L
    scale = M_DK ** -0.5
    rr = lax.broadcasted_iota(I32, (L, L), 0)
    cc = lax.broadcasted_iota(I32, (L, L), 1)

    def bcast8(x11):
        return jnp.broadcast_to(x11, (8, LANES))

    def local(cs):
        chains = []
        for c in cs:
            r0 = pl.multiple_of(c * L, L)
            q = q_ref[0, pl.ds(r0, L), :]
            k = k_ref[0, pl.ds(r0, L), :] * scale
            vb = v_ref[0, pl.ds(r0, L), :].astype(BF16)
            kb = k.astype(BF16)
            qk = lax.dot_general(q.astype(BF16), kb, _NT, preferred_element_type=F32)
            k_t = k.T
            gcol = gc_ref[0, pl.ds(r0, L), :]
            for d in range(N_DIR):
                ch_i = CH_I + d * M_HEADS + head
                ch_f = CH_F + d * M_HEADS + head
                bc_col = _lane_pick(gcol, ch_f)
                i_row = gr_ref[0, pl.ds(ch_i, 1), pl.ds(r0, L)]
                bc_row = gr_ref[0, pl.ds(ch_f, 1), pl.ds(r0, L)]
                a_row = i_row - bc_row
                mask = (cc <= rr) if d == 0 else (cc >= rr)
                log_intra = jnp.where(mask, bc_col + a_row, -jnp.inf)
                m_intra = jnp.max(log_intra, axis=1, keepdims=True)
                p = qk * jnp.exp(log_intra - m_intra)
                b_last = bc_row[:, L - 1:L] if d == 0 else bc_row[:, 0:1]
                log_state = b_last + a_row
                m_loc = jnp.max(log_state, axis=1, keepdims=True)
                w_state = jnp.exp(log_state - m_loc)
                chains.append(dict(
                    idx=d * nc + c, d=d, r0=r0, vb=vb, kb=kb, pb=p.astype(BF16),
                    den=jnp.sum(p, axis=1, keepdims=True), m_intra=m_intra,
                    kw=(k_t * w_state).astype(BF16), w8=jnp.broadcast_to(w_state, (8, L)).astype(BF16),
                    b_last=b_last, m_loc=m_loc))
        for c in chains:
            hn_ref[c["d"], pl.ds(c["r0"], L), :] = jnp.dot(c["pb"], c["vb"], preferred_element_type=F32)
            den_ref[c["d"], pl.ds(c["r0"], L), :] = jnp.broadcast_to(c["den"], (L, LANES))
            mi_ref[c["d"], pl.ds(c["r0"], L), :] = jnp.broadcast_to(c["m_intra"], (L, LANES))
        for c in chains:
            cu_ref[c["idx"]] = jnp.dot(c["kw"], c["vb"], preferred_element_type=F32)
            nu_ref[c["idx"]] = jnp.dot(c["w8"], c["kb"], preferred_element_type=F32)
            bl_ref[c["idx"]] = bcast8(c["b_last"])
            ml_ref[c["idx"]] = bcast8(c["m_loc"])

    def local_trip(i, _):
        local([M_CHUNKS_PER_ITER * i + t for t in range(M_CHUNKS_PER_ITER)])
        return 0

    lax.fori_loop(0, nc // M_CHUNKS_PER_ITER, local_trip, 0)

    c_ref[...] = jnp.zeros(c_ref.shape, F32)

    def scan(j, carry):
        out = []
        for d in range(N_DIR):
            m_st, n_st = carry[d]
            idx = d * nc + (j if d == 0 else nc - 1 - j)
            c_st = c_ref[d]
            cs_ref[idx] = c_st.astype(BF16)
            ns_ref[idx] = n_st
            ms_ref[idx] = m_st
            bl = bl_ref[idx]
            ml = ml_ref[idx]
            m_new = jnp.maximum(bl + m_st, ml)
            decay = jnp.exp(bl + m_st - m_new)
            gain = jnp.exp(ml - m_new)
            c_ref[d] = decay[0:1, 0:1] * c_st + gain[0:1, 0:1] * cu_ref[idx]
            out.append((m_new, decay * n_st + gain * nu_ref[idx]))
        return tuple(out)

    m0 = jnp.full((8, LANES), -jnp.inf, F32)
    n0 = jnp.zeros((8, M_DK), F32)
    lax.fori_loop(0, nc, scan, ((m0, n0), (m0, n0)))

    def wide(x):
        return jnp.concatenate([x] * (M_DV // LANES), axis=1)

    def combine(c, _):
        r0 = pl.multiple_of(c * L, L)
        q = q_ref[0, pl.ds(r0, L), :]
        qb = q.astype(BF16)
        gcol = gc_ref[0, pl.ds(r0, L), :]
        hh = None
        for d in range(N_DIR):
            idx = d * nc + c
            bc = jnp.broadcast_to(_lane_pick(gcol, CH_F + d * M_HEADS + head), (L, LANES))
            den_i = den_ref[d, pl.ds(r0, L), :]
            m_i = mi_ref[d, pl.ds(r0, L), :]
            log_inter = bc + ms_ref[idx][0:1, :]
            m_row = jnp.maximum(log_inter, m_i)
            w_i = jnp.exp(m_i - m_row)
            w_x = jnp.exp(log_inter - m_row)
            q_c = jnp.dot(qb, cs_ref[idx], preferred_element_type=F32)
            q_n = jnp.broadcast_to(jnp.sum(q * ns_ref[idx][0:1, :], axis=1, keepdims=True), (L, LANES))
            num = wide(w_i) * hn_ref[d, pl.ds(r0, L), :] + wide(w_x) * q_c
            den = w_i * den_i + w_x * q_n
            h_d = num * wide(1.0 / jnp.maximum(jnp.abs(den), jnp.exp(-m_row)))
            hh = h_d if hh is None else hh + h_d
        hh = hh * lax.rsqrt(jnp.mean(hh * hh, axis=-1, keepdims=True) + RMS_EPS)
        hh = hh * ng_ref[0]
        y_ref[0, pl.ds(r0, L), :] = (_sigmoid(o_ref[0, pl.ds(r0, L), :]) * hh).astype(y_ref.dtype)
        return 0

    lax.fori_loop(0, nc, combine, 0, unroll=2)


def _mlstm(proj3, gcol, grow, norm_g, off_q, off_k, off_v, off_o):
    b, s, _ = proj3.shape
    assert s % (M_CHUNK * M_CHUNKS_PER_ITER) == 0
    nc = s // M_CHUNK
    bq, bk = off_q // M_DK, off_k // M_DK
    bv, bo = off_v // M_DV, off_o // M_DV
    return pl.pallas_call(
        _mlstm_kernel,
        grid=(b, M_HEADS),
        in_specs=[
            pl.BlockSpec((1, s, M_DK), lambda i, h: (i, 0, bq + h)),
            pl.BlockSpec((1, s, M_DK), lambda i, h: (i, 0, bk + h)),
            pl.BlockSpec((1, s, M_DV), lambda i, h: (i, 0, bv + h)),
            pl.BlockSpec((1, s, M_DV), lambda i, h: (i, 0, bo + h)),
            pl.BlockSpec((1, s, LANES), lambda i, h: (i, 0, 0)),
            pl.BlockSpec((1, LANES, s), lambda i, h: (i, 0, 0)),
            pl.BlockSpec((1, 1, M_DV), lambda i, h: (h, 0, 0)),
        ],
        out_specs=pl.BlockSpec((1, s, M_DV), lambda i, h: (i, 0, h)),
        out_shape=jax.ShapeDtypeStruct((b, s, M_HEADS * M_DV), BF16),
        scratch_shapes=[
            pltpu.VMEM((N_DIR, s, M_DV), F32),
            pltpu.VMEM((N_DIR, s, LANES), F32),
            pltpu.VMEM((N_DIR, s, LANES), F32),
            pltpu.VMEM((N_DIR * nc, M_DK, M_DV), F32),
            pltpu.VMEM((N_DIR * nc, 8, M_DK), F32),
            pltpu.VMEM((N_DIR * nc, 8, LANES), F32),
            pltpu.VMEM((N_DIR * nc, 8, LANES), F32),
            pltpu.VMEM((N_DIR * nc, M_DK, M_DV), BF16),
            pltpu.VMEM((N_DIR * nc, 8, M_DK), F32),
            pltpu.VMEM((N_DIR * nc, 8, LANES), F32),
            pltpu.VMEM((N_DIR, M_DK, M_DV), F32),
        ],
        compiler_params=_cparams(("parallel", "arbitrary")),
        name="mlstm",
    )(proj3, proj3, proj3, proj3, gcol, grow, norm_g)


def _delta_kernel(q_ref, k_ref, v_ref, z_ref, cwq_ref, cwk_ref, cwv_ref, gc_ref, gr_ref, ng_ref, y_ref,
                  xp_ref, qs_ref, ks_ref, vs_ref, kq_ref, n_ref, dec_ref, o_ref, st_ref):
    t = pl.program_id(0)
    head = jnp.minimum(t, pl.num_programs(0) - 2) % D_HEADS
    s = q_ref.shape[1]
    L = D_CHUNK
    G = D_GROUP
    nc = s // L
    ng = s // G
    halo = CONV_HALO
    pad = CONV_W // 2

    @pl.when(t == 0)
    def _():
        for r in (kq_ref, n_ref, dec_ref, o_ref):
            r[...] = jnp.zeros(r.shape, r.dtype)

    xp_ref[0:halo, :] = jnp.zeros((halo, D_DIM), F32)
    xp_ref[s + halo:s + 2 * halo, :] = jnp.zeros((halo, D_DIM), F32)

    def conv_into(x_ref, cw_ref, dst_ref, l2, mult):
        xp_ref[halo:s + halo, :] = x_ref[0]
        for blk in range(ng):
            base = halo + blk * G - pad
            acc = xp_ref[base:base + G, :] * cw_ref[0:1, :]
            for j in range(1, CONV_W):
                acc = acc + xp_ref[base + j:base + j + G, :] * cw_ref[j:j + 1, :]
            y = acc * _sigmoid(acc)
            if l2:
                y = y * (lax.rsqrt(jnp.sum(y * y, axis=-1, keepdims=True) + L2_EPS) * mult)
            dst_ref[blk * G:(blk + 1) * G, :] = y

    conv_into(q_ref, cwq_ref, qs_ref, True, D_DIM ** -0.5)
    conv_into(k_ref, cwk_ref, ks_ref, True, 1.0)
    conv_into(v_ref, cwv_ref, vs_ref, False, 1.0)

    st_ref[...] = jnp.zeros(st_ref.shape, F32)

    def scan_step(scan, j):
        for d in range(N_DIR):
            c = j if d == 0 else nc - 1 - j
            r0 = pl.multiple_of(c * L, L)
            idx = (scan + d) * nc + c
            st = st_ref[d]
            res = jnp.dot(kq_ref[idx], st.astype(BF16), preferred_element_type=F32)
            st_ref[d] = dec_ref[idx][0:1, :] * st + res[0:D_DIM] + n_ref[idx]
            o_ref[scan + d, pl.ds(r0, L), :] = o_ref[scan + d, pl.ds(r0, L), :] + res[D_DIM:D_DIM + L]

    tchunk = lax.broadcasted_iota(I32, (D_DIM, D_GROUP), 1) // D_CHUNK
    rr = lax.broadcasted_iota(I32, (G, G), 0)
    cc = lax.broadcasted_iota(I32, (G, G), 1)
    same = (rr // L) == (cc // L)
    pr = lax.broadcasted_iota(I32, (L, G), 0)
    pc = lax.broadcasted_iota(I32, (L, G), 1)
    eye_p = jnp.where((pc % L) == pr, 1.0, 0.0).astype(F32)

    def pack(m):
        out = m[0:L]
        for i in range(1, D_PER_GROUP):
            out = out + m[i * L:(i + 1) * L]
        return out

    def unpack(p):
        return jnp.where(same, jnp.concatenate([p] * D_PER_GROUP, axis=0), 0.0)

    scan_per_trip = nc // (ng // D_GROUPS_PER_ITER)

    def groups(fill, scan, trip):
        pending = iter(range(scan_per_trip))

        def scan_some(n):
            for _ in range(n):
                j = next(pending, None)
                if j is not None:
                    scan_step(scan, trip * scan_per_trip + j)

        chains = []
        for gi in [D_GROUPS_PER_ITER * trip + g for g in range(D_GROUPS_PER_ITER)]:
            r0 = pl.multiple_of(gi * G, G)
            kk_ = ks_ref[pl.ds(r0, G), :]
            qq_ = qs_ref[pl.ds(r0, G), :]
            vv_ = vs_ref[pl.ds(r0, G), :]
            kb = kk_.astype(BF16)
            kk = lax.dot_general(kb, kb, _NT, preferred_element_type=F32)
            qk = lax.dot_general(qq_.astype(BF16), kb, _NT, preferred_element_type=F32)
            gcol = gc_ref[0, pl.ds(r0, G), :]
            for d in range(N_DIR):
                ch = d * D_HEADS + head
                g_col = _lane_pick(gcol, CH_G + ch)
                b_col = _lane_pick(gcol, CH_B + ch)
                t_col = _lane_pick(gcol, CH_T + ch)
                g_row = gr_ref[0, pl.ds(CH_G + ch, 1), pl.ds(r0, G)]
                tri = (cc <= rr) if d == 0 else (cc >= rr)
                strict = (cc < rr) if d == 0 else (cc > rr)
                gam = jnp.exp(jnp.where(tri, jnp.where(same, g_col - g_row, -jnp.inf), -jnp.inf))
                x = jnp.where(strict, -(b_col * kk * gam), 0.0)
                eg = jnp.exp(g_col)
                chains.append(dict(
                    gi=gi, slot=fill + d, r0=r0, x=x, xp=pack(x), attn=(qk * gam).astype(BF16),
                    rhs=jnp.concatenate([b_col * vv_, (b_col * eg) * kk_], axis=1).astype(BF16),
                    qg=qq_ * eg, kdt=(kk_ * jnp.exp(t_col - g_col)).T, dec=jnp.exp(t_col)))
            scan_some(-(-(scan_per_trip - D_SCAN_STEPS_BETWEEN_STAGES) // D_GROUPS_PER_ITER))
        for c in chains:
            c["tp"] = eye_p + c["xp"]
            c["p"] = _bdot(c["xp"], c["x"])
        scan_some(1)
        for _it in range(4):
            for c in chains:
                res = _bdot(jnp.concatenate([c["tp"], c["p"]], axis=0), unpack(c["p"]))
                c["tp"] = c["tp"] + res[0:L]
                c["p"] = res[L:2 * L]
            scan_some(1)
        for c in chains:
            c["tp"] = c["tp"] + _bdot(c["tp"], unpack(c["p"]))
        scan_some(1)
        for c in chains:
            c["uw"] = _bdot(unpack(c["tp"]), c["rhs"]).astype(BF16)
        scan_some(1)
        for c in chains:
            au = jnp.dot(c["attn"], c["uw"], preferred_element_type=F32)
            o_ref[c["slot"], pl.ds(c["r0"], G), :] = au[:, 0:D_DIM]
            c["qe"] = (c["qg"] - au[:, D_DIM:2 * D_DIM]).astype(BF16)
        scan_some(1)
        for c in chains:
            for ci in range(D_PER_GROUP):
                idx = c["slot"] * nc + c["gi"] * D_PER_GROUP + ci
                ku = _bdot(jnp.where(tchunk == ci, c["kdt"], 0.0), c["uw"])
                n_ref[idx] = ku[:, 0:D_DIM]
                kq_ref[idx, 0:D_DIM, :] = (-ku[:, D_DIM:2 * D_DIM]).astype(BF16)
                kq_ref[idx, D_DIM:D_DIM + L, :] = c["qe"][ci * L:(ci + 1) * L]
                dec_ref[idx] = jnp.broadcast_to(c["dec"][ci * L:ci * L + 8], (8, D_DIM))
        scan_some(scan_per_trip)

    def finish(scan, gi):
        r0 = pl.multiple_of(gi * G, G)
        o = o_ref[scan, pl.ds(r0, G), :] + o_ref[scan + 1, pl.ds(r0, G), :]
        o = o * lax.rsqrt(jnp.mean(o * o, axis=-1, keepdims=True) + RMS_EPS) * ng_ref[...]
        z = z_ref[0, pl.ds(r0, G), :]
        y_ref[0, pl.ds(r0, G), :] = (o * (z * _sigmoid(z))).astype(y_ref.dtype)

    fill = (t % 2) * N_DIR
    scan = N_DIR - fill

    def group_batch(i, carry):
        groups(fill, scan, i)
        return carry

    def finish_batch(gi, carry):
        finish(scan, gi)
        return carry

    lax.fori_loop(0, ng // D_GROUPS_PER_ITER, group_batch, 0)
    lax.fori_loop(0, ng, finish_batch, 0, unroll=4)


def _delta(proj3, conv_w, gcol, grow, norm_g, off_q, off_k, off_v, off_z):
    b, s, _ = proj3.shape
    w = D_HEADS * D_DIM
    nc = s // D_CHUNK
    trips = s // (D_GROUP * D_GROUPS_PER_ITER)
    assert s % (D_GROUP * D_GROUPS_PER_ITER) == 0 and nc % trips == 0
    bq, bk, bv, bz = (o // D_DIM for o in (off_q, off_k, off_v, off_z))
    n_units = b * D_HEADS
    cur = lambda t: jnp.minimum(t, n_units - 1)
    prev = lambda t: jnp.maximum(t - 1, 0)
    col = lambda unit, blk: lambda t: (unit(t) // D_HEADS, 0, blk + unit(t) % D_HEADS)
    return pl.pallas_call(
        _delta_kernel,
        grid=(n_units + 1,),
        in_specs=[
            pl.BlockSpec((1, s, D_DIM), col(cur, bq)),
            pl.BlockSpec((1, s, D_DIM), col(cur, bk)),
            pl.BlockSpec((1, s, D_DIM), col(cur, bv)),
            pl.BlockSpec((1, s, D_DIM), col(prev, bz)),
            pl.BlockSpec((CONV_W, D_DIM), lambda t: (0, cur(t) % D_HEADS)),
            pl.BlockSpec((CONV_W, D_DIM), lambda t: (0, D_HEADS + cur(t) % D_HEADS)),
            pl.BlockSpec((CONV_W, D_DIM), lambda t: (0, 2 * D_HEADS + cur(t) % D_HEADS)),
            pl.BlockSpec((1, s, LANES), lambda t: (cur(t) // D_HEADS, 0, 0)),
            pl.BlockSpec((1, LANES, s), lambda t: (cur(t) // D_HEADS, 0, 0)),
            pl.BlockSpec((1, D_DIM), lambda t: (0, 0)),
        ],
        out_specs=pl.BlockSpec((1, s, D_DIM), col(prev, 0)),
        out_shape=jax.ShapeDtypeStruct((b, s, w), BF16),
        scratch_shapes=[
            pltpu.VMEM((s + 2 * CONV_HALO, D_DIM), F32),
            pltpu.VMEM((s, D_DIM), F32),
            pltpu.VMEM((s, D_DIM), F32),
            pltpu.VMEM((s, D_DIM), F32),
            pltpu.VMEM((2 * N_DIR * nc, D_DIM + D_CHUNK, D_DIM), BF16),
            pltpu.VMEM((2 * N_DIR * nc, D_DIM, D_DIM), F32),
            pltpu.VMEM((2 * N_DIR * nc, 8, D_DIM), F32),
            pltpu.VMEM((2 * N_DIR, s, D_DIM), F32),
            pltpu.VMEM((N_DIR, D_DIM, D_DIM), F32),
        ],
        compiler_params=_cparams(("arbitrary",)),
        name="delta",
    )(proj3, proj3, proj3, proj3, conv_w, conv_w, conv_w, gcol, grow, norm_g)


def _mix_kernel(n_gate_blocks, x_ref, ym_ref, yd_ref, *refs):
    gate_refs, (wbm_ref, wbd_ref, wo_ref, g2_ref, o_ref, h_ref) = refs[:2 * n_gate_blocks], refs[2 * n_gate_blocks:]
    gate = lambda rs: _sigmoid(jnp.concatenate([r[...] for r in rs], axis=1))
    a = jnp.dot(ym_ref[...], wbm_ref[...], preferred_element_type=F32)
    b = jnp.dot(yd_ref[...], wbd_ref[...], preferred_element_type=F32)
    mixed = gate(gate_refs[:n_gate_blocks]) * a + gate(gate_refs[n_gate_blocks:]) * b
    x = x_ref[...] + jnp.dot(mixed.astype(BF16), wo_ref[...], preferred_element_type=F32)
    o_ref[...] = x
    ms = jnp.mean(x * x, axis=-1, keepdims=True)
    h_ref[...] = (x * lax.rsqrt(ms + RMS_EPS) * g2_ref[...]).astype(BF16)


def _mix(x2, ym2, yd2, proj2, off_gates, wbm, wbd, wo, g2, tm):
    m, d = x2.shape
    const = lambda i: (0, 0)
    gw = math.gcd(off_gates, d)
    gate_specs = [pl.BlockSpec((tm, gw), functools.partial(lambda i, c: (i, c), c=(off_gates + t * gw) // gw))
                  for t in range(2 * d // gw)]
    return pl.pallas_call(
        functools.partial(_mix_kernel, d // gw),
        grid=(m // tm,),
        in_specs=[
            pl.BlockSpec((tm, d), lambda i: (i, 0)),
            pl.BlockSpec((tm, ym2.shape[1]), lambda i: (i, 0)),
            pl.BlockSpec((tm, yd2.shape[1]), lambda i: (i, 0)),
            *gate_specs,
            pl.BlockSpec(wbm.shape, const),
            pl.BlockSpec(wbd.shape, const),
            pl.BlockSpec(wo.shape, const),
            pl.BlockSpec((1, d), const),
        ],
        out_specs=[pl.BlockSpec((tm, d), lambda i: (i, 0)), pl.BlockSpec((tm, d), lambda i: (i, 0))],
        out_shape=[jax.ShapeDtypeStruct((m, d), F32), jax.ShapeDtypeStruct((m, d), BF16)],
        compiler_params=_cparams(("parallel",)),
        name="mix",
    )(x2, ym2, yd2, *([proj2] * len(gate_specs)), wbm, wbd, wo, g2)


def _ffn_kernel(x_ref, h_ref, gf_ref, w1_ref, w2_ref, o_ref, acc_ref):
    f = pl.program_id(1)

    @pl.when(f == 0)
    def _():
        acc_ref[...] = jnp.zeros(acc_ref.shape, F32)

    hid = jnp.dot(h_ref[...], w1_ref[...], preferred_element_type=F32)
    act = jnp.square(jnp.maximum(hid, 0.0)).astype(BF16)
    acc_ref[...] += jnp.dot(act, w2_ref[...], preferred_element_type=F32)

    @pl.when(f == pl.num_programs(1) - 1)
    def _():
        x = x_ref[...] + acc_ref[...]
        ms = jnp.mean(x * x, axis=-1, keepdims=True)
        o_ref[...] = x * lax.rsqrt(ms + RMS_EPS) * gf_ref[...]


def _ffn(x2, h2, gf, w1, w2, tm, tf):
    m, d = x2.shape
    dff = w1.shape[1]
    return pl.pallas_call(
        _ffn_kernel,
        grid=(m // tm, dff // tf),
        in_specs=[
            pl.BlockSpec((tm, d), lambda i, f: (i, 0)),
            pl.BlockSpec((tm, d), lambda i, f: (i, 0)),
            pl.BlockSpec((1, d), lambda i, f: (0, 0)),
            pl.BlockSpec((d, tf), lambda i, f: (0, f)),
            pl.BlockSpec((tf, d), lambda i, f: (f, 0)),
        ],
        out_specs=pl.BlockSpec((tm, d), lambda i, f: (i, 0)),
        out_shape=jax.ShapeDtypeStruct((m, d), F32),
        scratch_shapes=[pltpu.VMEM((tm, d), F32)],
        compiler_params=_cparams(("parallel", "arbitrary")),
        name="ffn",
    )(x2, h2, gf, w1, w2)


def _pick_tile(n, candidates):
    for c in candidates:
        if n % c == 0:
            return c
    raise ValueError(f"no tile in {candidates} divides {n}")


def _layer(x, norm1_g, w_in, i_bias, f_bias, m_norm_g, conv_w, a_log, dt_bias, d_norm_g,
           w_bm, w_bd, w_out, norm2_g, w_ff1, w_ff2, out_g):
    b, s, d = x.shape
    m = b * s
    mw, mqk, dw = M_HEADS * M_DV, M_HEADS * M_DK, D_HEADS * D_DIM
    nm, nd = N_DIR * M_HEADS, N_DIR * D_HEADS
    splits = (mqk, mqk, mw, mw, nm, nm, 3 * dw, dw, nd, nd, 2 * d)
    bounds = [0]
    for w_ in splits:
        bounds.append(bounds[-1] + w_)
    wt = w_in.T.astype(BF16)
    rows = lambda first, last: wt[bounds[first]:bounds[last + 1]]
    seg_bounds = [(bounds[0], bounds[4]), (bounds[6], bounds[8]), (bounds[10], bounds[11])]
    n_small = 2 * nm + 2 * nd
    wt_small = jnp.concatenate([rows(4, 5), rows(8, 9), jnp.zeros((LANES - n_small, d), BF16)], axis=0)
    zpad = lambda n: jnp.zeros((n,), F32)
    bias_row = jnp.concatenate([i_bias.reshape(-1), f_bias.reshape(-1), dt_bias.reshape(-1), zpad(LANES - CH_B)])
    alog_row = jnp.concatenate([zpad(CH_G), a_log.reshape(-1), zpad(LANES - CH_B)])
    gate_params = jnp.concatenate([bias_row[None], alog_row[None], jnp.zeros((6, LANES), F32)], axis=0)

    x2 = x.reshape(m, d)
    tm = _pick_tile(m, (512, 256, 128))
    tn = next(c for c in (1024, 512, 256) if all((hi - lo) % c == 0 for lo, hi in seg_bounds))
    proj2, smalls2 = _inproj(x2, norm1_g.reshape(1, d), wt, wt_small, seg_bounds,
                             _pick_tile(m, (1024, 512, 256)), tn)
    proj3 = proj2.reshape(b, s, -1)
    gcol, grow = _gate_prep(smalls2.reshape(b, s, LANES), gate_params)

    off_d = 2 * mqk + 2 * mw
    off_g = off_d + 4 * dw
    ym = _mlstm(proj3, gcol, grow, m_norm_g.reshape(M_HEADS, 1, M_DV), 0, mqk, 2 * mqk, 2 * mqk + mw)
    yd = _delta(proj3, conv_w, gcol, grow, d_norm_g.reshape(1, D_DIM), off_d, off_d + dw, off_d + 2 * dw, off_d + 3 * dw)

    x1, h2 = _mix(x2, ym.reshape(m, mw), yd.reshape(m, dw), proj2, off_g, w_bm.astype(BF16), w_bd.astype(BF16),
                  w_out.astype(BF16), norm2_g.reshape(1, d), _pick_tile(m, (256, 128)))
    tf = _pick_tile(w_ff1.shape[1], (1024, 512, 256))
    out = _ffn(x1, h2, out_g.reshape(1, d), w_ff1.astype(BF16), w_ff2.astype(BF16), tm, tf)
    return out.reshape(b, s, d)


def kernel(x, norm1_g, w_in, mlstm_i_bias, mlstm_f_bias, mlstm_norm_g, delta_conv_w, delta_a_log,
           delta_dt_bias, delta_norm_g, w_branch_m, w_branch_d, w_out, norm2_g, w_ff1, w_ff2, norm_f_g):
    depth = w_in.shape[0]
    assert depth == 1, "the fused FFN + final-norm epilogue assumes a single layer"
    return _layer(x, norm1_g[0], w_in[0], mlstm_i_bias[0], mlstm_f_bias[0], mlstm_norm_g[0], delta_conv_w[0],
                  delta_a_log[0], delta_dt_bias[0], delta_norm_g[0], w_branch_m[0], w_branch_d[0], w_out[0],
                  norm2_g[0], w_ff1[0], w_ff2[0], norm_f_g)
```

```python
import functools
import math

import jax
import jax.numpy as jnp
import numpy as np
from jax import lax
from jax.experimental import pallas as pl
from jax.experimental.pallas import tpu as pltpu

F32 = jnp.float32
BF16 = jnp.bfloat16
I32 = jnp.int32

N_DIR = 2
M_HEADS = 4
M_DK = 128
M_DV = 256
D_HEADS = 8
D_DIM = 128
CONV_W = 5
RMS_EPS = 1e-6
L2_EPS = 1e-6

LANES = 128
BF16_SUBLANES = 16
M_CHUNK = 256
M_CHUNKS_PER_ITER = 4
D_CHUNK = 64
D_GROUP = 256
D_PER_GROUP = D_GROUP // D_CHUNK
D_GROUPS_PER_ITER = 4
D_SCAN_STEPS_BETWEEN_STAGES = 8
GATE_ROWS = 256
GATE_BLOCKS_PER_STEP = 4
CONV_HALO = 8

CH_I = 0
CH_F = 8
CH_G = 16
CH_B = 32
CH_T = 48

VMEM_LIMIT = 56 * 1024 * 1024

_NT = (((1,), (1,)), ((), ()))


def _cparams(sem):
    return pltpu.CompilerParams(dimension_semantics=sem, vmem_limit_bytes=VMEM_LIMIT)


def _bdot(a, b):
    return jnp.dot(a.astype(BF16), b.astype(BF16), preferred_element_type=F32)


def _bdot_nt(a, b):
    return lax.dot_general(a.astype(BF16), b.astype(BF16), _NT, preferred_element_type=F32)


def _sigmoid(x):
    return 1.0 / (1.0 + jnp.exp(-x))


def _softplus(x):
    return jnp.maximum(x, 0.0) + jnp.log1p(jnp.exp(-jnp.abs(x)))


def _cast_blocks(w_refs, o_refs):
    for w_ref, o_ref in zip(w_refs, o_refs):
        o_ref[...] = w_ref[...].astype(o_ref.dtype)


def _cast_specs(weights, n_steps, step_of):
    specs, shapes = [], []
    for w in weights:
        rows, cols = w.shape
        assert rows % (n_steps * BF16_SUBLANES) == 0
        specs.append(pl.BlockSpec((rows // n_steps, cols), lambda *g: (step_of(*g), 0)))
        shapes.append(jax.ShapeDtypeStruct((rows, cols), BF16))
    return specs, shapes


def _lane_pick(x, ch):
    lane = lax.broadcasted_iota(I32, x.shape, 1)
    return jnp.sum(jnp.where(lane == ch, x, 0.0), axis=1, keepdims=True)


def _inproj_kernel(x_ref, g_ref, w_ref, ws_ref, o_ref, os_ref, hn_ref):
    @pl.when(pl.program_id(1) == 0)
    def _():
        x = x_ref[...]
        ms = jnp.mean(x * x, axis=-1, keepdims=True)
        hn = (x * lax.rsqrt(ms + RMS_EPS) * g_ref[...]).astype(BF16)
        hn_ref[...] = hn
        os_ref[...] = lax.dot_general(hn, ws_ref[...], _NT, preferred_element_type=F32)

    o_ref[...] = lax.dot_general(hn_ref[...], w_ref[...], _NT, preferred_element_type=F32)


def _inproj(x2, g, wt, wt_small, seg_bounds, tm, tn):
    m, d = x2.shape
    starts = [0]
    for lo, hi in seg_bounds:
        starts.append(starts[-1] + (hi - lo) // tn)

    def w_row(i, j):
        row = seg_bounds[0][0] + j * tn
        for k in range(1, len(seg_bounds)):
            row = jnp.where(j >= starts[k], seg_bounds[k][0] + (j - starts[k]) * tn, row)
        return pl.multiple_of(row, BF16_SUBLANES), 0

    return pl.pallas_call(
        _inproj_kernel,
        grid=(m // tm, starts[-1]),
        in_specs=[
            pl.BlockSpec((tm, d), lambda i, j: (i, 0)),
            pl.BlockSpec((1, d), lambda i, j: (0, 0)),
            pl.BlockSpec((pl.Element(tn), pl.Element(d)), w_row),
            pl.BlockSpec((LANES, d), lambda i, j: (0, 0)),
        ],
        out_specs=[
            pl.BlockSpec((tm, tn), lambda i, j: (i, j)),
            pl.BlockSpec((tm, LANES), lambda i, j: (i, 0)),
        ],
        out_shape=[jax.ShapeDtypeStruct((m, starts[-1] * tn), F32), jax.ShapeDtypeStruct((m, LANES), F32)],
        scratch_shapes=[pltpu.VMEM((tm, d), BF16)],
        compiler_params=_cparams(("parallel", "arbitrary")),
        name="inproj",
    )(x2, g, wt, wt_small)


def _cumsum_matrices():
    rr, cc = np.indices((GATE_ROWS, GATE_ROWS))
    low, upp = cc <= rr, cc >= rr
    same_d = (rr // D_CHUNK) == (cc // D_CHUNK)
    same_m = (rr // M_CHUNK) == (cc // M_CHUNK)
    mats = np.concatenate([
        low & same_m,
        upp & same_m,
        low & same_d,
        upp & same_d,
        same_d,
    ], axis=0)
    return jnp.asarray(mats.astype(np.float32), dtype=BF16)


def _gate_kernel(sm_ref, par_ref, mats_ref, col_ref, row_ref):
    r = GATE_ROWS
    lane = lax.broadcasted_iota(I32, (r, LANES), 1)
    neg_a = -jnp.exp(par_ref[1:2, :])
    xes, x3s = [], []
    for blk in range(GATE_BLOCKS_PER_STEP):
        x = sm_ref[0, blk * r:(blk + 1) * r, :] + par_ref[0:1, :]
        logf = -_softplus(-x)
        g = neg_a * _softplus(x)
        beta = _sigmoid(x)
        xe = jnp.where(lane < CH_F, x,
                       jnp.where(lane < CH_G, logf,
                                 jnp.where(lane < CH_B, g,
                                           jnp.where(lane < CH_T, beta, 0.0))))
        hi = xe.astype(BF16)
        r1 = xe - hi.astype(F32)
        mid = r1.astype(BF16)
        lo = (r1 - mid.astype(F32)).astype(BF16)
        xes.append(xe)
        x3s.append(jnp.concatenate([hi, mid, lo], axis=1))
    y3s = [jnp.dot(mats_ref[...], x3, preferred_element_type=F32) for x3 in x3s]
    half_m = CH_F + M_HEADS
    half_d = CH_G + D_HEADS
    for blk, (xe, y3) in enumerate(zip(xes, y3s)):
        y = y3[:, 0:LANES] + y3[:, LANES:2 * LANES] + y3[:, 2 * LANES:3 * LANES]
        pm, sm_, pd, sd, td = (y[i * r:(i + 1) * r] for i in range(5))
        out = jnp.where(lane < CH_F, xe,
              jnp.where(lane < half_m, pm,
              jnp.where(lane < CH_G, sm_,
              jnp.where(lane < half_d, pd,
              jnp.where(lane < CH_B, sd,
              jnp.where(lane < CH_T, xe,
              jnp.where(lane < CH_T + N_DIR * D_HEADS, pltpu.roll(td, CH_T - CH_G, axis=1), 0.0)))))))
        col_ref[0, blk * r:(blk + 1) * r, :] = out
        row_ref[0, :, blk * r:(blk + 1) * r] = out.T


def _gate_prep(smalls3, params):
    b, s, _ = smalls3.shape
    r = GATE_ROWS * GATE_BLOCKS_PER_STEP
    assert s % r == 0
    return pl.pallas_call(
        _gate_kernel,
        grid=(b, s // r),
        in_specs=[
            pl.BlockSpec((1, r, LANES), lambda i, j: (i, j, 0)),
            pl.BlockSpec((8, LANES), lambda i, j: (0, 0)),
            pl.BlockSpec((5 * GATE_ROWS, GATE_ROWS), lambda i, j: (0, 0)),
        ],
        out_specs=[
            pl.BlockSpec((1, r, LANES), lambda i, j: (i, j, 0)),
            pl.BlockSpec((1, LANES, r), lambda i, j: (i, 0, j)),
        ],
        out_shape=[jax.ShapeDtypeStruct((b, s, LANES), F32), jax.ShapeDtypeStruct((b, LANES, s), F32)],
        compiler_params=_cparams(("parallel", "parallel")),
        name="gateprep",
    )(smalls3, params, _cumsum_matrices())


def _mlstm_kernel(n_cast, q_ref, k_ref, v_ref, o_ref, gc_ref, gr_ref, ng_ref, *refs):
    w_refs, y_ref, wo_refs = refs[:n_cast], refs[n_cast], refs[n_cast + 1:2 * n_cast + 1]
    hn_ref, den_ref, mi_ref, cu_ref, nu_ref, bl_ref, ml_ref, cs_ref, ns_ref, ms_ref, c_ref = refs[2 * n_cast + 1:]
    _cast_blocks(w_refs, wo_refs)
    head = pl.program_id(1)
    s = q_ref.shape[1]
    L = M_CHUNK
    nc = s // L
    scale = M_DK ** -0.5
    rr = lax.broadcasted_iota(I32, (L, L), 0)
    cc = lax.broadcasted_iota(I32, (L, L), 1)

    def bcast8(x11):
        return jnp.broadcast_to(x11, (8, LANES))

    def local(cs):
        chains = []
        for c in cs:
            r0 = pl.multiple_of(c * L, L)
            q = q_ref[0, pl.ds(r0, L), :]
            k = k_ref[0, pl.ds(r0, L), :] * scale
            vb = v_ref[0, pl.ds(r0, L), :].astype(BF16)
            kb = k.astype(BF16)
            qk = lax.dot_general(q.astype(BF16), kb, _NT, preferred_element_type=F32)
            k_t = k.T
            gcol = gc_ref[0, pl.ds(r0, L), :]
            for d in range(N_DIR):
                ch_i = CH_I + d * M_HEADS + head
                ch_f = CH_F + d * M_HEADS + head
                bc_col = _lane_pick(gcol, ch_f)
                i_row = gr_ref[0, pl.ds(ch_i, 1), pl.ds(r0, L)]
                bc_row = gr_ref[0, pl.ds(ch_f, 1), pl.ds(r0, L)]
                a_row = i_row - bc_row
                mask = (cc <= rr) if d == 0 else (cc >= rr)
                log_intra = jnp.where(mask, bc_col + a_row, -jnp.inf)
                m_intra = jnp.max(log_intra, axis=1, keepdims=True)
                p = qk * jnp.exp(log_intra - m_intra)
                b_last = bc_row[:, L - 1:L] if d == 0 else bc_row[:, 0:1]
                log_state = b_last + a_row
                m_loc = jnp.max(log_state, axis=1, keepdims=True)
                w_state = jnp.exp(log_state - m_loc)
                chains.append(dict(
                    idx=d * nc + c, d=d, r0=r0, vb=vb, kb=kb, pb=p.astype(BF16),
                    den=jnp.sum(p, axis=1, keepdims=True), m_intra=m_intra,
                    kw=(k_t * w_state).astype(BF16), w8=jnp.broadcast_to(w_state, (8, L)).astype(BF16),
                    b_last=b_last, m_loc=m_loc))
        for c in chains:
            hn_ref[c["d"], pl.ds(c["r0"], L), :] = jnp.dot(c["pb"], c["vb"], preferred_element_type=F32)
            den_ref[c["d"], pl.ds(c["r0"], L), :] = jnp.broadcast_to(c["den"], (L, LANES))
            mi_ref[c["d"], pl.ds(c["r0"], L), :] = jnp.broadcast_to(c["m_intra"], (L, LANES))
        for c in chains:
            cu_ref[c["idx"]] = jnp.dot(c["kw"], c["vb"], preferred_element_type=F32)
            nu_ref[c["idx"]] = jnp.dot(c["w8"], c["kb"], preferred_element_type=F32)
            bl_ref[c["idx"]] = bcast8(c["b_last"])
            ml_ref[c["idx"]] = bcast8(c["m_loc"])

    def local_trip(i, _):
        local([M_CHUNKS_PER_ITER * i + t for t in range(M_CHUNKS_PER_ITER)])
        return 0

    lax.fori_loop(0, nc // M_CHUNKS_PER_ITER, local_trip, 0)

    c_ref[...] = jnp.zeros(c_ref.shape, F32)

    def scan(j, carry):
        out = []
        for d in range(N_DIR):
            m_st, n_st = carry[d]
            idx = d * nc + (j if d == 0 else nc - 1 - j)
            c_st = c_ref[d]
            cs_ref[idx] = c_st.astype(BF16)
            ns_ref[idx] = n_st
            ms_ref[idx] = m_st
            bl = bl_ref[idx]
            ml = ml_ref[idx]
            m_new = jnp.maximum(bl + m_st, ml)
            decay = jnp.exp(bl + m_st - m_new)
            gain = jnp.exp(ml - m_new)
            c_ref[d] = decay[0:1, 0:1] * c_st + gain[0:1, 0:1] * cu_ref[idx]
            out.append((m_new, decay * n_st + gain * nu_ref[idx]))
        return tuple(out)

    m0 = jnp.full((8, LANES), -jnp.inf, F32)
    n0 = jnp.zeros((8, M_DK), F32)
    lax.fori_loop(0, nc, scan, ((m0, n0), (m0, n0)))

    def wide(x):
        return jnp.concatenate([x] * (M_DV // LANES), axis=1)

    def combine(c, _):
        r0 = pl.multiple_of(c * L, L)
        q = q_ref[0, pl.ds(r0, L), :]
        qb = q.astype(BF16)
        gcol = gc_ref[0, pl.ds(r0, L), :]
        hh = None
        for d in range(N_DIR):
            idx = d * nc + c
            bc = jnp.broadcast_to(_lane_pick(gcol, CH_F + d * M_HEADS + head), (L, LANES))
            den_i = den_ref[d, pl.ds(r0, L), :]
            m_i = mi_ref[d, pl.ds(r0, L), :]
            log_inter = bc + ms_ref[idx][0:1, :]
            m_row = jnp.maximum(log_inter, m_i)
            w_i = jnp.exp(m_i - m_row)
            w_x = jnp.exp(log_inter - m_row)
            q_c = jnp.dot(qb, cs_ref[idx], preferred_element_type=F32)
            q_n = jnp.broadcast_to(jnp.sum(q * ns_ref[idx][0:1, :], axis=1, keepdims=True), (L, LANES))
            num = wide(w_i) * hn_ref[d, pl.ds(r0, L), :] + wide(w_x) * q_c
            den = w_i * den_i + w_x * q_n
            h_d = num * wide(1.0 / jnp.maximum(jnp.abs(den), jnp.exp(-m_row)))
            hh = h_d if hh is None else hh + h_d
        hh = hh * lax.rsqrt(jnp.mean(hh * hh, axis=-1, keepdims=True) + RMS_EPS)
        hh = hh * ng_ref[0]
        y_ref[0, pl.ds(r0, L), :] = (_sigmoid(o_ref[0, pl.ds(r0, L), :]) * hh).astype(y_ref.dtype)
        return 0

    lax.fori_loop(0, nc, combine, 0, unroll=2)


def _mlstm(proj3, gcol, grow, norm_g, off_q, off_k, off_v, off_o, cast_weights):
    b, s, _ = proj3.shape
    w_specs, w_shapes = _cast_specs(cast_weights, b * M_HEADS, lambda i, h: i * M_HEADS + h)
    assert s % (M_CHUNK * M_CHUNKS_PER_ITER) == 0
    nc = s // M_CHUNK
    bq, bk = off_q // M_DK, off_k // M_DK
    bv, bo = off_v // M_DV, off_o // M_DV
    return pl.pallas_call(
        functools.partial(_mlstm_kernel, len(cast_weights)),
        grid=(b, M_HEADS),
        in_specs=[
            pl.BlockSpec((1, s, M_DK), lambda i, h: (i, 0, bq + h)),
            pl.BlockSpec((1, s, M_DK), lambda i, h: (i, 0, bk + h)),
            pl.BlockSpec((1, s, M_DV), lambda i, h: (i, 0, bv + h)),
            pl.BlockSpec((1, s, M_DV), lambda i, h: (i, 0, bo + h)),
            pl.BlockSpec((1, s, LANES), lambda i, h: (i, 0, 0)),
            pl.BlockSpec((1, LANES, s), lambda i, h: (i, 0, 0)),
            pl.BlockSpec((1, 1, M_DV), lambda i, h: (h, 0, 0)),
            *w_specs,
        ],
        out_specs=[pl.BlockSpec((1, s, M_DV), lambda i, h: (i, 0, h)), *w_specs],
        out_shape=[jax.ShapeDtypeStruct((b, s, M_HEADS * M_DV), BF16), *w_shapes],
        scratch_shapes=[
            pltpu.VMEM((N_DIR, s, M_DV), F32),
            pltpu.VMEM((N_DIR, s, LANES), F32),
            pltpu.VMEM((N_DIR, s, LANES), F32),
            pltpu.VMEM((N_DIR * nc, M_DK, M_DV), F32),
            pltpu.VMEM((N_DIR * nc, 8, M_DK), F32),
            pltpu.VMEM((N_DIR * nc, 8, LANES), F32),
            pltpu.VMEM((N_DIR * nc, 8, LANES), F32),
            pltpu.VMEM((N_DIR * nc, M_DK, M_DV), BF16),
            pltpu.VMEM((N_DIR * nc, 8, M_DK), F32),
            pltpu.VMEM((N_DIR * nc, 8, LANES), F32),
            pltpu.VMEM((N_DIR, M_DK, M_DV), F32),
        ],
        compiler_params=_cparams(("parallel", "arbitrary")),
        name="mlstm",
    )(proj3, proj3, proj3, proj3, gcol, grow, norm_g, *cast_weights)


def _delta_kernel(n_cast, q_ref, k_ref, v_ref, z_ref, cwq_ref, cwk_ref, cwv_ref, gc_ref, gr_ref, ng_ref, *refs):
    w_refs, y_ref, wo_refs = refs[:n_cast], refs[n_cast], refs[n_cast + 1:2 * n_cast + 1]
    xp_ref, qs_ref, ks_ref, vs_ref, kq_ref, n_ref, dec_ref, o_ref, st_ref = refs[2 * n_cast + 1:]
    _cast_blocks(w_refs, wo_refs)
    t = pl.program_id(0)
    head = jnp.minimum(t, pl.num_programs(0) - 2) % D_HEADS
    s = q_ref.shape[1]
    L = D_CHUNK
    G = D_GROUP
    nc = s // L
    ng = s // G
    halo = CONV_HALO
    pad = CONV_W // 2

    @pl.when(t == 0)
    def _():
        for r in (kq_ref, n_ref, dec_ref, o_ref):
            r[...] = jnp.zeros(r.shape, r.dtype)

    xp_ref[0:halo, :] = jnp.zeros((halo, D_DIM), F32)
    xp_ref[s + halo:s + 2 * halo, :] = jnp.zeros((halo, D_DIM), F32)

    def conv_into(x_ref, cw_ref, dst_ref, l2, mult):
        xp_ref[halo:s + halo, :] = x_ref[0]
        for blk in range(ng):
            base = halo + blk * G - pad
            acc = xp_ref[base:base + G, :] * cw_ref[0:1, :]
            for j in range(1, CONV_W):
                acc = acc + xp_ref[base + j:base + j + G, :] * cw_ref[j:j + 1, :]
            y = acc * _sigmoid(acc)
            if l2:
                y = y * (lax.rsqrt(jnp.sum(y * y, axis=-1, keepdims=True) + L2_EPS) * mult)
            dst_ref[blk * G:(blk + 1) * G, :] = y

    conv_into(q_ref, cwq_ref, qs_ref, True, D_DIM ** -0.5)
    conv_into(k_ref, cwk_ref, ks_ref, True, 1.0)
    conv_into(v_ref, cwv_ref, vs_ref, False, 1.0)

    st_ref[...] = jnp.zeros(st_ref.shape, F32)

    def scan_step(scan, j):
        for d in range(N_DIR):
            c = j if d == 0 else nc - 1 - j
            r0 = pl.multiple_of(c * L, L)
            idx = (scan + d) * nc + c
            st = st_ref[d]
            res = jnp.dot(kq_ref[idx], st.astype(BF16), preferred_element_type=F32)
            st_ref[d] = dec_ref[idx][0:1, :] * st + res[0:D_DIM] + n_ref[idx]
            o_ref[scan + d, pl.ds(r0, L), :] = o_ref[scan + d, pl.ds(r0, L), :] + res[D_DIM:D_DIM + L]

    tchunk = lax.broadcasted_iota(I32, (D_DIM, D_GROUP), 1) // D_CHUNK
    rr = lax.broadcasted_iota(I32, (G, G), 0)
    cc = lax.broadcasted_iota(I32, (G, G), 1)
    same = (rr // L) == (cc // L)
    pr = lax.broadcasted_iota(I32, (L, G), 0)
    pc = lax.broadcasted_iota(I32, (L, G), 1)
    eye_p = jnp.where((pc % L) == pr, 1.0, 0.0).astype(F32)

    def pack(m):
        out = m[0:L]
        for i in range(1, D_PER_GROUP):
            out = out + m[i * L:(i + 1) * L]
        return out

    def unpack(p):
        return jnp.where(same, jnp.concatenate([p] * D_PER_GROUP, axis=0), 0.0)

    scan_per_trip = nc // (ng // D_GROUPS_PER_ITER)

    def groups(fill, scan, trip):
        pending = iter(range(scan_per_trip))

        def scan_some(n):
            for _ in range(n):
                j = next(pending, None)
                if j is not None:
                    scan_step(scan, trip * scan_per_trip + j)

        chains = []
        for gi in [D_GROUPS_PER_ITER * trip + g for g in range(D_GROUPS_PER_ITER)]:
            r0 = pl.multiple_of(gi * G, G)
            kk_ = ks_ref[pl.ds(r0, G), :]
            qq_ = qs_ref[pl.ds(r0, G), :]
            vv_ = vs_ref[pl.ds(r0, G), :]
            kb = kk_.astype(BF16)
            kk = lax.dot_general(kb, kb, _NT, preferred_element_type=F32)
            qk = lax.dot_general(qq_.astype(BF16), kb, _NT, preferred_element_type=F32)
            gcol = gc_ref[0, pl.ds(r0, G), :]
            for d in range(N_DIR):
                ch = d * D_HEADS + head
                g_col = _lane_pick(gcol, CH_G + ch)
                b_col = _lane_pick(gcol, CH_B + ch)
                t_col = _lane_pick(gcol, CH_T + ch)
                g_row = gr_ref[0, pl.ds(CH_G + ch, 1), pl.ds(r0, G)]
                tri = (cc <= rr) if d == 0 else (cc >= rr)
                strict = (cc < rr) if d == 0 else (cc > rr)
                gam = jnp.exp(jnp.where(tri, jnp.where(same, g_col - g_row, -jnp.inf), -jnp.inf))
                x = jnp.where(strict, -(b_col * kk * gam), 0.0)
                eg = jnp.exp(g_col)
                chains.append(dict(
                    gi=gi, slot=fill + d, r0=r0, x=x, xp=pack(x), attn=(qk * gam).astype(BF16),
                    rhs=jnp.concatenate([b_col * vv_, (b_col * eg) * kk_], axis=1).astype(BF16),
                    qg=qq_ * eg, kdt=(kk_ * jnp.exp(t_col - g_col)).T, dec=jnp.exp(t_col)))
            scan_some(-(-(scan_per_trip - D_SCAN_STEPS_BETWEEN_STAGES) // D_GROUPS_PER_ITER))
        for c in chains:
            c["tp"] = eye_p + c["xp"]
            c["p"] = _bdot(c["xp"], c["x"])
        scan_some(1)
        for _it in range(4):
            for c in chains:
                res = _bdot(jnp.concatenate([c["tp"], c["p"]], axis=0), unpack(c["p"]))
                c["tp"] = c["tp"] + res[0:L]
                c["p"] = res[L:2 * L]
            scan_some(1)
        for c in chains:
            c["tp"] = c["tp"] + _bdot(c["tp"], unpack(c["p"]))
        scan_some(1)
        for c in chains:
            c["uw"] = _bdot(unpack(c["tp"]), c["rhs"]).astype(BF16)
        scan_some(1)
        for c in chains:
            au = jnp.dot(c["attn"], c["uw"], preferred_element_type=F32)
            o_ref[c["slot"], pl.ds(c["r0"], G), :] = au[:, 0:D_DIM]
            c["qe"] = (c["qg"] - au[:, D_DIM:2 * D_DIM]).astype(BF16)
        scan_some(1)
        for c in chains:
            for ci in range(D_PER_GROUP):
                idx = c["slot"] * nc + c["gi"] * D_PER_GROUP + ci
                ku = _bdot(jnp.where(tchunk == ci, c["kdt"], 0.0), c["uw"])
                n_ref[idx] = ku[:, 0:D_DIM]
                kq_ref[idx, 0:D_DIM, :] = (-ku[:, D_DIM:2 * D_DIM]).astype(BF16)
                kq_ref[idx, D_DIM:D_DIM + L, :] = c["qe"][ci * L:(ci + 1) * L]
                dec_ref[idx] = jnp.broadcast_to(c["dec"][ci * L:ci * L + 8], (8, D_DIM))
        scan_some(scan_per_trip)

    def finish(scan, gi):
        r0 = pl.multiple_of(gi * G, G)
        o = o_ref[scan, pl.ds(r0, G), :] + o_ref[scan + 1, pl.ds(r0, G), :]
        o = o * lax.rsqrt(jnp.mean(o * o, axis=-1, keepdims=True) + RMS_EPS) * ng_ref[...]
        z = z_ref[0, pl.ds(r0, G), :]
        y_ref[0, pl.ds(r0, G), :] = (o * (z * _sigmoid(z))).astype(y_ref.dtype)

    fill = (t % 2) * N_DIR
    scan = N_DIR - fill

    def group_batch(i, carry):
        groups(fill, scan, i)
        return carry

    def finish_batch(gi, carry):
        finish(scan, gi)
        return carry

    lax.fori_loop(0, ng // D_GROUPS_PER_ITER, group_batch, 0)
    lax.fori_loop(0, ng, finish_batch, 0, unroll=4)


def _delta(proj3, conv_w, gcol, grow, norm_g, off_q, off_k, off_v, off_z, cast_weights):
    b, s, _ = proj3.shape
    w = D_HEADS * D_DIM
    nc = s // D_CHUNK
    trips = s // (D_GROUP * D_GROUPS_PER_ITER)
    assert s % (D_GROUP * D_GROUPS_PER_ITER) == 0 and nc % trips == 0
    bq, bk, bv, bz = (o // D_DIM for o in (off_q, off_k, off_v, off_z))
    n_units = b * D_HEADS
    cur = lambda t: jnp.minimum(t, n_units - 1)
    prev = lambda t: jnp.maximum(t - 1, 0)
    col = lambda unit, blk: lambda t: (unit(t) // D_HEADS, 0, blk + unit(t) % D_HEADS)
    w_specs, w_shapes = _cast_specs(cast_weights, n_units, cur)
    return pl.pallas_call(
        functools.partial(_delta_kernel, len(cast_weights)),
        grid=(n_units + 1,),
        in_specs=[
            pl.BlockSpec((1, s, D_DIM), col(cur, bq)),
            pl.BlockSpec((1, s, D_DIM), col(cur, bk)),
            pl.BlockSpec((1, s, D_DIM), col(cur, bv)),
            pl.BlockSpec((1, s, D_DIM), col(prev, bz)),
            pl.BlockSpec((CONV_W, D_DIM), lambda t: (0, cur(t) % D_HEADS)),
            pl.BlockSpec((CONV_W, D_DIM), lambda t: (0, D_HEADS + cur(t) % D_HEADS)),
            pl.BlockSpec((CONV_W, D_DIM), lambda t: (0, 2 * D_HEADS + cur(t) % D_HEADS)),
            pl.BlockSpec((1, s, LANES), lambda t: (cur(t) // D_HEADS, 0, 0)),
            pl.BlockSpec((1, LANES, s), lambda t: (cur(t) // D_HEADS, 0, 0)),
            pl.BlockSpec((1, D_DIM), lambda t: (0, 0)),
            *w_specs,
        ],
        out_specs=[pl.BlockSpec((1, s, D_DIM), col(prev, 0)), *w_specs],
        out_shape=[jax.ShapeDtypeStruct((b, s, w), BF16), *w_shapes],
        scratch_shapes=[
            pltpu.VMEM((s + 2 * CONV_HALO, D_DIM), F32),
            pltpu.VMEM((s, D_DIM), F32),
            pltpu.VMEM((s, D_DIM), F32),
            pltpu.VMEM((s, D_DIM), F32),
            pltpu.VMEM((2 * N_DIR * nc, D_DIM + D_CHUNK, D_DIM), BF16),
            pltpu.VMEM((2 * N_DIR * nc, D_DIM, D_DIM), F32),
            pltpu.VMEM((2 * N_DIR * nc, 8, D_DIM), F32),
            pltpu.VMEM((2 * N_DIR, s, D_DIM), F32),
            pltpu.VMEM((N_DIR, D_DIM, D_DIM), F32),
        ],
        compiler_params=_cparams(("arbitrary",)),
        name="delta",
    )(proj3, proj3, proj3, proj3, conv_w, conv_w, conv_w, gcol, grow, norm_g, *cast_weights)


def _mix_kernel(n_gate_blocks, x_ref, ym_ref, yd_ref, *refs):
    gate_refs, (wbm_ref, wbd_ref, wo_ref, g2_ref, o_ref, h_ref) = refs[:2 * n_gate_blocks], refs[2 * n_gate_blocks:]
    gate = lambda rs: _sigmoid(jnp.concatenate([r[...] for r in rs], axis=1))
    a = jnp.dot(ym_ref[...], wbm_ref[...], preferred_element_type=F32)
    b = jnp.dot(yd_ref[...], wbd_ref[...], preferred_element_type=F32)
    mixed = gate(gate_refs[:n_gate_blocks]) * a + gate(gate_refs[n_gate_blocks:]) * b
    x = x_ref[...] + jnp.dot(mixed.astype(BF16), wo_ref[...], preferred_element_type=F32)
    o_ref[...] = x
    ms = jnp.mean(x * x, axis=-1, keepdims=True)
    h_ref[...] = (x * lax.rsqrt(ms + RMS_EPS) * g2_ref[...]).astype(BF16)


def _mix(x2, ym2, yd2, proj2, off_gates, wbm, wbd, wo, g2, tm):
    m, d = x2.shape
    const = lambda i: (0, 0)
    gw = math.gcd(off_gates, d)
    gate_specs = [pl.BlockSpec((tm, gw), functools.partial(lambda i, c: (i, c), c=(off_gates + t * gw) // gw))
                  for t in range(2 * d // gw)]
    return pl.pallas_call(
        functools.partial(_mix_kernel, d // gw),
        grid=(m // tm,),
        in_specs=[
            pl.BlockSpec((tm, d), lambda i: (i, 0)),
            pl.BlockSpec((tm, ym2.shape[1]), lambda i: (i, 0)),
            pl.BlockSpec((tm, yd2.shape[1]), lambda i: (i, 0)),
            *gate_specs,
            pl.BlockSpec(wbm.shape, const),
            pl.BlockSpec(wbd.shape, const),
            pl.BlockSpec(wo.shape, const),
            pl.BlockSpec((1, d), const),
        ],
        out_specs=[pl.BlockSpec((tm, d), lambda i: (i, 0)), pl.BlockSpec((tm, d), lambda i: (i, 0))],
        out_shape=[jax.ShapeDtypeStruct((m, d), F32), jax.ShapeDtypeStruct((m, d), BF16)],
        compiler_params=_cparams(("parallel",)),
        name="mix",
    )(x2, ym2, yd2, *([proj2] * len(gate_specs)), wbm, wbd, wo, g2)


def _ffn_kernel(x_ref, h_ref, gf_ref, w1_ref, w2_ref, o_ref, acc_ref):
    f = pl.program_id(1)

    @pl.when(f == 0)
    def _():
        acc_ref[...] = jnp.zeros(acc_ref.shape, F32)

    hid = jnp.dot(h_ref[...], w1_ref[...], preferred_element_type=F32)
    act = jnp.square(jnp.maximum(hid, 0.0)).astype(BF16)
    acc_ref[...] += jnp.dot(act, w2_ref[...], preferred_element_type=F32)

    @pl.when(f == pl.num_programs(1) - 1)
    def _():
        x = x_ref[...] + acc_ref[...]
        ms = jnp.mean(x * x, axis=-1, keepdims=True)
        o_ref[...] = x * lax.rsqrt(ms + RMS_EPS) * gf_ref[...]


def _ffn(x2, h2, gf, w1, w2, tm, tf):
    m, d = x2.shape
    dff = w1.shape[1]
    return pl.pallas_call(
        _ffn_kernel,
        grid=(m // tm, dff // tf),
        in_specs=[
            pl.BlockSpec((tm, d), lambda i, f: (i, 0)),
            pl.BlockSpec((tm, d), lambda i, f: (i, 0)),
            pl.BlockSpec((1, d), lambda i, f: (0, 0)),
            pl.BlockSpec((d, tf), lambda i, f: (0, f)),
            pl.BlockSpec((tf, d), lambda i, f: (f, 0)),
        ],
        out_specs=pl.BlockSpec((tm, d), lambda i, f: (i, 0)),
        out_shape=jax.ShapeDtypeStruct((m, d), F32),
        scratch_shapes=[pltpu.VMEM((tm, d), F32)],
        compiler_params=_cparams(("parallel", "arbitrary")),
        name="ffn",
    )(x2, h2, gf, w1, w2)


def _pick_tile(n, candidates):
    for c in candidates:
        if n % c == 0:
            return c
    raise ValueError(f"no tile in {candidates} divides {n}")


def _layer(x, norm1_g, w_in, i_bias, f_bias, m_norm_g, conv_w, a_log, dt_bias, d_norm_g,
           w_bm, w_bd, w_out, norm2_g, w_ff1, w_ff2, out_g):
    b, s, d = x.shape
    m = b * s
    mw, mqk, dw = M_HEADS * M_DV, M_HEADS * M_DK, D_HEADS * D_DIM
    nm, nd = N_DIR * M_HEADS, N_DIR * D_HEADS
    splits = (mqk, mqk, mw, mw, nm, nm, 3 * dw, dw, nd, nd, 2 * d)
    bounds = [0]
    for w_ in splits:
        bounds.append(bounds[-1] + w_)
    wt = w_in.T.astype(BF16)
    rows = lambda first, last: wt[bounds[first]:bounds[last + 1]]
    seg_bounds = [(bounds[0], bounds[4]), (bounds[6], bounds[8]), (bounds[10], bounds[11])]
    n_small = 2 * nm + 2 * nd
    wt_small = jnp.concatenate([rows(4, 5), rows(8, 9), jnp.zeros((LANES - n_small, d), BF16)], axis=0)
    zpad = lambda n: jnp.zeros((n,), F32)
    bias_row = jnp.concatenate([i_bias.reshape(-1), f_bias.reshape(-1), dt_bias.reshape(-1), zpad(LANES - CH_B)])
    alog_row = jnp.concatenate([zpad(CH_G), a_log.reshape(-1), zpad(LANES - CH_B)])
    gate_params = jnp.concatenate([bias_row[None], alog_row[None], jnp.zeros((6, LANES), F32)], axis=0)

    x2 = x.reshape(m, d)
    tm = _pick_tile(m, (512, 256, 128))
    tn = next(c for c in (1024, 512, 256) if all((hi - lo) % c == 0 for lo, hi in seg_bounds))
    proj2, smalls2 = _inproj(x2, norm1_g.reshape(1, d), wt, wt_small, seg_bounds,
                             _pick_tile(m, (1024, 512, 256)), tn)
    proj3 = proj2.reshape(b, s, -1)
    gcol, grow = _gate_prep(smalls2.reshape(b, s, LANES), gate_params)

    off_d = 2 * mqk + 2 * mw
    off_g = off_d + 4 * dw
    ym, w_bm16, w_bd16, w_out16 = _mlstm(proj3, gcol, grow, m_norm_g.reshape(M_HEADS, 1, M_DV),
                                         0, mqk, 2 * mqk, 2 * mqk + mw, (w_bm, w_bd, w_out))
    yd, w_ff1_16, w_ff2_16 = _delta(proj3, conv_w, gcol, grow, d_norm_g.reshape(1, D_DIM),
                                    off_d, off_d + dw, off_d + 2 * dw, off_d + 3 * dw, (w_ff1, w_ff2))

    x1, h2 = _mix(x2, ym.reshape(m, mw), yd.reshape(m, dw), proj2, off_g, w_bm16, w_bd16, w_out16,
                  norm2_g.reshape(1, d), _pick_tile(m, (256, 128)))
    tf = _pick_tile(w_ff1.shape[1], (1024, 512, 256))
    out = _ffn(x1, h2, out_g.reshape(1, d), w_ff1_16, w_ff2_16, tm, tf)
    return out.reshape(b, s, d)


def kernel(x, norm1_g, w_in, mlstm_i_bias, mlstm_f_bias, mlstm_norm_g, delta_conv_w, delta_a_log,
           delta_dt_bias, delta_norm_g, w_branch_m, w_branch_d, w_out, norm2_g, w_ff1, w_ff2, norm_f_g):
    depth = w_in.shape[0]
    assert depth == 1, "the fused FFN + final-norm epilogue assumes a single layer"
    return _layer(x, norm1_g[0], w_in[0], mlstm_i_bias[0], mlstm_f_bias[0], mlstm_norm_g[0], delta_conv_w[0],
                  delta_a_log[0], delta_dt_bias[0], delta_norm_g[0], w_branch_m[0], w_branch_d[0], w_out[0],
                  norm2_g[0], w_ff1[0], w_ff2[0], norm_f_g)
```

```python
import functools
import math

import jax
import jax.numpy as jnp
import numpy as np
from jax import lax
from jax.experimental import pallas as pl
from jax.experimental.pallas import tpu as pltpu

F32 = jnp.float32
BF16 = jnp.bfloat16
I32 = jnp.int32

N_DIR = 2
M_HEADS = 4
M_DK = 128
M_DV = 256
D_HEADS = 8
D_DIM = 128
CONV_W = 5
RMS_EPS = 1e-6
L2_EPS = 1e-6

LANES = 128
BF16_SUBLANES = 16
M_CHUNK = 256
M_CHUNKS_PER_ITER = 4
D_CHUNK = 64
D_GROUP = 256
D_PER_GROUP = D_GROUP // D_CHUNK
D_GROUPS_PER_ITER = 4
D_SCAN_STEPS_BETWEEN_STAGES = 8
GATE_ROWS = 256
GATE_BLOCKS_PER_STEP = 4
EDGE_STEP_PIECES = 4
CONV_HALO = 8

CH_I = 0
CH_F = 8
CH_G = 16
CH_B = 32
CH_T = 48

VMEM_LIMIT = 56 * 1024 * 1024

_NT = (((1,), (1,)), ((), ()))


def _cparams(sem):
    return pltpu.CompilerParams(dimension_semantics=sem, vmem_limit_bytes=VMEM_LIMIT)


def _bdot(a, b):
    return jnp.dot(a.astype(BF16), b.astype(BF16), preferred_element_type=F32)


def _bdot_nt(a, b):
    return lax.dot_general(a.astype(BF16), b.astype(BF16), _NT, preferred_element_type=F32)


def _sigmoid(x):
    return 1.0 / (1.0 + jnp.exp(-x))


def _softplus(x):
    return jnp.maximum(x, 0.0) + jnp.log1p(jnp.exp(-jnp.abs(x)))


def _cast_blocks(w_refs, o_refs):
    for w_ref, o_ref in zip(w_refs, o_refs):
        o_ref[...] = w_ref[...].astype(o_ref.dtype)


def _cast_specs(weights, n_steps, step_of):
    specs, shapes = [], []
    for w in weights:
        rows, cols = w.shape
        assert rows % (n_steps * BF16_SUBLANES) == 0
        specs.append(pl.BlockSpec((rows // n_steps, cols), lambda *g: (step_of(*g), 0)))
        shapes.append(jax.ShapeDtypeStruct((rows, cols), BF16))
    return specs, shapes


def _lane_pick(x, ch):
    lane = lax.broadcasted_iota(I32, x.shape, 1)
    return jnp.sum(jnp.where(lane == ch, x, 0.0), axis=1, keepdims=True)


def _inproj_kernel(x_ref, g_ref, w_ref, ws_ref, o_ref, os_ref, hn_ref):
    j = pl.program_id(1)

    @pl.when(j == 0)
    def _():
        rows = x_ref.shape[0] // EDGE_STEP_PIECES
        for r in range(EDGE_STEP_PIECES):
            sl = slice(r * rows, (r + 1) * rows)
            x = x_ref[sl, :]
            ms = jnp.mean(x * x, axis=-1, keepdims=True)
            hn = (x * lax.rsqrt(ms + RMS_EPS) * g_ref[...]).astype(BF16)
            hn_ref[sl, :] = hn
            os_ref[sl, :] = lax.dot_general(hn, ws_ref[...], _NT, preferred_element_type=F32)
            o_ref[sl, :] = lax.dot_general(hn, w_ref[...], _NT, preferred_element_type=F32)

    @pl.when(j > 0)
    def _():
        o_ref[...] = lax.dot_general(hn_ref[...], w_ref[...], _NT, preferred_element_type=F32)


def _inproj(x2, g, wt, wt_small, seg_bounds, tm, tn):
    m, d = x2.shape
    starts = [0]
    for lo, hi in seg_bounds:
        starts.append(starts[-1] + (hi - lo) // tn)

    def w_row(i, j):
        row = seg_bounds[0][0] + j * tn
        for k in range(1, len(seg_bounds)):
            row = jnp.where(j >= starts[k], seg_bounds[k][0] + (j - starts[k]) * tn, row)
        return pl.multiple_of(row, BF16_SUBLANES), 0

    return pl.pallas_call(
        _inproj_kernel,
        grid=(m // tm, starts[-1]),
        in_specs=[
            pl.BlockSpec((tm, d), lambda i, j: (i, 0)),
            pl.BlockSpec((1, d), lambda i, j: (0, 0)),
            pl.BlockSpec((pl.Element(tn), pl.Element(d)), w_row),
            pl.BlockSpec((LANES, d), lambda i, j: (0, 0)),
        ],
        out_specs=[
            pl.BlockSpec((tm, tn), lambda i, j: (i, j)),
            pl.BlockSpec((tm, LANES), lambda i, j: (i, 0)),
        ],
        out_shape=[jax.ShapeDtypeStruct((m, starts[-1] * tn), F32), jax.ShapeDtypeStruct((m, LANES), F32)],
        scratch_shapes=[pltpu.VMEM((tm, d), BF16)],
        compiler_params=_cparams(("parallel", "arbitrary")),
        name="inproj",
    )(x2, g, wt, wt_small)


def _cumsum_matrices():
    rr, cc = np.indices((GATE_ROWS, GATE_ROWS))
    low, upp = cc <= rr, cc >= rr
    same_d = (rr // D_CHUNK) == (cc // D_CHUNK)
    same_m = (rr // M_CHUNK) == (cc // M_CHUNK)
    mats = np.concatenate([
        low & same_m,
        upp & same_m,
        low & same_d,
        upp & same_d,
        same_d,
    ], axis=0)
    return jnp.asarray(mats.astype(np.float32), dtype=BF16)


def _gate_kernel(sm_ref, par_ref, mats_ref, col_ref, row_ref):
    r = GATE_ROWS
    lane = lax.broadcasted_iota(I32, (r, LANES), 1)
    neg_a = -jnp.exp(par_ref[1:2, :])
    xes, x3s = [], []
    for blk in range(GATE_BLOCKS_PER_STEP):
        x = sm_ref[0, blk * r:(blk + 1) * r, :] + par_ref[0:1, :]
        logf = -_softplus(-x)
        g = neg_a * _softplus(x)
        beta = _sigmoid(x)
        xe = jnp.where(lane < CH_F, x,
                       jnp.where(lane < CH_G, logf,
                                 jnp.where(lane < CH_B, g,
                                           jnp.where(lane < CH_T, beta, 0.0))))
        hi = xe.astype(BF16)
        r1 = xe - hi.astype(F32)
        mid = r1.astype(BF16)
        lo = (r1 - mid.astype(F32)).astype(BF16)
        xes.append(xe)
        x3s.append(jnp.concatenate([hi, mid, lo], axis=1))
    y3s = [jnp.dot(mats_ref[...], x3, preferred_element_type=F32) for x3 in x3s]
    half_m = CH_F + M_HEADS
    half_d = CH_G + D_HEADS
    for blk, (xe, y3) in enumerate(zip(xes, y3s)):
        y = y3[:, 0:LANES] + y3[:, LANES:2 * LANES] + y3[:, 2 * LANES:3 * LANES]
        pm, sm_, pd, sd, td = (y[i * r:(i + 1) * r] for i in range(5))
        out = jnp.where(lane < CH_F, xe,
              jnp.where(lane < half_m, pm,
              jnp.where(lane < CH_G, sm_,
              jnp.where(lane < half_d, pd,
              jnp.where(lane < CH_B, sd,
              jnp.where(lane < CH_T, xe,
              jnp.where(lane < CH_T + N_DIR * D_HEADS, pltpu.roll(td, CH_T - CH_G, axis=1), 0.0)))))))
        col_ref[0, blk * r:(blk + 1) * r, :] = out
        row_ref[0, :, blk * r:(blk + 1) * r] = out.T


def _gate_prep(smalls3, params):
    b, s, _ = smalls3.shape
    r = GATE_ROWS * GATE_BLOCKS_PER_STEP
    assert s % r == 0
    return pl.pallas_call(
        _gate_kernel,
        grid=(b, s // r),
        in_specs=[
            pl.BlockSpec((1, r, LANES), lambda i, j: (i, j, 0)),
            pl.BlockSpec((8, LANES), lambda i, j: (0, 0)),
            pl.BlockSpec((5 * GATE_ROWS, GATE_ROWS), lambda i, j: (0, 0)),
        ],
        out_specs=[
            pl.BlockSpec((1, r, LANES), lambda i, j: (i, j, 0)),
            pl.BlockSpec((1, LANES, r), lambda i, j: (i, 0, j)),
        ],
        out_shape=[jax.ShapeDtypeStruct((b, s, LANES), F32), jax.ShapeDtypeStruct((b, LANES, s), F32)],
        compiler_params=_cparams(("parallel", "parallel")),
        name="gateprep",
    )(smalls3, params, _cumsum_matrices())


def _mlstm_kernel(n_cast, q_ref, k_ref, v_ref, o_ref, gc_ref, gr_ref, ng_ref, *refs):
    w_refs, y_ref, wo_refs = refs[:n_cast], refs[n_cast], refs[n_cast + 1:2 * n_cast + 1]
    hn_ref, den_ref, mi_ref, cu_ref, nu_ref, bl_ref, ml_ref, cs_ref, ns_ref, ms_ref, c_ref = refs[2 * n_cast + 1:]
    _cast_blocks(w_refs, wo_refs)
    head = pl.program_id(1)
    s = q_ref.shape[1]
    L = M_CHUNK
    nc = s // L
    scale = M_DK ** -0.5
    rr = lax.broadcasted_iota(I32, (L, L), 0)
    cc = lax.broadcasted_iota(I32, (L, L), 1)

    def bcast8(x11):
        return jnp.broadcast_to(x11, (8, LANES))

    def local(cs):
        chains = []
        for c in cs:
            r0 = pl.multiple_of(c * L, L)
            q = q_ref[0, pl.ds(r0, L), :]
            k = k_ref[0, pl.ds(r0, L), :] * scale
            vb = v_ref[0, pl.ds(r0, L), :].astype(BF16)
            kb = k.astype(BF16)
            qk = lax.dot_general(q.astype(BF16), kb, _NT, preferred_element_type=F32)
            k_t = k.T
            gcol = gc_ref[0, pl.ds(r0, L), :]
            for d in range(N_DIR):
                ch_i = CH_I + d * M_HEADS + head
                ch_f = CH_F + d * M_HEADS + head
                bc_col = _lane_pick(gcol, ch_f)
                i_row = gr_ref[0, pl.ds(ch_i, 1), pl.ds(r0, L)]
                bc_row = gr_ref[0, pl.ds(ch_f, 1), pl.ds(r0, L)]
                a_row = i_row - bc_row
                mask = (cc <= rr) if d == 0 else (cc >= rr)
                log_intra = jnp.where(mask, bc_col + a_row, -jnp.inf)
                m_intra = jnp.max(log_intra, axis=1, keepdims=True)
                p = qk * jnp.exp(log_intra - m_intra)
                b_last = bc_row[:, L - 1:L] if d == 0 else bc_row[:, 0:1]
                log_state = b_last + a_row
                m_loc = jnp.max(log_state, axis=1, keepdims=True)
                w_state = jnp.exp(log_state - m_loc)
                chains.append(dict(
                    idx=d * nc + c, d=d, r0=r0, vb=vb, kb=kb, pb=p.astype(BF16),
                    den=jnp.sum(p, axis=1, keepdims=True), m_intra=m_intra,
                    kw=(k_t * w_state).astype(BF16), w8=jnp.broadcast_to(w_state, (8, L)).astype(BF16),
                    b_last=b_last, m_loc=m_loc))
        for c in chains:
            hn_ref[c["d"], pl.ds(c["r0"], L), :] = jnp.dot(c["pb"], c["vb"], preferred_element_type=F32)
            den_ref[c["d"], pl.ds(c["r0"], L), :] = jnp.broadcast_to(c["den"], (L, LANES))
            mi_ref[c["d"], pl.ds(c["r0"], L), :] = jnp.broadcast_to(c["m_intra"], (L, LANES))
        for c in chains:
            cu_ref[c["idx"]] = jnp.dot(c["kw"], c["vb"], preferred_element_type=F32)
            nu_ref[c["idx"]] = jnp.dot(c["w8"], c["kb"], preferred_element_type=F32)
            bl_ref[c["idx"]] = bcast8(c["b_last"])
            ml_ref[c["idx"]] = bcast8(c["m_loc"])

    def local_trip(i, _):
        local([M_CHUNKS_PER_ITER * i + t for t in range(M_CHUNKS_PER_ITER)])
        return 0

    lax.fori_loop(0, nc // M_CHUNKS_PER_ITER, local_trip, 0)

    c_ref[...] = jnp.zeros(c_ref.shape, F32)

    def scan(j, carry):
        out = []
        for d in range(N_DIR):
            m_st, n_st = carry[d]
            idx = d * nc + (j if d == 0 else nc - 1 - j)
            c_st = c_ref[d]
            cs_ref[idx] = c_st.astype(BF16)
            ns_ref[idx] = n_st
            ms_ref[idx] = m_st
            bl = bl_ref[idx]
            ml = ml_ref[idx]
            m_new = jnp.maximum(bl + m_st, ml)
            decay = jnp.exp(bl + m_st - m_new)
            gain = jnp.exp(ml - m_new)
            c_ref[d] = decay[0:1, 0:1] * c_st + gain[0:1, 0:1] * cu_ref[idx]
            out.append((m_new, decay * n_st + gain * nu_ref[idx]))
        return tuple(out)

    m0 = jnp.full((8, LANES), -jnp.inf, F32)
    n0 = jnp.zeros((8, M_DK), F32)
    lax.fori_loop(0, nc, scan, ((m0, n0), (m0, n0)))

    def wide(x):
        return jnp.concatenate([x] * (M_DV // LANES), axis=1)

    def combine(c, _):
        r0 = pl.multiple_of(c * L, L)
        q = q_ref[0, pl.ds(r0, L), :]
        qb = q.astype(BF16)
        gcol = gc_ref[0, pl.ds(r0, L), :]
        hh = None
        for d in range(N_DIR):
            idx = d * nc + c
            bc = jnp.broadcast_to(_lane_pick(gcol, CH_F + d * M_HEADS + head), (L, LANES))
            den_i = den_ref[d, pl.ds(r0, L), :]
            m_i = mi_ref[d, pl.ds(r0, L), :]
            log_inter = bc + ms_ref[idx][0:1, :]
            m_row = jnp.maximum(log_inter, m_i)
            w_i = jnp.exp(m_i - m_row)
            w_x = jnp.exp(log_inter - m_row)
            q_c = jnp.dot(qb, cs_ref[idx], preferred_element_type=F32)
            q_n = jnp.broadcast_to(jnp.sum(q * ns_ref[idx][0:1, :], axis=1, keepdims=True), (L, LANES))
            num = wide(w_i) * hn_ref[d, pl.ds(r0, L), :] + wide(w_x) * q_c
            den = w_i * den_i + w_x * q_n
            h_d = num * wide(1.0 / jnp.maximum(jnp.abs(den), jnp.exp(-m_row)))
            hh = h_d if hh is None else hh + h_d
        hh = hh * lax.rsqrt(jnp.mean(hh * hh, axis=-1, keepdims=True) + RMS_EPS)
        hh = hh * ng_ref[0]
        y_ref[0, pl.ds(r0, L), :] = (_sigmoid(o_ref[0, pl.ds(r0, L), :]) * hh).astype(y_ref.dtype)
        return 0

    lax.fori_loop(0, nc, combine, 0, unroll=2)


def _mlstm(proj3, gcol, grow, norm_g, off_q, off_k, off_v, off_o, cast_weights):
    b, s, _ = proj3.shape
    w_specs, w_shapes = _cast_specs(cast_weights, b * M_HEADS, lambda i, h: i * M_HEADS + h)
    assert s % (M_CHUNK * M_CHUNKS_PER_ITER) == 0
    nc = s // M_CHUNK
    bq, bk = off_q // M_DK, off_k // M_DK
    bv, bo = off_v // M_DV, off_o // M_DV
    return pl.pallas_call(
        functools.partial(_mlstm_kernel, len(cast_weights)),
        grid=(b, M_HEADS),
        in_specs=[
            pl.BlockSpec((1, s, M_DK), lambda i, h: (i, 0, bq + h)),
            pl.BlockSpec((1, s, M_DK), lambda i, h: (i, 0, bk + h)),
            pl.BlockSpec((1, s, M_DV), lambda i, h: (i, 0, bv + h)),
            pl.BlockSpec((1, s, M_DV), lambda i, h: (i, 0, bo + h)),
            pl.BlockSpec((1, s, LANES), lambda i, h: (i, 0, 0)),
            pl.BlockSpec((1, LANES, s), lambda i, h: (i, 0, 0)),
            pl.BlockSpec((1, 1, M_DV), lambda i, h: (h, 0, 0)),
            *w_specs,
        ],
        out_specs=[pl.BlockSpec((1, s, M_DV), lambda i, h: (i, 0, h)), *w_specs],
        out_shape=[jax.ShapeDtypeStruct((b, s, M_HEADS * M_DV), BF16), *w_shapes],
        scratch_shapes=[
            pltpu.VMEM((N_DIR, s, M_DV), F32),
            pltpu.VMEM((N_DIR, s, LANES), F32),
            pltpu.VMEM((N_DIR, s, LANES), F32),
            pltpu.VMEM((N_DIR * nc, M_DK, M_DV), F32),
            pltpu.VMEM((N_DIR * nc, 8, M_DK), F32),
            pltpu.VMEM((N_DIR * nc, 8, LANES), F32),
            pltpu.VMEM((N_DIR * nc, 8, LANES), F32),
            pltpu.VMEM((N_DIR * nc, M_DK, M_DV), BF16),
            pltpu.VMEM((N_DIR * nc, 8, M_DK), F32),
            pltpu.VMEM((N_DIR * nc, 8, LANES), F32),
            pltpu.VMEM((N_DIR, M_DK, M_DV), F32),
        ],
        compiler_params=_cparams(("parallel", "arbitrary")),
        name="mlstm",
    )(proj3, proj3, proj3, proj3, gcol, grow, norm_g, *cast_weights)


def _delta_kernel(n_cast, q_ref, k_ref, v_ref, z_ref, cwq_ref, cwk_ref, cwv_ref, gc_ref, gr_ref, ng_ref, *refs):
    w_refs, y_ref, wo_refs = refs[:n_cast], refs[n_cast], refs[n_cast + 1:2 * n_cast + 1]
    xp_ref, qs_ref, ks_ref, vs_ref, kq_ref, n_ref, dec_ref, o_ref, st_ref = refs[2 * n_cast + 1:]
    _cast_blocks(w_refs, wo_refs)
    t = pl.program_id(0)
    head = jnp.minimum(t, pl.num_programs(0) - 2) % D_HEADS
    s = q_ref.shape[1]
    L = D_CHUNK
    G = D_GROUP
    nc = s // L
    ng = s // G
    halo = CONV_HALO
    pad = CONV_W // 2

    @pl.when(t == 0)
    def _():
        for r in (kq_ref, n_ref, dec_ref, o_ref):
            r[...] = jnp.zeros(r.shape, r.dtype)

    xp_ref[0:halo, :] = jnp.zeros((halo, D_DIM), F32)
    xp_ref[s + halo:s + 2 * halo, :] = jnp.zeros((halo, D_DIM), F32)

    def conv_into(x_ref, cw_ref, dst_ref, l2, mult):
        xp_ref[halo:s + halo, :] = x_ref[0]
        for blk in range(ng):
            base = halo + blk * G - pad
            acc = xp_ref[base:base + G, :] * cw_ref[0:1, :]
            for j in range(1, CONV_W):
                acc = acc + xp_ref[base + j:base + j + G, :] * cw_ref[j:j + 1, :]
            y = acc * _sigmoid(acc)
            if l2:
                y = y * (lax.rsqrt(jnp.sum(y * y, axis=-1, keepdims=True) + L2_EPS) * mult)
            dst_ref[blk * G:(blk + 1) * G, :] = y

    conv_into(q_ref, cwq_ref, qs_ref, True, D_DIM ** -0.5)
    conv_into(k_ref, cwk_ref, ks_ref, True, 1.0)
    conv_into(v_ref, cwv_ref, vs_ref, False, 1.0)

    st_ref[...] = jnp.zeros(st_ref.shape, F32)

    def scan_step(scan, j):
        for d in range(N_DIR):
            c = j if d == 0 else nc - 1 - j
            r0 = pl.multiple_of(c * L, L)
            idx = (scan + d) * nc + c
            st = st_ref[d]
            res = jnp.dot(kq_ref[idx], st.astype(BF16), preferred_element_type=F32)
            st_ref[d] = dec_ref[idx][0:1, :] * st + res[0:D_DIM] + n_ref[idx]
            o_ref[scan + d, pl.ds(r0, L), :] = o_ref[scan + d, pl.ds(r0, L), :] + res[D_DIM:D_DIM + L]

    tchunk = lax.broadcasted_iota(I32, (D_DIM, D_GROUP), 1) // D_CHUNK
    rr = lax.broadcasted_iota(I32, (G, G), 0)
    cc = lax.broadcasted_iota(I32, (G, G), 1)
    same = (rr // L) == (cc // L)
    pr = lax.broadcasted_iota(I32, (L, G), 0)
    pc = lax.broadcasted_iota(I32, (L, G), 1)
    eye_p = jnp.where((pc % L) == pr, 1.0, 0.0).astype(F32)

    def pack(m):
        out = m[0:L]
        for i in range(1, D_PER_GROUP):
            out = out + m[i * L:(i + 1) * L]
        return out

    def unpack(p):
        return jnp.where(same, jnp.concatenate([p] * D_PER_GROUP, axis=0), 0.0)

    scan_per_trip = nc // (ng // D_GROUPS_PER_ITER)

    def groups(fill, scan, trip):
        pending = iter(range(scan_per_trip))

        def scan_some(n):
            for _ in range(n):
                j = next(pending, None)
                if j is not None:
                    scan_step(scan, trip * scan_per_trip + j)

        chains = []
        for gi in [D_GROUPS_PER_ITER * trip + g for g in range(D_GROUPS_PER_ITER)]:
            r0 = pl.multiple_of(gi * G, G)
            kk_ = ks_ref[pl.ds(r0, G), :]
            qq_ = qs_ref[pl.ds(r0, G), :]
            vv_ = vs_ref[pl.ds(r0, G), :]
            kb = kk_.astype(BF16)
            kk = lax.dot_general(kb, kb, _NT, preferred_element_type=F32)
            qk = lax.dot_general(qq_.astype(BF16), kb, _NT, preferred_element_type=F32)
            gcol = gc_ref[0, pl.ds(r0, G), :]
            for d in range(N_DIR):
                ch = d * D_HEADS + head
                g_col = _lane_pick(gcol, CH_G + ch)
                b_col = _lane_pick(gcol, CH_B + ch)
                t_col = _lane_pick(gcol, CH_T + ch)
                g_row = gr_ref[0, pl.ds(CH_G + ch, 1), pl.ds(r0, G)]
                tri = (cc <= rr) if d == 0 else (cc >= rr)
                strict = (cc < rr) if d == 0 else (cc > rr)
                gam = jnp.exp(jnp.where(tri, jnp.where(same, g_col - g_row, -jnp.inf), -jnp.inf))
                x = jnp.where(strict, -(b_col * kk * gam), 0.0)
                eg = jnp.exp(g_col)
                chains.append(dict(
                    gi=gi, slot=fill + d, r0=r0, x=x, xp=pack(x), attn=(qk * gam).astype(BF16),
                    rhs=jnp.concatenate([b_col * vv_, (b_col * eg) * kk_], axis=1).astype(BF16),
                    qg=qq_ * eg, kdt=(kk_ * jnp.exp(t_col - g_col)).T, dec=jnp.exp(t_col)))
            scan_some(-(-(scan_per_trip - D_SCAN_STEPS_BETWEEN_STAGES) // D_GROUPS_PER_ITER))
        for c in chains:
            c["tp"] = eye_p + c["xp"]
            c["p"] = _bdot(c["xp"], c["x"])
        scan_some(1)
        for _it in range(4):
            for c in chains:
                res = _bdot(jnp.concatenate([c["tp"], c["p"]], axis=0), unpack(c["p"]))
                c["tp"] = c["tp"] + res[0:L]
                c["p"] = res[L:2 * L]
            scan_some(1)
        for c in chains:
            c["tp"] = c["tp"] + _bdot(c["tp"], unpack(c["p"]))
        scan_some(1)
        for c in chains:
            c["uw"] = _bdot(unpack(c["tp"]), c["rhs"]).astype(BF16)
        scan_some(1)
        for c in chains:
            au = jnp.dot(c["attn"], c["uw"], preferred_element_type=F32)
            o_ref[c["slot"], pl.ds(c["r0"], G), :] = au[:, 0:D_DIM]
            c["qe"] = (c["qg"] - au[:, D_DIM:2 * D_DIM]).astype(BF16)
        scan_some(1)
        for c in chains:
            for ci in range(D_PER_GROUP):
                idx = c["slot"] * nc + c["gi"] * D_PER_GROUP + ci
                ku = _bdot(jnp.where(tchunk == ci, c["kdt"], 0.0), c["uw"])
                n_ref[idx] = ku[:, 0:D_DIM]
                kq_ref[idx, 0:D_DIM, :] = (-ku[:, D_DIM:2 * D_DIM]).astype(BF16)
                kq_ref[idx, D_DIM:D_DIM + L, :] = c["qe"][ci * L:(ci + 1) * L]
                dec_ref[idx] = jnp.broadcast_to(c["dec"][ci * L:ci * L + 8], (8, D_DIM))
        scan_some(scan_per_trip)

    def finish(scan, gi):
        r0 = pl.multiple_of(gi * G, G)
        o = o_ref[scan, pl.ds(r0, G), :] + o_ref[scan + 1, pl.ds(r0, G), :]
        o = o * lax.rsqrt(jnp.mean(o * o, axis=-1, keepdims=True) + RMS_EPS) * ng_ref[...]
        z = z_ref[0, pl.ds(r0, G), :]
        y_ref[0, pl.ds(r0, G), :] = (o * (z * _sigmoid(z))).astype(y_ref.dtype)

    fill = (t % 2) * N_DIR
    scan = N_DIR - fill

    def group_batch(i, carry):
        groups(fill, scan, i)
        return carry

    def finish_batch(gi, carry):
        finish(scan, gi)
        return carry

    lax.fori_loop(0, ng // D_GROUPS_PER_ITER, group_batch, 0)
    lax.fori_loop(0, ng, finish_batch, 0, unroll=4)


def _delta(proj3, conv_w, gcol, grow, norm_g, off_q, off_k, off_v, off_z, cast_weights):
    b, s, _ = proj3.shape
    w = D_HEADS * D_DIM
    nc = s // D_CHUNK
    trips = s // (D_GROUP * D_GROUPS_PER_ITER)
    assert s % (D_GROUP * D_GROUPS_PER_ITER) == 0 and nc % trips == 0
    bq, bk, bv, bz = (o // D_DIM for o in (off_q, off_k, off_v, off_z))
    n_units = b * D_HEADS
    cur = lambda t: jnp.minimum(t, n_units - 1)
    prev = lambda t: jnp.maximum(t - 1, 0)
    col = lambda unit, blk: lambda t: (unit(t) // D_HEADS, 0, blk + unit(t) % D_HEADS)
    w_specs, w_shapes = _cast_specs(cast_weights, n_units, cur)
    return pl.pallas_call(
        functools.partial(_delta_kernel, len(cast_weights)),
        grid=(n_units + 1,),
        in_specs=[
            pl.BlockSpec((1, s, D_DIM), col(cur, bq)),
            pl.BlockSpec((1, s, D_DIM), col(cur, bk)),
            pl.BlockSpec((1, s, D_DIM), col(cur, bv)),
            pl.BlockSpec((1, s, D_DIM), col(prev, bz)),
            pl.BlockSpec((CONV_W, D_DIM), lambda t: (0, cur(t) % D_HEADS)),
            pl.BlockSpec((CONV_W, D_DIM), lambda t: (0, D_HEADS + cur(t) % D_HEADS)),
            pl.BlockSpec((CONV_W, D_DIM), lambda t: (0, 2 * D_HEADS + cur(t) % D_HEADS)),
            pl.BlockSpec((1, s, LANES), lambda t: (cur(t) // D_HEADS, 0, 0)),
            pl.BlockSpec((1, LANES, s), lambda t: (cur(t) // D_HEADS, 0, 0)),
            pl.BlockSpec((1, D_DIM), lambda t: (0, 0)),
            *w_specs,
        ],
        out_specs=[pl.BlockSpec((1, s, D_DIM), col(prev, 0)), *w_specs],
        out_shape=[jax.ShapeDtypeStruct((b, s, w), BF16), *w_shapes],
        scratch_shapes=[
            pltpu.VMEM((s + 2 * CONV_HALO, D_DIM), F32),
            pltpu.VMEM((s, D_DIM), F32),
            pltpu.VMEM((s, D_DIM), F32),
            pltpu.VMEM((s, D_DIM), F32),
            pltpu.VMEM((2 * N_DIR * nc, D_DIM + D_CHUNK, D_DIM), BF16),
            pltpu.VMEM((2 * N_DIR * nc, D_DIM, D_DIM), F32),
            pltpu.VMEM((2 * N_DIR * nc, 8, D_DIM), F32),
            pltpu.VMEM((2 * N_DIR, s, D_DIM), F32),
            pltpu.VMEM((N_DIR, D_DIM, D_DIM), F32),
        ],
        compiler_params=_cparams(("arbitrary",)),
        name="delta",
    )(proj3, proj3, proj3, proj3, conv_w, conv_w, conv_w, gcol, grow, norm_g, *cast_weights)


def _mix_kernel(n_gate_blocks, x_ref, ym_ref, yd_ref, *refs):
    gate_refs, (wbm_ref, wbd_ref, wo_ref, g2_ref, o_ref, h_ref) = refs[:2 * n_gate_blocks], refs[2 * n_gate_blocks:]
    gate = lambda rs: _sigmoid(jnp.concatenate([r[...] for r in rs], axis=1))
    a = jnp.dot(ym_ref[...], wbm_ref[...], preferred_element_type=F32)
    b = jnp.dot(yd_ref[...], wbd_ref[...], preferred_element_type=F32)
    mixed = gate(gate_refs[:n_gate_blocks]) * a + gate(gate_refs[n_gate_blocks:]) * b
    x = x_ref[...] + jnp.dot(mixed.astype(BF16), wo_ref[...], preferred_element_type=F32)
    o_ref[...] = x
    ms = jnp.mean(x * x, axis=-1, keepdims=True)
    h_ref[...] = (x * lax.rsqrt(ms + RMS_EPS) * g2_ref[...]).astype(BF16)


def _mix(x2, ym2, yd2, proj2, off_gates, wbm, wbd, wo, g2, tm):
    m, d = x2.shape
    const = lambda i: (0, 0)
    gw = math.gcd(off_gates, d)
    gate_specs = [pl.BlockSpec((tm, gw), functools.partial(lambda i, c: (i, c), c=(off_gates + t * gw) // gw))
                  for t in range(2 * d // gw)]
    return pl.pallas_call(
        functools.partial(_mix_kernel, d // gw),
        grid=(m // tm,),
        in_specs=[
            pl.BlockSpec((tm, d), lambda i: (i, 0)),
            pl.BlockSpec((tm, ym2.shape[1]), lambda i: (i, 0)),
            pl.BlockSpec((tm, yd2.shape[1]), lambda i: (i, 0)),
            *gate_specs,
            pl.BlockSpec(wbm.shape, const),
            pl.BlockSpec(wbd.shape, const),
            pl.BlockSpec(wo.shape, const),
            pl.BlockSpec((1, d), const),
        ],
        out_specs=[pl.BlockSpec((tm, d), lambda i: (i, 0)), pl.BlockSpec((tm, d), lambda i: (i, 0))],
        out_shape=[jax.ShapeDtypeStruct((m, d), F32), jax.ShapeDtypeStruct((m, d), BF16)],
        compiler_params=_cparams(("parallel",)),
        name="mix",
    )(x2, ym2, yd2, *([proj2] * len(gate_specs)), wbm, wbd, wo, g2)


def _ffn_kernel(n_slices, x_ref, h_ref, gf_ref, w1_ref, w2_ref, o_ref, acc_ref):
    f = pl.program_id(1)
    last = n_slices - 1

    def ffn_part(sl):
        hid = jnp.dot(h_ref[sl, :], w1_ref[...], preferred_element_type=F32)
        act = jnp.square(jnp.maximum(hid, 0.0)).astype(BF16)
        return jnp.dot(act, w2_ref[...], preferred_element_type=F32)

    @pl.when(jnp.logical_and(f == 0, f < last))
    def _():
        acc_ref[...] = ffn_part(slice(None))

    @pl.when(jnp.logical_and(f > 0, f < last))
    def _():
        acc_ref[...] += ffn_part(slice(None))

    @pl.when(f == last)
    def _():
        rows = x_ref.shape[0] // EDGE_STEP_PIECES
        for r in range(EDGE_STEP_PIECES):
            sl = slice(r * rows, (r + 1) * rows)
            y = ffn_part(sl) if last == 0 else acc_ref[sl, :] + ffn_part(sl)
            x = x_ref[sl, :] + y
            ms = jnp.mean(x * x, axis=-1, keepdims=True)
            o_ref[sl, :] = x * lax.rsqrt(ms + RMS_EPS) * gf_ref[...]


def _ffn(x2, h2, gf, w1, w2, tm, tf):
    m, d = x2.shape
    dff = w1.shape[1]
    return pl.pallas_call(
        functools.partial(_ffn_kernel, dff // tf),
        grid=(m // tm, dff // tf),
        in_specs=[
            pl.BlockSpec((tm, d), lambda i, f: (i, 0)),
            pl.BlockSpec((tm, d), lambda i, f: (i, 0)),
            pl.BlockSpec((1, d), lambda i, f: (0, 0)),
            pl.BlockSpec((d, tf), lambda i, f: (0, f)),
            pl.BlockSpec((tf, d), lambda i, f: (f, 0)),
        ],
        out_specs=pl.BlockSpec((tm, d), lambda i, f: (i, 0)),
        out_shape=jax.ShapeDtypeStruct((m, d), F32),
        scratch_shapes=[pltpu.VMEM((tm, d), F32)],
        compiler_params=_cparams(("parallel", "arbitrary")),
        name="ffn",
    )(x2, h2, gf, w1, w2)


def _pick_tile(n, candidates):
    for c in candidates:
        if n % c == 0:
            return c
    raise ValueError(f"no tile in {candidates} divides {n}")


def _layer(x, norm1_g, w_in, i_bias, f_bias, m_norm_g, conv_w, a_log, dt_bias, d_norm_g,
           w_bm, w_bd, w_out, norm2_g, w_ff1, w_ff2, out_g):
    b, s, d = x.shape
    m = b * s
    mw, mqk, dw = M_HEADS * M_DV, M_HEADS * M_DK, D_HEADS * D_DIM
    nm, nd = N_DIR * M_HEADS, N_DIR * D_HEADS
    splits = (mqk, mqk, mw, mw, nm, nm, 3 * dw, dw, nd, nd, 2 * d)
    bounds = [0]
    for w_ in splits:
        bounds.append(bounds[-1] + w_)
    wt = w_in.T.astype(BF16)
    rows = lambda first, last: wt[bounds[first]:bounds[last + 1]]
    seg_bounds = [(bounds[0], bounds[4]), (bounds[6], bounds[8]), (bounds[10], bounds[11])]
    n_small = 2 * nm + 2 * nd
    wt_small = jnp.concatenate([rows(4, 5), rows(8, 9), jnp.zeros((LANES - n_small, d), BF16)], axis=0)
    zpad = lambda n: jnp.zeros((n,), F32)
    bias_row = jnp.concatenate([i_bias.reshape(-1), f_bias.reshape(-1), dt_bias.reshape(-1), zpad(LANES - CH_B)])
    alog_row = jnp.concatenate([zpad(CH_G), a_log.reshape(-1), zpad(LANES - CH_B)])
    gate_params = jnp.concatenate([bias_row[None], alog_row[None], jnp.zeros((6, LANES), F32)], axis=0)

    x2 = x.reshape(m, d)
    tm = _pick_tile(m, (512, 256, 128))
    tn = next(c for c in (1024, 512, 256) if all((hi - lo) % c == 0 for lo, hi in seg_bounds))
    proj2, smalls2 = _inproj(x2, norm1_g.reshape(1, d), wt, wt_small, seg_bounds,
                             _pick_tile(m, (1024, 512, 256)), tn)
    proj3 = proj2.reshape(b, s, -1)
    gcol, grow = _gate_prep(smalls2.reshape(b, s, LANES), gate_params)

    off_d = 2 * mqk + 2 * mw
    off_g = off_d + 4 * dw
    ym, w_bm16, w_bd16, w_out16 = _mlstm(proj3, gcol, grow, m_norm_g.reshape(M_HEADS, 1, M_DV),
                                         0, mqk, 2 * mqk, 2 * mqk + mw, (w_bm, w_bd, w_out))
    yd, w_ff1_16, w_ff2_16 = _delta(proj3, conv_w, gcol, grow, d_norm_g.reshape(1, D_DIM),
                                    off_d, off_d + dw, off_d + 2 * dw, off_d + 3 * dw, (w_ff1, w_ff2))

    x1, h2 = _mix(x2, ym.reshape(m, mw), yd.reshape(m, dw), proj2, off_g, w_bm16, w_bd16, w_out16,
                  norm2_g.reshape(1, d), _pick_tile(m, (256, 128)))
    tf = _pick_tile(w_ff1.shape[1], (1024, 512, 256))
    out = _ffn(x1, h2, out_g.reshape(1, d), w_ff1_16, w_ff2_16, tm, tf)
    return out.reshape(b, s, d)


def kernel(x, norm1_g, w_in, mlstm_i_bias, mlstm_f_bias, mlstm_norm_g, delta_conv_w, delta_a_log,
           delta_dt_bias, delta_norm_g, w_branch_m, w_branch_d, w_out, norm2_g, w_ff1, w_ff2, norm_f_g):
    depth = w_in.shape[0]
    assert depth == 1, "the fused FFN + final-norm epilogue assumes a single layer"
    return _layer(x, norm1_g[0], w_in[0], mlstm_i_bias[0], mlstm_f_bias[0], mlstm_norm_g[0], delta_conv_w[0],
                  delta_a_log[0], delta_dt_bias[0], delta_norm_g[0], w_branch_m[0], w_branch_d[0], w_out[0],
                  norm2_g[0], w_ff1[0], w_ff2[0], norm_f_g)
```

```python
import functools
import math

import jax
import jax.numpy as jnp
import numpy as np
from jax import lax
from jax.experimental import pallas as pl
from jax.experimental.pallas import tpu as pltpu

F32 = jnp.float32
BF16 = jnp.bfloat16
I32 = jnp.int32

N_DIR = 2
M_HEADS = 4
M_DK = 128
M_DV = 256
D_HEADS = 8
D_DIM = 128
CONV_W = 5
RMS_EPS = 1e-6
L2_EPS = 1e-6

LANES = 128
BF16_SUBLANES = 16
M_CHUNK = 256
M_CHUNKS_PER_ITER = 4
D_CHUNK = 64
D_GROUP = 256
D_PER_GROUP = D_GROUP // D_CHUNK
D_GROUPS_PER_ITER = 2
D_SCAN_STEPS_BETWEEN_STAGES = 8
GATE_ROWS = 256
GATE_BLOCKS_PER_STEP = 4
EDGE_STEP_PIECES = 4
CONV_HALO = 8

CH_I = 0
CH_F = 8
CH_G = 16
CH_B = 32
CH_T = 48

VMEM_LIMIT = 56 * 1024 * 1024

_NT = (((1,), (1,)), ((), ()))


def _cparams(sem):
    return pltpu.CompilerParams(dimension_semantics=sem, vmem_limit_bytes=VMEM_LIMIT)


def _bdot(a, b):
    return jnp.dot(a.astype(BF16), b.astype(BF16), preferred_element_type=F32)


def _bdot_nt(a, b):
    return lax.dot_general(a.astype(BF16), b.astype(BF16), _NT, preferred_element_type=F32)


def _sigmoid(x):
    return 1.0 / (1.0 + jnp.exp(-x))


def _softplus(x):
    return jnp.maximum(x, 0.0) + jnp.log1p(jnp.exp(-jnp.abs(x)))


def _cast_blocks(w_refs, o_refs):
    for w_ref, o_ref in zip(w_refs, o_refs):
        o_ref[...] = w_ref[...].astype(o_ref.dtype)


def _cast_specs(weights, n_steps, step_of):
    specs, shapes = [], []
    for w in weights:
        rows, cols = w.shape
        assert rows % (n_steps * BF16_SUBLANES) == 0
        specs.append(pl.BlockSpec((rows // n_steps, cols), lambda *g: (step_of(*g), 0)))
        shapes.append(jax.ShapeDtypeStruct((rows, cols), BF16))
    return specs, shapes


def _lane_pick(x, ch):
    lane = lax.broadcasted_iota(I32, x.shape, 1)
    return jnp.sum(jnp.where(lane == ch, x, 0.0), axis=1, keepdims=True)


def _inproj_kernel(x_ref, g_ref, w_ref, ws_ref, o_ref, os_ref, hn_ref):
    j = pl.program_id(1)

    @pl.when(j == 0)
    def _():
        rows = x_ref.shape[0] // EDGE_STEP_PIECES
        for r in range(EDGE_STEP_PIECES):
            sl = slice(r * rows, (r + 1) * rows)
            x = x_ref[sl, :]
            ms = jnp.mean(x * x, axis=-1, keepdims=True)
            hn = (x * lax.rsqrt(ms + RMS_EPS) * g_ref[...]).astype(BF16)
            hn_ref[sl, :] = hn
            os_ref[sl, :] = lax.dot_general(hn, ws_ref[...], _NT, preferred_element_type=F32)
            o_ref[sl, :] = lax.dot_general(hn, w_ref[...], _NT, preferred_element_type=F32)

    @pl.when(j > 0)
    def _():
        o_ref[...] = lax.dot_general(hn_ref[...], w_ref[...], _NT, preferred_element_type=F32)


def _inproj(x2, g, wt, wt_small, seg_bounds, tm, tn):
    m, d = x2.shape
    starts = [0]
    for lo, hi in seg_bounds:
        starts.append(starts[-1] + (hi - lo) // tn)

    def w_row(i, j):
        row = seg_bounds[0][0] + j * tn
        for k in range(1, len(seg_bounds)):
            row = jnp.where(j >= starts[k], seg_bounds[k][0] + (j - starts[k]) * tn, row)
        return pl.multiple_of(row, BF16_SUBLANES), 0

    return pl.pallas_call(
        _inproj_kernel,
        grid=(m // tm, starts[-1]),
        in_specs=[
            pl.BlockSpec((tm, d), lambda i, j: (i, 0)),
            pl.BlockSpec((1, d), lambda i, j: (0, 0)),
            pl.BlockSpec((pl.Element(tn), pl.Element(d)), w_row),
            pl.BlockSpec((LANES, d), lambda i, j: (0, 0)),
        ],
        out_specs=[
            pl.BlockSpec((tm, tn), lambda i, j: (i, j)),
            pl.BlockSpec((tm, LANES), lambda i, j: (i, 0)),
        ],
        out_shape=[jax.ShapeDtypeStruct((m, starts[-1] * tn), F32), jax.ShapeDtypeStruct((m, LANES), F32)],
        scratch_shapes=[pltpu.VMEM((tm, d), BF16)],
        compiler_params=_cparams(("parallel", "arbitrary")),
        name="inproj",
    )(x2, g, wt, wt_small)


def _cumsum_matrices():
    rr, cc = np.indices((GATE_ROWS, GATE_ROWS))
    low, upp = cc <= rr, cc >= rr
    same_d = (rr // D_CHUNK) == (cc // D_CHUNK)
    same_m = (rr // M_CHUNK) == (cc // M_CHUNK)
    mats = np.concatenate([
        low & same_m,
        upp & same_m,
        low & same_d,
        upp & same_d,
        same_d,
    ], axis=0)
    return jnp.asarray(mats.astype(np.float32), dtype=BF16)


def _gate_kernel(sm_ref, par_ref, mats_ref, col_ref, row_ref):
    r = GATE_ROWS
    lane = lax.broadcasted_iota(I32, (r, LANES), 1)
    neg_a = -jnp.exp(par_ref[1:2, :])
    xes, x3s = [], []
    for blk in range(GATE_BLOCKS_PER_STEP):
        x = sm_ref[0, blk * r:(blk + 1) * r, :] + par_ref[0:1, :]
        logf = -_softplus(-x)
        g = neg_a * _softplus(x)
        beta = _sigmoid(x)
        xe = jnp.where(lane < CH_F, x,
                       jnp.where(lane < CH_G, logf,
                                 jnp.where(lane < CH_B, g,
                                           jnp.where(lane < CH_T, beta, 0.0))))
        hi = xe.astype(BF16)
        r1 = xe - hi.astype(F32)
        mid = r1.astype(BF16)
        lo = (r1 - mid.astype(F32)).astype(BF16)
        xes.append(xe)
        x3s.append(jnp.concatenate([hi, mid, lo], axis=1))
    y3s = [jnp.dot(mats_ref[...], x3, preferred_element_type=F32) for x3 in x3s]
    half_m = CH_F + M_HEADS
    half_d = CH_G + D_HEADS
    for blk, (xe, y3) in enumerate(zip(xes, y3s)):
        y = y3[:, 0:LANES] + y3[:, LANES:2 * LANES] + y3[:, 2 * LANES:3 * LANES]
        pm, sm_, pd, sd, td = (y[i * r:(i + 1) * r] for i in range(5))
        out = jnp.where(lane < CH_F, xe,
              jnp.where(lane < half_m, pm,
              jnp.where(lane < CH_G, sm_,
              jnp.where(lane < half_d, pd,
              jnp.where(lane < CH_B, sd,
              jnp.where(lane < CH_T, xe,
              jnp.where(lane < CH_T + N_DIR * D_HEADS, pltpu.roll(td, CH_T - CH_G, axis=1), 0.0)))))))
        col_ref[0, blk * r:(blk + 1) * r, :] = out
        row_ref[0, :, blk * r:(blk + 1) * r] = out.T


def _gate_prep(smalls3, params):
    b, s, _ = smalls3.shape
    r = GATE_ROWS * GATE_BLOCKS_PER_STEP
    assert s % r == 0
    return pl.pallas_call(
        _gate_kernel,
        grid=(b, s // r),
        in_specs=[
            pl.BlockSpec((1, r, LANES), lambda i, j: (i, j, 0)),
            pl.BlockSpec((8, LANES), lambda i, j: (0, 0)),
            pl.BlockSpec((5 * GATE_ROWS, GATE_ROWS), lambda i, j: (0, 0)),
        ],
        out_specs=[
            pl.BlockSpec((1, r, LANES), lambda i, j: (i, j, 0)),
            pl.BlockSpec((1, LANES, r), lambda i, j: (i, 0, j)),
        ],
        out_shape=[jax.ShapeDtypeStruct((b, s, LANES), F32), jax.ShapeDtypeStruct((b, LANES, s), F32)],
        compiler_params=_cparams(("parallel", "parallel")),
        name="gateprep",
    )(smalls3, params, _cumsum_matrices())


def _mlstm_kernel(n_cast, q_ref, k_ref, v_ref, o_ref, gc_ref, gr_ref, ng_ref, *refs):
    w_refs, y_ref, wo_refs = refs[:n_cast], refs[n_cast], refs[n_cast + 1:2 * n_cast + 1]
    hn_ref, den_ref, mi_ref, cu_ref, nu_ref, bl_ref, ml_ref, cs_ref, ns_ref, ms_ref, c_ref = refs[2 * n_cast + 1:]
    _cast_blocks(w_refs, wo_refs)
    head = pl.program_id(1)
    s = q_ref.shape[1]
    L = M_CHUNK
    nc = s // L
    scale = M_DK ** -0.5
    rr = lax.broadcasted_iota(I32, (L, L), 0)
    cc = lax.broadcasted_iota(I32, (L, L), 1)

    def bcast8(x11):
        return jnp.broadcast_to(x11, (8, LANES))

    def local(cs):
        chains = []
        for c in cs:
            r0 = pl.multiple_of(c * L, L)
            q = q_ref[0, pl.ds(r0, L), :]
            k = k_ref[0, pl.ds(r0, L), :] * scale
            vb = v_ref[0, pl.ds(r0, L), :].astype(BF16)
            kb = k.astype(BF16)
            qk = lax.dot_general(q.astype(BF16), kb, _NT, preferred_element_type=F32)
            k_t = k.T
            gcol = gc_ref[0, pl.ds(r0, L), :]
            for d in range(N_DIR):
                ch_i = CH_I + d * M_HEADS + head
                ch_f = CH_F + d * M_HEADS + head
                bc_col = _lane_pick(gcol, ch_f)
                i_row = gr_ref[0, pl.ds(ch_i, 1), pl.ds(r0, L)]
                bc_row = gr_ref[0, pl.ds(ch_f, 1), pl.ds(r0, L)]
                a_row = i_row - bc_row
                mask = (cc <= rr) if d == 0 else (cc >= rr)
                log_intra = jnp.where(mask, bc_col + a_row, -jnp.inf)
                m_intra = jnp.max(log_intra, axis=1, keepdims=True)
                p = qk * jnp.exp(log_intra - m_intra)
                b_last = bc_row[:, L - 1:L] if d == 0 else bc_row[:, 0:1]
                log_state = b_last + a_row
                m_loc = jnp.max(log_state, axis=1, keepdims=True)
                w_state = jnp.exp(log_state - m_loc)
                chains.append(dict(
                    idx=d * nc + c, d=d, r0=r0, vb=vb, kb=kb, pb=p.astype(BF16),
                    den=jnp.sum(p, axis=1, keepdims=True), m_intra=m_intra,
                    kw=(k_t * w_state).astype(BF16), w8=jnp.broadcast_to(w_state, (8, L)).astype(BF16),
                    b_last=b_last, m_loc=m_loc))
        for c in chains:
            hn_ref[c["d"], pl.ds(c["r0"], L), :] = jnp.dot(c["pb"], c["vb"], preferred_element_type=F32)
            den_ref[c["d"], pl.ds(c["r0"], L), :] = jnp.broadcast_to(c["den"], (L, LANES))
            mi_ref[c["d"], pl.ds(c["r0"], L), :] = jnp.broadcast_to(c["m_intra"], (L, LANES))
        for c in chains:
            cu_ref[c["idx"]] = jnp.dot(c["kw"], c["vb"], preferred_element_type=F32)
            nu_ref[c["idx"]] = jnp.dot(c["w8"], c["kb"], preferred_element_type=F32)
            bl_ref[c["idx"]] = bcast8(c["b_last"])
            ml_ref[c["idx"]] = bcast8(c["m_loc"])

    def local_trip(i, _):
        local([M_CHUNKS_PER_ITER * i + t for t in range(M_CHUNKS_PER_ITER)])
        return 0

    lax.fori_loop(0, nc // M_CHUNKS_PER_ITER, local_trip, 0)

    c_ref[...] = jnp.zeros(c_ref.shape, F32)

    def scan(j, carry):
        out = []
        for d in range(N_DIR):
            m_st, n_st = carry[d]
            idx = d * nc + (j if d == 0 else nc - 1 - j)
            c_st = c_ref[d]
            cs_ref[idx] = c_st.astype(BF16)
            ns_ref[idx] = n_st
            ms_ref[idx] = m_st
            bl = bl_ref[idx]
            ml = ml_ref[idx]
            m_new = jnp.maximum(bl + m_st, ml)
            decay = jnp.exp(bl + m_st - m_new)
            gain = jnp.exp(ml - m_new)
            c_ref[d] = decay[0:1, 0:1] * c_st + gain[0:1, 0:1] * cu_ref[idx]
            out.append((m_new, decay * n_st + gain * nu_ref[idx]))
        return tuple(out)

    m0 = jnp.full((8, LANES), -jnp.inf, F32)
    n0 = jnp.zeros((8, M_DK), F32)
    lax.fori_loop(0, nc, scan, ((m0, n0), (m0, n0)))

    def wide(x):
        return jnp.concatenate([x] * (M_DV // LANES), axis=1)

    def combine(c, _):
        r0 = pl.multiple_of(c * L, L)
        q = q_ref[0, pl.ds(r0, L), :]
        qb = q.astype(BF16)
        gcol = gc_ref[0, pl.ds(r0, L), :]
        hh = None
        for d in range(N_DIR):
            idx = d * nc + c
            bc = jnp.broadcast_to(_lane_pick(gcol, CH_F + d * M_HEADS + head), (L, LANES))
            den_i = den_ref[d, pl.ds(r0, L), :]
            m_i = mi_ref[d, pl.ds(r0, L), :]
            log_inter = bc + ms_ref[idx][0:1, :]
            m_row = jnp.maximum(log_inter, m_i)
            w_i = jnp.exp(m_i - m_row)
            w_x = jnp.exp(log_inter - m_row)
            q_c = jnp.dot(qb, cs_ref[idx], preferred_element_type=F32)
            q_n = jnp.broadcast_to(jnp.sum(q * ns_ref[idx][0:1, :], axis=1, keepdims=True), (L, LANES))
            num = wide(w_i) * hn_ref[d, pl.ds(r0, L), :] + wide(w_x) * q_c
            den = w_i * den_i + w_x * q_n
            h_d = num * wide(1.0 / jnp.maximum(jnp.abs(den), jnp.exp(-m_row)))
            hh = h_d if hh is None else hh + h_d
        hh = hh * lax.rsqrt(jnp.mean(hh * hh, axis=-1, keepdims=True) + RMS_EPS)
        hh = hh * ng_ref[0]
        y_ref[0, pl.ds(r0, L), :] = (_sigmoid(o_ref[0, pl.ds(r0, L), :]) * hh).astype(y_ref.dtype)
        return 0

    lax.fori_loop(0, nc, combine, 0, unroll=2)


def _mlstm(proj3, gcol, grow, norm_g, off_q, off_k, off_v, off_o, cast_weights):
    b, s, _ = proj3.shape
    w_specs, w_shapes = _cast_specs(cast_weights, b * M_HEADS, lambda i, h: i * M_HEADS + h)
    assert s % (M_CHUNK * M_CHUNKS_PER_ITER) == 0
    nc = s // M_CHUNK
    bq, bk = off_q // M_DK, off_k // M_DK
    bv, bo = off_v // M_DV, off_o // M_DV
    return pl.pallas_call(
        functools.partial(_mlstm_kernel, len(cast_weights)),
        grid=(b, M_HEADS),
        in_specs=[
            pl.BlockSpec((1, s, M_DK), lambda i, h: (i, 0, bq + h)),
            pl.BlockSpec((1, s, M_DK), lambda i, h: (i, 0, bk + h)),
            pl.BlockSpec((1, s, M_DV), lambda i, h: (i, 0, bv + h)),
            pl.BlockSpec((1, s, M_DV), lambda i, h: (i, 0, bo + h)),
            pl.BlockSpec((1, s, LANES), lambda i, h: (i, 0, 0)),
            pl.BlockSpec((1, LANES, s), lambda i, h: (i, 0, 0)),
            pl.BlockSpec((1, 1, M_DV), lambda i, h: (h, 0, 0)),
            *w_specs,
        ],
        out_specs=[pl.BlockSpec((1, s, M_DV), lambda i, h: (i, 0, h)), *w_specs],
        out_shape=[jax.ShapeDtypeStruct((b, s, M_HEADS * M_DV), BF16), *w_shapes],
        scratch_shapes=[
            pltpu.VMEM((N_DIR, s, M_DV), F32),
            pltpu.VMEM((N_DIR, s, LANES), F32),
            pltpu.VMEM((N_DIR, s, LANES), F32),
            pltpu.VMEM((N_DIR * nc, M_DK, M_DV), F32),
            pltpu.VMEM((N_DIR * nc, 8, M_DK), F32),
            pltpu.VMEM((N_DIR * nc, 8, LANES), F32),
            pltpu.VMEM((N_DIR * nc, 8, LANES), F32),
            pltpu.VMEM((N_DIR * nc, M_DK, M_DV), BF16),
            pltpu.VMEM((N_DIR * nc, 8, M_DK), F32),
            pltpu.VMEM((N_DIR * nc, 8, LANES), F32),
            pltpu.VMEM((N_DIR, M_DK, M_DV), F32),
        ],
        compiler_params=_cparams(("parallel", "arbitrary")),
        name="mlstm",
    )(proj3, proj3, proj3, proj3, gcol, grow, norm_g, *cast_weights)


def _delta_kernel(n_cast, q_ref, k_ref, v_ref, z_ref, cwq_ref, cwk_ref, cwv_ref, gc_ref, gr_ref, ng_ref, *refs):
    w_refs, y_ref, wo_refs = refs[:n_cast], refs[n_cast], refs[n_cast + 1:2 * n_cast + 1]
    xp_ref, qs_ref, ks_ref, vs_ref, kq_ref, n_ref, dec_ref, o_ref, st_ref = refs[2 * n_cast + 1:]
    _cast_blocks(w_refs, wo_refs)
    t = pl.program_id(0)
    head = jnp.minimum(t, pl.num_programs(0) - 2) % D_HEADS
    s = q_ref.shape[1]
    L = D_CHUNK
    G = D_GROUP
    nc = s // L
    ng = s // G
    halo = CONV_HALO
    pad = CONV_W // 2

    @pl.when(t == 0)
    def _():
        for r in (kq_ref, n_ref, dec_ref, o_ref):
            r[...] = jnp.zeros(r.shape, r.dtype)

    xp_ref[0:halo, :] = jnp.zeros((halo, D_DIM), F32)
    xp_ref[s + halo:s + 2 * halo, :] = jnp.zeros((halo, D_DIM), F32)

    def conv_into(x_ref, cw_ref, dst_ref, l2, mult):
        xp_ref[halo:s + halo, :] = x_ref[0]
        for blk in range(ng):
            base = halo + blk * G - pad
            acc = xp_ref[base:base + G, :] * cw_ref[0:1, :]
            for j in range(1, CONV_W):
                acc = acc + xp_ref[base + j:base + j + G, :] * cw_ref[j:j + 1, :]
            y = acc * _sigmoid(acc)
            if l2:
                y = y * (lax.rsqrt(jnp.sum(y * y, axis=-1, keepdims=True) + L2_EPS) * mult)
            dst_ref[blk * G:(blk + 1) * G, :] = y

    conv_into(q_ref, cwq_ref, qs_ref, True, D_DIM ** -0.5)
    conv_into(k_ref, cwk_ref, ks_ref, True, 1.0)
    conv_into(v_ref, cwv_ref, vs_ref, False, 1.0)

    st_ref[...] = jnp.zeros(st_ref.shape, F32)

    def scan_step(scan, j):
        for d in range(N_DIR):
            c = j if d == 0 else nc - 1 - j
            r0 = pl.multiple_of(c * L, L)
            idx = (scan + d) * nc + c
            st = st_ref[d]
            res = jnp.dot(kq_ref[idx], st.astype(BF16), preferred_element_type=F32)
            st_ref[d] = dec_ref[idx][0:1, :] * st + res[0:D_DIM] + n_ref[idx]
            o_ref[scan + d, pl.ds(r0, L), :] = o_ref[scan + d, pl.ds(r0, L), :] + res[D_DIM:D_DIM + L]

    tchunk = lax.broadcasted_iota(I32, (D_DIM, D_GROUP), 1) // D_CHUNK
    rr = lax.broadcasted_iota(I32, (G, G), 0)
    cc = lax.broadcasted_iota(I32, (G, G), 1)
    same = (rr // L) == (cc // L)
    pr = lax.broadcasted_iota(I32, (L, G), 0)
    pc = lax.broadcasted_iota(I32, (L, G), 1)
    eye_p = jnp.where((pc % L) == pr, 1.0, 0.0).astype(F32)

    def pack(m):
        out = m[0:L]
        for i in range(1, D_PER_GROUP):
            out = out + m[i * L:(i + 1) * L]
        return out

    def unpack(p):
        return jnp.where(same, jnp.concatenate([p] * D_PER_GROUP, axis=0), 0.0)

    scan_per_trip = nc // (ng // D_GROUPS_PER_ITER)

    def groups(fill, scan, trip):
        pending = iter(range(scan_per_trip))

        def scan_some(n):
            for _ in range(n):
                j = next(pending, None)
                if j is not None:
                    scan_step(scan, trip * scan_per_trip + j)

        chains = []
        for gi in [D_GROUPS_PER_ITER * trip + g for g in range(D_GROUPS_PER_ITER)]:
            r0 = pl.multiple_of(gi * G, G)
            kk_ = ks_ref[pl.ds(r0, G), :]
            qq_ = qs_ref[pl.ds(r0, G), :]
            vv_ = vs_ref[pl.ds(r0, G), :]
            kb = kk_.astype(BF16)
            kk = lax.dot_general(kb, kb, _NT, preferred_element_type=F32)
            qk = lax.dot_general(qq_.astype(BF16), kb, _NT, preferred_element_type=F32)
            gcol = gc_ref[0, pl.ds(r0, G), :]
            for d in range(N_DIR):
                ch = d * D_HEADS + head
                g_col = _lane_pick(gcol, CH_G + ch)
                b_col = _lane_pick(gcol, CH_B + ch)
                t_col = _lane_pick(gcol, CH_T + ch)
                g_row = gr_ref[0, pl.ds(CH_G + ch, 1), pl.ds(r0, G)]
                tri = (cc <= rr) if d == 0 else (cc >= rr)
                strict = (cc < rr) if d == 0 else (cc > rr)
                gam = jnp.exp(jnp.where(tri, jnp.where(same, g_col - g_row, -jnp.inf), -jnp.inf))
                x = jnp.where(strict, -(b_col * kk * gam), 0.0)
                eg = jnp.exp(g_col)
                chains.append(dict(
                    gi=gi, slot=fill + d, r0=r0, x=x, xp=pack(x), attn=(qk * gam).astype(BF16),
                    rhs=jnp.concatenate([b_col * vv_, (b_col * eg) * kk_], axis=1).astype(BF16),
                    qg=qq_ * eg, kdt=(kk_ * jnp.exp(t_col - g_col)).T, dec=jnp.exp(t_col)))
            scan_some(-(-(scan_per_trip - D_SCAN_STEPS_BETWEEN_STAGES) // D_GROUPS_PER_ITER))
        for c in chains:
            c["tp"] = eye_p + c["xp"]
            c["p"] = _bdot(c["xp"], c["x"])
        scan_some(1)
        for _it in range(4):
            for c in chains:
                res = _bdot(jnp.concatenate([c["tp"], c["p"]], axis=0), unpack(c["p"]))
                c["tp"] = c["tp"] + res[0:L]
                c["p"] = res[L:2 * L]
            scan_some(1)
        for c in chains:
            c["tp"] = c["tp"] + _bdot(c["tp"], unpack(c["p"]))
        scan_some(1)
        for c in chains:
            c["uw"] = _bdot(unpack(c["tp"]), c["rhs"]).astype(BF16)
        scan_some(1)
        for c in chains:
            au = jnp.dot(c["attn"], c["uw"], preferred_element_type=F32)
            o_ref[c["slot"], pl.ds(c["r0"], G), :] = au[:, 0:D_DIM]
            c["qe"] = (c["qg"] - au[:, D_DIM:2 * D_DIM]).astype(BF16)
        scan_some(1)
        for c in chains:
            for ci in range(D_PER_GROUP):
                idx = c["slot"] * nc + c["gi"] * D_PER_GROUP + ci
                ku = _bdot(jnp.where(tchunk == ci, c["kdt"], 0.0), c["uw"])
                n_ref[idx] = ku[:, 0:D_DIM]
                kq_ref[idx, 0:D_DIM, :] = (-ku[:, D_DIM:2 * D_DIM]).astype(BF16)
                kq_ref[idx, D_DIM:D_DIM + L, :] = c["qe"][ci * L:(ci + 1) * L]
                dec_ref[idx] = jnp.broadcast_to(c["dec"][ci * L:ci * L + 8], (8, D_DIM))
        scan_some(scan_per_trip)

    def finish(scan, gi):
        r0 = pl.multiple_of(gi * G, G)
        o = o_ref[scan, pl.ds(r0, G), :] + o_ref[scan + 1, pl.ds(r0, G), :]
        o = o * lax.rsqrt(jnp.mean(o * o, axis=-1, keepdims=True) + RMS_EPS) * ng_ref[...]
        z = z_ref[0, pl.ds(r0, G), :]
        y_ref[0, pl.ds(r0, G), :] = (o * (z * _sigmoid(z))).astype(y_ref.dtype)

    fill = (t % 2) * N_DIR
    scan = N_DIR - fill

    def group_batch(i, carry):
        groups(fill, scan, i)
        return carry

    def finish_batch(gi, carry):
        finish(scan, gi)
        return carry

    lax.fori_loop(0, ng // D_GROUPS_PER_ITER, group_batch, 0)
    lax.fori_loop(0, ng, finish_batch, 0, unroll=4)


def _delta(proj3, conv_w, gcol, grow, norm_g, off_q, off_k, off_v, off_z, cast_weights):
    b, s, _ = proj3.shape
    w = D_HEADS * D_DIM
    nc = s // D_CHUNK
    trips = s // (D_GROUP * D_GROUPS_PER_ITER)
    assert s % (D_GROUP * D_GROUPS_PER_ITER) == 0 and nc % trips == 0
    bq, bk, bv, bz = (o // D_DIM for o in (off_q, off_k, off_v, off_z))
    n_units = b * D_HEADS
    cur = lambda t: jnp.minimum(t, n_units - 1)
    prev = lambda t: jnp.maximum(t - 1, 0)
    col = lambda unit, blk: lambda t: (unit(t) // D_HEADS, 0, blk + unit(t) % D_HEADS)
    w_specs, w_shapes = _cast_specs(cast_weights, n_units, cur)
    return pl.pallas_call(
        functools.partial(_delta_kernel, len(cast_weights)),
        grid=(n_units + 1,),
        in_specs=[
            pl.BlockSpec((1, s, D_DIM), col(cur, bq)),
            pl.BlockSpec((1, s, D_DIM), col(cur, bk)),
            pl.BlockSpec((1, s, D_DIM), col(cur, bv)),
            pl.BlockSpec((1, s, D_DIM), col(prev, bz)),
            pl.BlockSpec((CONV_W, D_DIM), lambda t: (0, cur(t) % D_HEADS)),
            pl.BlockSpec((CONV_W, D_DIM), lambda t: (0, D_HEADS + cur(t) % D_HEADS)),
            pl.BlockSpec((CONV_W, D_DIM), lambda t: (0, 2 * D_HEADS + cur(t) % D_HEADS)),
            pl.BlockSpec((1, s, LANES), lambda t: (cur(t) // D_HEADS, 0, 0)),
            pl.BlockSpec((1, LANES, s), lambda t: (cur(t) // D_HEADS, 0, 0)),
            pl.BlockSpec((1, D_DIM), lambda t: (0, 0)),
            *w_specs,
        ],
        out_specs=[pl.BlockSpec((1, s, D_DIM), col(prev, 0)), *w_specs],
        out_shape=[jax.ShapeDtypeStruct((b, s, w), BF16), *w_shapes],
        scratch_shapes=[
            pltpu.VMEM((s + 2 * CONV_HALO, D_DIM), F32),
            pltpu.VMEM((s, D_DIM), F32),
            pltpu.VMEM((s, D_DIM), F32),
            pltpu.VMEM((s, D_DIM), F32),
            pltpu.VMEM((2 * N_DIR * nc, D_DIM + D_CHUNK, D_DIM), BF16),
            pltpu.VMEM((2 * N_DIR * nc, D_DIM, D_DIM), F32),
            pltpu.VMEM((2 * N_DIR * nc, 8, D_DIM), F32),
            pltpu.VMEM((2 * N_DIR, s, D_DIM), F32),
            pltpu.VMEM((N_DIR, D_DIM, D_DIM), F32),
        ],
        compiler_params=_cparams(("arbitrary",)),
        name="delta",
    )(proj3, proj3, proj3, proj3, conv_w, conv_w, conv_w, gcol, grow, norm_g, *cast_weights)


def _mix_kernel(n_gate_blocks, x_ref, ym_ref, yd_ref, *refs):
    gate_refs, (wbm_ref, wbd_ref, wo_ref, g2_ref, o_ref, h_ref) = refs[:2 * n_gate_blocks], refs[2 * n_gate_blocks:]
    gate = lambda rs: _sigmoid(jnp.concatenate([r[...] for r in rs], axis=1))
    a = jnp.dot(ym_ref[...], wbm_ref[...], preferred_element_type=F32)
    b = jnp.dot(yd_ref[...], wbd_ref[...], preferred_element_type=F32)
    mixed = gate(gate_refs[:n_gate_blocks]) * a + gate(gate_refs[n_gate_blocks:]) * b
    x = x_ref[...] + jnp.dot(mixed.astype(BF16), wo_ref[...], preferred_element_type=F32)
    o_ref[...] = x
    ms = jnp.mean(x * x, axis=-1, keepdims=True)
    h_ref[...] = (x * lax.rsqrt(ms + RMS_EPS) * g2_ref[...]).astype(BF16)


def _mix(x2, ym2, yd2, proj2, off_gates, wbm, wbd, wo, g2, tm):
    m, d = x2.shape
    const = lambda i: (0, 0)
    gw = math.gcd(off_gates, d)
    gate_specs = [pl.BlockSpec((tm, gw), functools.partial(lambda i, c: (i, c), c=(off_gates + t * gw) // gw))
                  for t in range(2 * d // gw)]
    return pl.pallas_call(
        functools.partial(_mix_kernel, d // gw),
        grid=(m // tm,),
        in_specs=[
            pl.BlockSpec((tm, d), lambda i: (i, 0)),
            pl.BlockSpec((tm, ym2.shape[1]), lambda i: (i, 0)),
            pl.BlockSpec((tm, yd2.shape[1]), lambda i: (i, 0)),
            *gate_specs,
            pl.BlockSpec(wbm.shape, const),
            pl.BlockSpec(wbd.shape, const),
            pl.BlockSpec(wo.shape, const),
            pl.BlockSpec((1, d), const),
        ],
        out_specs=[pl.BlockSpec((tm, d), lambda i: (i, 0)), pl.BlockSpec((tm, d), lambda i: (i, 0))],
        out_shape=[jax.ShapeDtypeStruct((m, d), F32), jax.ShapeDtypeStruct((m, d), BF16)],
        compiler_params=_cparams(("parallel",)),
        name="mix",
    )(x2, ym2, yd2, *([proj2] * len(gate_specs)), wbm, wbd, wo, g2)


def _ffn_kernel(n_slices, x_ref, h_ref, gf_ref, w1_ref, w2_ref, o_ref, acc_ref):
    f = pl.program_id(1)
    last = n_slices - 1

    def ffn_part(sl):
        hid = jnp.dot(h_ref[sl, :], w1_ref[...], preferred_element_type=F32)
        act = jnp.square(jnp.maximum(hid, 0.0)).astype(BF16)
        return jnp.dot(act, w2_ref[...], preferred_element_type=F32)

    @pl.when(jnp.logical_and(f == 0, f < last))
    def _():
        acc_ref[...] = ffn_part(slice(None))

    @pl.when(jnp.logical_and(f > 0, f < last))
    def _():
        acc_ref[...] += ffn_part(slice(None))

    @pl.when(f == last)
    def _():
        rows = x_ref.shape[0] // EDGE_STEP_PIECES
        for r in range(EDGE_STEP_PIECES):
            sl = slice(r * rows, (r + 1) * rows)
            y = ffn_part(sl) if last == 0 else acc_ref[sl, :] + ffn_part(sl)
            x = x_ref[sl, :] + y
            ms = jnp.mean(x * x, axis=-1, keepdims=True)
            o_ref[sl, :] = x * lax.rsqrt(ms + RMS_EPS) * gf_ref[...]


def _ffn(x2, h2, gf, w1, w2, tm, tf):
    m, d = x2.shape
    dff = w1.shape[1]
    return pl.pallas_call(
        functools.partial(_ffn_kernel, dff // tf),
        grid=(m // tm, dff // tf),
        in_specs=[
            pl.BlockSpec((tm, d), lambda i, f: (i, 0)),
            pl.BlockSpec((tm, d), lambda i, f: (i, 0)),
            pl.BlockSpec((1, d), lambda i, f: (0, 0)),
            pl.BlockSpec((d, tf), lambda i, f: (0, f)),
            pl.BlockSpec((tf, d), lambda i, f: (f, 0)),
        ],
        out_specs=pl.BlockSpec((tm, d), lambda i, f: (i, 0)),
        out_shape=jax.ShapeDtypeStruct((m, d), F32),
        scratch_shapes=[pltpu.VMEM((tm, d), F32)],
        compiler_params=_cparams(("parallel", "arbitrary")),
        name="ffn",
    )(x2, h2, gf, w1, w2)


def _pick_tile(n, candidates):
    for c in candidates:
        if n % c == 0:
            return c
    raise ValueError(f"no tile in {candidates} divides {n}")


def _layer(x, norm1_g, w_in, i_bias, f_bias, m_norm_g, conv_w, a_log, dt_bias, d_norm_g,
           w_bm, w_bd, w_out, norm2_g, w_ff1, w_ff2, out_g):
    b, s, d = x.shape
    m = b * s
    mw, mqk, dw = M_HEADS * M_DV, M_HEADS * M_DK, D_HEADS * D_DIM
    nm, nd = N_DIR * M_HEADS, N_DIR * D_HEADS
    splits = (mqk, mqk, mw, mw, nm, nm, 3 * dw, dw, nd, nd, 2 * d)
    bounds = [0]
    for w_ in splits:
        bounds.append(bounds[-1] + w_)
    wt = w_in.T.astype(BF16)
    rows = lambda first, last: wt[bounds[first]:bounds[last + 1]]
    seg_bounds = [(bounds[0], bounds[4]), (bounds[6], bounds[8]), (bounds[10], bounds[11])]
    n_small = 2 * nm + 2 * nd
    wt_small = jnp.concatenate([rows(4, 5), rows(8, 9), jnp.zeros((LANES - n_small, d), BF16)], axis=0)
    zpad = lambda n: jnp.zeros((n,), F32)
    bias_row = jnp.concatenate([i_bias.reshape(-1), f_bias.reshape(-1), dt_bias.reshape(-1), zpad(LANES - CH_B)])
    alog_row = jnp.concatenate([zpad(CH_G), a_log.reshape(-1), zpad(LANES - CH_B)])
    gate_params = jnp.concatenate([bias_row[None], alog_row[None], jnp.zeros((6, LANES), F32)], axis=0)

    x2 = x.reshape(m, d)
    tm = _pick_tile(m, (512, 256, 128))
    tn = next(c for c in (1024, 512, 256) if all((hi - lo) % c == 0 for lo, hi in seg_bounds))
    proj2, smalls2 = _inproj(x2, norm1_g.reshape(1, d), wt, wt_small, seg_bounds,
                             _pick_tile(m, (1024, 512, 256)), tn)
    proj3 = proj2.reshape(b, s, -1)
    gcol, grow = _gate_prep(smalls2.reshape(b, s, LANES), gate_params)

    off_d = 2 * mqk + 2 * mw
    off_g = off_d + 4 * dw
    ym, w_bm16, w_bd16, w_out16 = _mlstm(proj3, gcol, grow, m_norm_g.reshape(M_HEADS, 1, M_DV),
                                         0, mqk, 2 * mqk, 2 * mqk + mw, (w_bm, w_bd, w_out))
    yd, w_ff1_16, w_ff2_16 = _delta(proj3, conv_w, gcol, grow, d_norm_g.reshape(1, D_DIM),
                                    off_d, off_d + dw, off_d + 2 * dw, off_d + 3 * dw, (w_ff1, w_ff2))

    x1, h2 = _mix(x2, ym.reshape(m, mw), yd.reshape(m, dw), proj2, off_g, w_bm16, w_bd16, w_out16,
                  norm2_g.reshape(1, d), _pick_tile(m, (256, 128)))
    tf = _pick_tile(w_ff1.shape[1], (1024, 512, 256))
    out = _ffn(x1, h2, out_g.reshape(1, d), w_ff1_16, w_ff2_16, tm, tf)
    return out.reshape(b, s, d)


def kernel(x, norm1_g, w_in, mlstm_i_bias, mlstm_f_bias, mlstm_norm_g, delta_conv_w, delta_a_log,
           delta_dt_bias, delta_norm_g, w_branch_m, w_branch_d, w_out, norm2_g, w_ff1, w_ff2, norm_f_g):
    depth = w_in.shape[0]
    assert depth == 1, "the fused FFN + final-norm epilogue assumes a single layer"
    return _layer(x, norm1_g[0], w_in[0], mlstm_i_bias[0], mlstm_f_bias[0], mlstm_norm_g[0], delta_conv_w[0],
                  delta_a_log[0], delta_dt_bias[0], delta_norm_g[0], w_branch_m[0], w_branch_d[0], w_out[0],
                  norm2_g[0], w_ff1[0], w_ff2[0], norm_f_g)
```

```python
import functools
import math

import jax
import jax.numpy as jnp
import numpy as np
from jax import lax
from jax.experimental import pallas as pl
from jax.experimental.pallas import tpu as pltpu

F32 = jnp.float32
BF16 = jnp.bfloat16
I32 = jnp.int32

N_DIR = 2
M_HEADS = 4
M_DK = 128
M_DV = 256
D_HEADS = 8
D_DIM = 128
CONV_W = 5
RMS_EPS = 1e-6
L2_EPS = 1e-6
LOG2_E = math.log2(math.e)

LANES = 128
BF16_SUBLANES = 16
M_CHUNK = 256
M_CHUNKS_PER_ITER = 4
D_CHUNK = 64
D_GROUP = 256
D_PER_GROUP = D_GROUP // D_CHUNK
D_GROUPS_PER_ITER = 4
D_SCAN_STEPS_BETWEEN_STAGES = 8
GATE_ROWS = 256
GATE_BLOCKS_PER_STEP = 4
EDGE_STEP_PIECES = 4
CONV_HALO = 8

CH_I = 0
CH_F = 8
CH_G = 16
CH_B = 32
CH_T = 48

VMEM_LIMIT = 56 * 1024 * 1024

_NT = (((1,), (1,)), ((), ()))


def _cparams(sem):
    return pltpu.CompilerParams(dimension_semantics=sem, vmem_limit_bytes=VMEM_LIMIT)


def _bdot(a, b):
    return jnp.dot(a.astype(BF16), b.astype(BF16), preferred_element_type=F32)


def _bdot_nt(a, b):
    return lax.dot_general(a.astype(BF16), b.astype(BF16), _NT, preferred_element_type=F32)


def _sigmoid(x):
    return 1.0 / (1.0 + jnp.exp2(x * -LOG2_E))


def _softplus(x):
    return jnp.maximum(x, 0.0) + jnp.log1p(jnp.exp(-jnp.abs(x)))


def _cast_blocks(w_refs, o_refs):
    for w_ref, o_ref in zip(w_refs, o_refs):
        o_ref[...] = w_ref[...].astype(o_ref.dtype)


def _cast_specs(weights, n_steps, step_of):
    specs, shapes = [], []
    for w in weights:
        rows, cols = w.shape
        assert rows % (n_steps * BF16_SUBLANES) == 0
        specs.append(pl.BlockSpec((rows // n_steps, cols), lambda *g: (step_of(*g), 0)))
        shapes.append(jax.ShapeDtypeStruct((rows, cols), BF16))
    return specs, shapes


def _lane_pick(x, ch):
    lane = lax.broadcasted_iota(I32, x.shape, 1)
    return jnp.sum(jnp.where(lane == ch, x, 0.0), axis=1, keepdims=True)


def _inproj_kernel(x_ref, g_ref, w_ref, ws_ref, o_ref, os_ref, hn_ref):
    j = pl.program_id(1)

    @pl.when(j == 0)
    def _():
        rows = x_ref.shape[0] // EDGE_STEP_PIECES
        for r in range(EDGE_STEP_PIECES):
            sl = slice(r * rows, (r + 1) * rows)
            x = x_ref[sl, :]
            ms = jnp.mean(x * x, axis=-1, keepdims=True)
            hn = (x * lax.rsqrt(ms + RMS_EPS) * g_ref[...]).astype(BF16)
            hn_ref[sl, :] = hn
            os_ref[sl, :] = lax.dot_general(hn, ws_ref[...], _NT, preferred_element_type=F32)
            o_ref[sl, :] = lax.dot_general(hn, w_ref[...], _NT, preferred_element_type=F32)

    @pl.when(j > 0)
    def _():
        o_ref[...] = lax.dot_general(hn_ref[...], w_ref[...], _NT, preferred_element_type=F32)


def _inproj(x2, g, wt, wt_small, seg_bounds, tm, tn):
    m, d = x2.shape
    starts = [0]
    for lo, hi in seg_bounds:
        starts.append(starts[-1] + (hi - lo) // tn)

    def w_row(i, j):
        row = seg_bounds[0][0] + j * tn
        for k in range(1, len(seg_bounds)):
            row = jnp.where(j >= starts[k], seg_bounds[k][0] + (j - starts[k]) * tn, row)
        return pl.multiple_of(row, BF16_SUBLANES), 0

    return pl.pallas_call(
        _inproj_kernel,
        grid=(m // tm, starts[-1]),
        in_specs=[
            pl.BlockSpec((tm, d), lambda i, j: (i, 0)),
            pl.BlockSpec((1, d), lambda i, j: (0, 0)),
            pl.BlockSpec((pl.Element(tn), pl.Element(d)), w_row),
            pl.BlockSpec((LANES, d), lambda i, j: (0, 0)),
        ],
        out_specs=[
            pl.BlockSpec((tm, tn), lambda i, j: (i, j)),
            pl.BlockSpec((tm, LANES), lambda i, j: (i, 0)),
        ],
        out_shape=[jax.ShapeDtypeStruct((m, starts[-1] * tn), F32), jax.ShapeDtypeStruct((m, LANES), F32)],
        scratch_shapes=[pltpu.VMEM((tm, d), BF16)],
        compiler_params=_cparams(("parallel", "arbitrary")),
        name="inproj",
    )(x2, g, wt, wt_small)


def _cumsum_matrices():
    rr, cc = np.indices((GATE_ROWS, GATE_ROWS))
    low, upp = cc <= rr, cc >= rr
    same_d = (rr // D_CHUNK) == (cc // D_CHUNK)
    same_m = (rr // M_CHUNK) == (cc // M_CHUNK)
    mats = np.concatenate([
        low & same_m,
        upp & same_m,
        low & same_d,
        upp & same_d,
        same_d,
    ], axis=0)
    return jnp.asarray(mats.astype(np.float32), dtype=BF16)


def _gate_kernel(sm_ref, par_ref, mats_ref, col_ref, row_ref):
    r = GATE_ROWS
    lane = lax.broadcasted_iota(I32, (r, LANES), 1)
    neg_a = -jnp.exp(par_ref[1:2, :])
    xes, x3s = [], []
    for blk in range(GATE_BLOCKS_PER_STEP):
        x = sm_ref[0, blk * r:(blk + 1) * r, :] + par_ref[0:1, :]
        logf = -_softplus(-x)
        g = neg_a * _softplus(x)
        beta = _sigmoid(x)
        xe = jnp.where(lane < CH_F, x,
                       jnp.where(lane < CH_G, logf,
                                 jnp.where(lane < CH_B, g,
                                           jnp.where(lane < CH_T, beta, 0.0))))
        hi = xe.astype(BF16)
        r1 = xe - hi.astype(F32)
        mid = r1.astype(BF16)
        lo = (r1 - mid.astype(F32)).astype(BF16)
        xes.append(xe)
        x3s.append(jnp.concatenate([hi, mid, lo], axis=1))
    y3s = [jnp.dot(mats_ref[...], x3, preferred_element_type=F32) for x3 in x3s]
    half_m = CH_F + M_HEADS
    half_d = CH_G + D_HEADS
    for blk, (xe, y3) in enumerate(zip(xes, y3s)):
        y = y3[:, 0:LANES] + y3[:, LANES:2 * LANES] + y3[:, 2 * LANES:3 * LANES]
        pm, sm_, pd, sd, td = (y[i * r:(i + 1) * r] for i in range(5))
        out = jnp.where(lane < CH_F, xe,
              jnp.where(lane < half_m, pm,
              jnp.where(lane < CH_G, sm_,
              jnp.where(lane < half_d, pd,
              jnp.where(lane < CH_B, sd,
              jnp.where(lane < CH_T, xe,
              jnp.where(lane < CH_T + N_DIR * D_HEADS, pltpu.roll(td, CH_T - CH_G, axis=1), 0.0)))))))
        col_ref[0, blk * r:(blk + 1) * r, :] = out
        row_ref[0, :, blk * r:(blk + 1) * r] = out.T


def _gate_prep(smalls3, params):
    b, s, _ = smalls3.shape
    r = GATE_ROWS * GATE_BLOCKS_PER_STEP
    assert s % r == 0
    return pl.pallas_call(
        _gate_kernel,
        grid=(b, s // r),
        in_specs=[
            pl.BlockSpec((1, r, LANES), lambda i, j: (i, j, 0)),
            pl.BlockSpec((8, LANES), lambda i, j: (0, 0)),
            pl.BlockSpec((5 * GATE_ROWS, GATE_ROWS), lambda i, j: (0, 0)),
        ],
        out_specs=[
            pl.BlockSpec((1, r, LANES), lambda i, j: (i, j, 0)),
            pl.BlockSpec((1, LANES, r), lambda i, j: (i, 0, j)),
        ],
        out_shape=[jax.ShapeDtypeStruct((b, s, LANES), F32), jax.ShapeDtypeStruct((b, LANES, s), F32)],
        compiler_params=_cparams(("parallel", "parallel")),
        name="gateprep",
    )(smalls3, params, _cumsum_matrices())


def _mlstm_kernel(n_cast, q_ref, k_ref, v_ref, o_ref, gc_ref, gr_ref, ng_ref, *refs):
    w_refs, y_ref, wo_refs = refs[:n_cast], refs[n_cast], refs[n_cast + 1:2 * n_cast + 1]
    hn_ref, den_ref, mi_ref, cu_ref, nu_ref, bl_ref, ml_ref, cs_ref, ns_ref, ms_ref, c_ref = refs[2 * n_cast + 1:]
    _cast_blocks(w_refs, wo_refs)
    head = pl.program_id(1)
    s = q_ref.shape[1]
    L = M_CHUNK
    nc = s // L
    scale = M_DK ** -0.5
    rr = lax.broadcasted_iota(I32, (L, L), 0)
    cc = lax.broadcasted_iota(I32, (L, L), 1)

    def bcast8(x11):
        return jnp.broadcast_to(x11, (8, LANES))

    def local(cs):
        chains = []
        for c in cs:
            r0 = pl.multiple_of(c * L, L)
            q = q_ref[0, pl.ds(r0, L), :]
            k = k_ref[0, pl.ds(r0, L), :] * scale
            vb = v_ref[0, pl.ds(r0, L), :].astype(BF16)
            kb = k.astype(BF16)
            qk = lax.dot_general(q.astype(BF16), kb, _NT, preferred_element_type=F32)
            k_t = k.T
            gcol = gc_ref[0, pl.ds(r0, L), :]
            for d in range(N_DIR):
                ch_i = CH_I + d * M_HEADS + head
                ch_f = CH_F + d * M_HEADS + head
                bc_col = _lane_pick(gcol, ch_f)
                i_row = gr_ref[0, pl.ds(ch_i, 1), pl.ds(r0, L)]
                bc_row = gr_ref[0, pl.ds(ch_f, 1), pl.ds(r0, L)]
                a_row = i_row - bc_row
                mask = (cc <= rr) if d == 0 else (cc >= rr)
                log_intra = jnp.where(mask, bc_col + a_row, -jnp.inf)
                m_intra = jnp.max(log_intra, axis=1, keepdims=True)
                p = qk * jnp.exp(log_intra - m_intra)
                b_last = bc_row[:, L - 1:L] if d == 0 else bc_row[:, 0:1]
                log_state = b_last + a_row
                m_loc = jnp.max(log_state, axis=1, keepdims=True)
                w_state = jnp.exp(log_state - m_loc)
                chains.append(dict(
                    idx=d * nc + c, d=d, r0=r0, vb=vb, kb=kb, pb=p.astype(BF16),
                    den=jnp.sum(p, axis=1, keepdims=True), m_intra=m_intra,
                    kw=(k_t * w_state).astype(BF16), w8=jnp.broadcast_to(w_state, (8, L)).astype(BF16),
                    b_last=b_last, m_loc=m_loc))
        for c in chains:
            hn_ref[c["d"], pl.ds(c["r0"], L), :] = jnp.dot(c["pb"], c["vb"], preferred_element_type=F32)
            den_ref[c["d"], pl.ds(c["r0"], L), :] = jnp.broadcast_to(c["den"], (L, LANES))
            mi_ref[c["d"], pl.ds(c["r0"], L), :] = jnp.broadcast_to(c["m_intra"], (L, LANES))
        for c in chains:
            cu_ref[c["idx"]] = jnp.dot(c["kw"], c["vb"], preferred_element_type=F32)
            nu_ref[c["idx"]] = jnp.dot(c["w8"], c["kb"], preferred_element_type=F32)
            bl_ref[c["idx"]] = bcast8(c["b_last"])
            ml_ref[c["idx"]] = bcast8(c["m_loc"])

    def local_trip(i, _):
        local([M_CHUNKS_PER_ITER * i + t for t in range(M_CHUNKS_PER_ITER)])
        return 0

    lax.fori_loop(0, nc // M_CHUNKS_PER_ITER, local_trip, 0)

    c_ref[...] = jnp.zeros(c_ref.shape, F32)

    def scan(j, carry):
        out = []
        for d in range(N_DIR):
            m_st, n_st = carry[d]
            idx = d * nc + (j if d == 0 else nc - 1 - j)
            c_st = c_ref[d]
            cs_ref[idx] = c_st.astype(BF16)
            ns_ref[idx] = n_st
            ms_ref[idx] = m_st
            bl = bl_ref[idx]
            ml = ml_ref[idx]
            m_new = jnp.maximum(bl + m_st, ml)
            decay = jnp.exp(bl + m_st - m_new)
            gain = jnp.exp(ml - m_new)
            c_ref[d] = decay[0:1, 0:1] * c_st + gain[0:1, 0:1] * cu_ref[idx]
            out.append((m_new, decay * n_st + gain * nu_ref[idx]))
        return tuple(out)

    m0 = jnp.full((8, LANES), -jnp.inf, F32)
    n0 = jnp.zeros((8, M_DK), F32)
    lax.fori_loop(0, nc, scan, ((m0, n0), (m0, n0)))

    def wide(x):
        return jnp.concatenate([x] * (M_DV // LANES), axis=1)

    def combine(c, _):
        r0 = pl.multiple_of(c * L, L)
        q = q_ref[0, pl.ds(r0, L), :]
        qb = q.astype(BF16)
        gcol = gc_ref[0, pl.ds(r0, L), :]
        hh = None
        for d in range(N_DIR):
            idx = d * nc + c
            bc = jnp.broadcast_to(_lane_pick(gcol, CH_F + d * M_HEADS + head), (L, LANES))
            den_i = den_ref[d, pl.ds(r0, L), :]
            m_i = mi_ref[d, pl.ds(r0, L), :]
            log_inter = bc + ms_ref[idx][0:1, :]
            m_row = jnp.maximum(log_inter, m_i)
            w_i = jnp.exp(m_i - m_row)
            w_x = jnp.exp(log_inter - m_row)
            q_c = jnp.dot(qb, cs_ref[idx], preferred_element_type=F32)
            q_n = jnp.broadcast_to(jnp.sum(q * ns_ref[idx][0:1, :], axis=1, keepdims=True), (L, LANES))
            den = w_i * den_i + w_x * q_n
            inv = 1.0 / jnp.maximum(jnp.abs(den), jnp.exp(-m_row))
            h_d = wide(w_i * inv) * hn_ref[d, pl.ds(r0, L), :] + wide(w_x * inv) * q_c
            hh = h_d if hh is None else hh + h_d
        hh = hh * lax.rsqrt(jnp.mean(hh * hh, axis=-1, keepdims=True) + RMS_EPS)
        hh = hh * ng_ref[0]
        y_ref[0, pl.ds(r0, L), :] = (_sigmoid(o_ref[0, pl.ds(r0, L), :]) * hh).astype(y_ref.dtype)
        return 0

    lax.fori_loop(0, nc, combine, 0, unroll=2)


def _mlstm(proj3, gcol, grow, norm_g, off_q, off_k, off_v, off_o, cast_weights):
    b, s, _ = proj3.shape
    w_specs, w_shapes = _cast_specs(cast_weights, b * M_HEADS, lambda i, h: i * M_HEADS + h)
    assert s % (M_CHUNK * M_CHUNKS_PER_ITER) == 0
    nc = s // M_CHUNK
    bq, bk = off_q // M_DK, off_k // M_DK
    bv, bo = off_v // M_DV, off_o // M_DV
    return pl.pallas_call(
        functools.partial(_mlstm_kernel, len(cast_weights)),
        grid=(b, M_HEADS),
        in_specs=[
            pl.BlockSpec((1, s, M_DK), lambda i, h: (i, 0, bq + h)),
            pl.BlockSpec((1, s, M_DK), lambda i, h: (i, 0, bk + h)),
            pl.BlockSpec((1, s, M_DV), lambda i, h: (i, 0, bv + h)),
            pl.BlockSpec((1, s, M_DV), lambda i, h: (i, 0, bo + h)),
            pl.BlockSpec((1, s, LANES), lambda i, h: (i, 0, 0)),
            pl.BlockSpec((1, LANES, s), lambda i, h: (i, 0, 0)),
            pl.BlockSpec((1, 1, M_DV), lambda i, h: (h, 0, 0)),
            *w_specs,
        ],
        out_specs=[pl.BlockSpec((1, s, M_DV), lambda i, h: (i, 0, h)), *w_specs],
        out_shape=[jax.ShapeDtypeStruct((b, s, M_HEADS * M_DV), BF16), *w_shapes],
        scratch_shapes=[
            pltpu.VMEM((N_DIR, s, M_DV), F32),
            pltpu.VMEM((N_DIR, s, LANES), F32),
            pltpu.VMEM((N_DIR, s, LANES), F32),
            pltpu.VMEM((N_DIR * nc, M_DK, M_DV), F32),
            pltpu.VMEM((N_DIR * nc, 8, M_DK), F32),
            pltpu.VMEM((N_DIR * nc, 8, LANES), F32),
            pltpu.VMEM((N_DIR * nc, 8, LANES), F32),
            pltpu.VMEM((N_DIR * nc, M_DK, M_DV), BF16),
            pltpu.VMEM((N_DIR * nc, 8, M_DK), F32),
            pltpu.VMEM((N_DIR * nc, 8, LANES), F32),
            pltpu.VMEM((N_DIR, M_DK, M_DV), F32),
        ],
        compiler_params=_cparams(("parallel", "arbitrary")),
        name="mlstm",
    )(proj3, proj3, proj3, proj3, gcol, grow, norm_g, *cast_weights)


def _delta_kernel(n_cast, q_ref, k_ref, v_ref, z_ref, cwq_ref, cwk_ref, cwv_ref, gc_ref, gr_ref, ng_ref, *refs):
    w_refs, y_ref, wo_refs = refs[:n_cast], refs[n_cast], refs[n_cast + 1:2 * n_cast + 1]
    xp_ref, qs_ref, ks_ref, vs_ref, kq_ref, n_ref, dec_ref, o_ref, st_ref = refs[2 * n_cast + 1:]
    _cast_blocks(w_refs, wo_refs)
    t = pl.program_id(0)
    head = jnp.minimum(t, pl.num_programs(0) - 2) % D_HEADS
    s = q_ref.shape[1]
    L = D_CHUNK
    G = D_GROUP
    nc = s // L
    ng = s // G
    halo = CONV_HALO
    pad = CONV_W // 2

    @pl.when(t == 0)
    def _():
        for r in (kq_ref, n_ref, dec_ref, o_ref):
            r[...] = jnp.zeros(r.shape, r.dtype)

    xp_ref[0:halo, :] = jnp.zeros((halo, D_DIM), F32)
    xp_ref[s + halo:s + 2 * halo, :] = jnp.zeros((halo, D_DIM), F32)

    def conv_into(x_ref, cw_ref, dst_ref, l2, mult):
        xp_ref[halo:s + halo, :] = x_ref[0]
        for blk in range(ng):
            base = halo + blk * G - pad
            acc = xp_ref[base:base + G, :] * cw_ref[0:1, :]
            for j in range(1, CONV_W):
                acc = acc + xp_ref[base + j:base + j + G, :] * cw_ref[j:j + 1, :]
            y = acc * _sigmoid(acc)
            if l2:
                y = y * (lax.rsqrt(jnp.sum(y * y, axis=-1, keepdims=True) + L2_EPS) * mult)
            dst_ref[blk * G:(blk + 1) * G, :] = y

    conv_into(q_ref, cwq_ref, qs_ref, True, D_DIM ** -0.5)
    conv_into(k_ref, cwk_ref, ks_ref, True, 1.0)
    conv_into(v_ref, cwv_ref, vs_ref, False, 1.0)

    st_ref[...] = jnp.zeros(st_ref.shape, F32)

    def scan_step(scan, j):
        for d in range(N_DIR):
            c = j if d == 0 else nc - 1 - j
            r0 = pl.multiple_of(c * L, L)
            idx = (scan + d) * nc + c
            st = st_ref[d]
            res = jnp.dot(kq_ref[idx], st.astype(BF16), preferred_element_type=F32)
            st_ref[d] = dec_ref[idx][0:1, :] * st + res[0:D_DIM] + n_ref[idx]
            o_ref[scan + d, pl.ds(r0, L), :] = o_ref[scan + d, pl.ds(r0, L), :] + res[D_DIM:D_DIM + L]

    tchunk = lax.broadcasted_iota(I32, (D_DIM, D_GROUP), 1) // D_CHUNK
    rr = lax.broadcasted_iota(I32, (G, G), 0)
    cc = lax.broadcasted_iota(I32, (G, G), 1)
    same = (rr // L) == (cc // L)
    pr = lax.broadcasted_iota(I32, (L, G), 0)
    pc = lax.broadcasted_iota(I32, (L, G), 1)
    eye_p = jnp.where((pc % L) == pr, 1.0, 0.0).astype(F32)

    def pack(m):
        out = m[0:L]
        for i in range(1, D_PER_GROUP):
            out = out + m[i * L:(i + 1) * L]
        return out

    def unpack(p):
        return jnp.where(same, jnp.concatenate([p] * D_PER_GROUP, axis=0), 0.0)

    scan_per_trip = nc // (ng // D_GROUPS_PER_ITER)

    def groups(fill, scan, trip):
        pending = iter(range(scan_per_trip))

        def scan_some(n):
            for _ in range(n):
                j = next(pending, None)
                if j is not None:
                    scan_step(scan, trip * scan_per_trip + j)

        chains = []
        for gi in [D_GROUPS_PER_ITER * trip + g for g in range(D_GROUPS_PER_ITER)]:
            r0 = pl.multiple_of(gi * G, G)
            kk_ = ks_ref[pl.ds(r0, G), :]
            qq_ = qs_ref[pl.ds(r0, G), :]
            vv_ = vs_ref[pl.ds(r0, G), :]
            kb = kk_.astype(BF16)
            kk = lax.dot_general(kb, kb, _NT, preferred_element_type=F32)
            qk = lax.dot_general(qq_.astype(BF16), kb, _NT, preferred_element_type=F32)
            gcol = gc_ref[0, pl.ds(r0, G), :]
            for d in range(N_DIR):
                ch = d * D_HEADS + head
                g_col = _lane_pick(gcol, CH_G + ch)
                b_col = _lane_pick(gcol, CH_B + ch)
                t_col = _lane_pick(gcol, CH_T + ch)
                g_row = gr_ref[0, pl.ds(CH_G + ch, 1), pl.ds(r0, G)]
                tri = (cc <= rr) if d == 0 else (cc >= rr)
                strict = (cc < rr) if d == 0 else (cc > rr)
                gam = jnp.exp(jnp.where(tri, jnp.where(same, g_col - g_row, -jnp.inf), -jnp.inf))
                x = jnp.where(strict, -(b_col * kk * gam), 0.0)
                eg = jnp.exp(g_col)
                chains.append(dict(
                    gi=gi, slot=fill + d, r0=r0, x=x, xp=pack(x), attn=(qk * gam).astype(BF16),
                    rhs=jnp.concatenate([b_col * vv_, (b_col * eg) * kk_], axis=1).astype(BF16),
                    qg=qq_ * eg, kdt=(kk_ * jnp.exp(t_col - g_col)).T, dec=jnp.exp(t_col)))
            scan_some(-(-(scan_per_trip - D_SCAN_STEPS_BETWEEN_STAGES) // D_GROUPS_PER_ITER))
        for c in chains:
            c["tp"] = eye_p + c["xp"]
            c["p"] = _bdot(c["xp"], c["x"])
        scan_some(1)
        for _it in range(4):
            for c in chains:
                res = _bdot(jnp.concatenate([c["tp"], c["p"]], axis=0), unpack(c["p"]))
                c["tp"] = c["tp"] + res[0:L]
                c["p"] = res[L:2 * L]
            scan_some(1)
        for c in chains:
            c["tp"] = c["tp"] + _bdot(c["tp"], unpack(c["p"]))
        scan_some(1)
        for c in chains:
            c["uw"] = _bdot(unpack(c["tp"]), c["rhs"]).astype(BF16)
        scan_some(1)
        for c in chains:
            au = jnp.dot(c["attn"], c["uw"], preferred_element_type=F32)
            o_ref[c["slot"], pl.ds(c["r0"], G), :] = au[:, 0:D_DIM]
            c["qe"] = (c["qg"] - au[:, D_DIM:2 * D_DIM]).astype(BF16)
        scan_some(1)
        for c in chains:
            for ci in range(D_PER_GROUP):
                idx = c["slot"] * nc + c["gi"] * D_PER_GROUP + ci
                ku = _bdot(jnp.where(tchunk == ci, c["kdt"], 0.0), c["uw"])
                n_ref[idx] = ku[:, 0:D_DIM]
                kq_ref[idx, 0:D_DIM, :] = (-ku[:, D_DIM:2 * D_DIM]).astype(BF16)
                kq_ref[idx, D_DIM:D_DIM + L, :] = c["qe"][ci * L:(ci + 1) * L]
                dec_ref[idx] = jnp.broadcast_to(c["dec"][ci * L:ci * L + 8], (8, D_DIM))
        scan_some(scan_per_trip)

    def finish(scan, gi):
        r0 = pl.multiple_of(gi * G, G)
        o = o_ref[scan, pl.ds(r0, G), :] + o_ref[scan + 1, pl.ds(r0, G), :]
        o = o * lax.rsqrt(jnp.mean(o * o, axis=-1, keepdims=True) + RMS_EPS) * ng_ref[...]
        z = z_ref[0, pl.ds(r0, G), :]
        y_ref[0, pl.ds(r0, G), :] = (o * (z * _sigmoid(z))).astype(y_ref.dtype)

    fill = (t % 2) * N_DIR
    scan = N_DIR - fill

    def group_batch(i, carry):
        groups(fill, scan, i)
        return carry

    def finish_batch(gi, carry):
        finish(scan, gi)
        return carry

    lax.fori_loop(0, ng // D_GROUPS_PER_ITER, group_batch, 0)
    lax.fori_loop(0, ng, finish_batch, 0, unroll=4)


def _delta(proj3, conv_w, gcol, grow, norm_g, off_q, off_k, off_v, off_z, cast_weights):
    b, s, _ = proj3.shape
    w = D_HEADS * D_DIM
    nc = s // D_CHUNK
    trips = s // (D_GROUP * D_GROUPS_PER_ITER)
    assert s % (D_GROUP * D_GROUPS_PER_ITER) == 0 and nc % trips == 0
    bq, bk, bv, bz = (o // D_DIM for o in (off_q, off_k, off_v, off_z))
    n_units = b * D_HEADS
    cur = lambda t: jnp.minimum(t, n_units - 1)
    prev = lambda t: jnp.maximum(t - 1, 0)
    col = lambda unit, blk: lambda t: (unit(t) // D_HEADS, 0, blk + unit(t) % D_HEADS)
    w_specs, w_shapes = _cast_specs(cast_weights, n_units, cur)
    return pl.pallas_call(
        functools.partial(_delta_kernel, len(cast_weights)),
        grid=(n_units + 1,),
        in_specs=[
            pl.BlockSpec((1, s, D_DIM), col(cur, bq)),
            pl.BlockSpec((1, s, D_DIM), col(cur, bk)),
            pl.BlockSpec((1, s, D_DIM), col(cur, bv)),
            pl.BlockSpec((1, s, D_DIM), col(prev, bz)),
            pl.BlockSpec((CONV_W, D_DIM), lambda t: (0, cur(t) % D_HEADS)),
            pl.BlockSpec((CONV_W, D_DIM), lambda t: (0, D_HEADS + cur(t) % D_HEADS)),
            pl.BlockSpec((CONV_W, D_DIM), lambda t: (0, 2 * D_HEADS + cur(t) % D_HEADS)),
            pl.BlockSpec((1, s, LANES), lambda t: (cur(t) // D_HEADS, 0, 0)),
            pl.BlockSpec((1, LANES, s), lambda t: (cur(t) // D_HEADS, 0, 0)),
            pl.BlockSpec((1, D_DIM), lambda t: (0, 0)),
            *w_specs,
        ],
        out_specs=[pl.BlockSpec((1, s, D_DIM), col(prev, 0)), *w_specs],
        out_shape=[jax.ShapeDtypeStruct((b, s, w), BF16), *w_shapes],
        scratch_shapes=[
            pltpu.VMEM((s + 2 * CONV_HALO, D_DIM), F32),
            pltpu.VMEM((s, D_DIM), F32),
            pltpu.VMEM((s, D_DIM), F32),
            pltpu.VMEM((s, D_DIM), F32),
            pltpu.VMEM((2 * N_DIR * nc, D_DIM + D_CHUNK, D_DIM), BF16),
            pltpu.VMEM((2 * N_DIR * nc, D_DIM, D_DIM), F32),
            pltpu.VMEM((2 * N_DIR * nc, 8, D_DIM), F32),
            pltpu.VMEM((2 * N_DIR, s, D_DIM), F32),
            pltpu.VMEM((N_DIR, D_DIM, D_DIM), F32),
        ],
        compiler_params=_cparams(("arbitrary",)),
        name="delta",
    )(proj3, proj3, proj3, proj3, conv_w, conv_w, conv_w, gcol, grow, norm_g, *cast_weights)


def _mix_kernel(n_gate_blocks, x_ref, ym_ref, yd_ref, *refs):
    gate_refs, (wbm_ref, wbd_ref, wo_ref, g2_ref, o_ref, h_ref) = refs[:2 * n_gate_blocks], refs[2 * n_gate_blocks:]
    gate = lambda rs: _sigmoid(jnp.concatenate([r[...] for r in rs], axis=1))
    a = jnp.dot(ym_ref[...], wbm_ref[...], preferred_element_type=F32)
    b = jnp.dot(yd_ref[...], wbd_ref[...], preferred_element_type=F32)
    mixed = gate(gate_refs[:n_gate_blocks]) * a + gate(gate_refs[n_gate_blocks:]) * b
    x = x_ref[...] + jnp.dot(mixed.astype(BF16), wo_ref[...], preferred_element_type=F32)
    o_ref[...] = x
    ms = jnp.mean(x * x, axis=-1, keepdims=True)
    h_ref[...] = (x * lax.rsqrt(ms + RMS_EPS) * g2_ref[...]).astype(BF16)


def _mix(x2, ym2, yd2, proj2, off_gates, wbm, wbd, wo, g2, tm):
    m, d = x2.shape
    const = lambda i: (0, 0)
    gw = math.gcd(off_gates, d)
    gate_specs = [pl.BlockSpec((tm, gw), functools.partial(lambda i, c: (i, c), c=(off_gates + t * gw) // gw))
                  for t in range(2 * d // gw)]
    return pl.pallas_call(
        functools.partial(_mix_kernel, d // gw),
        grid=(m // tm,),
        in_specs=[
            pl.BlockSpec((tm, d), lambda i: (i, 0)),
            pl.BlockSpec((tm, ym2.shape[1]), lambda i: (i, 0)),
            pl.BlockSpec((tm, yd2.shape[1]), lambda i: (i, 0)),
            *gate_specs,
            pl.BlockSpec(wbm.shape, const),
            pl.BlockSpec(wbd.shape, const),
            pl.BlockSpec(wo.shape, const),
            pl.BlockSpec((1, d), const),
        ],
        out_specs=[pl.BlockSpec((tm, d), lambda i: (i, 0)), pl.BlockSpec((tm, d), lambda i: (i, 0))],
        out_shape=[jax.ShapeDtypeStruct((m, d), F32), jax.ShapeDtypeStruct((m, d), BF16)],
        compiler_params=_cparams(("parallel",)),
        name="mix",
    )(x2, ym2, yd2, *([proj2] * len(gate_specs)), wbm, wbd, wo, g2)


def _ffn_kernel(n_slices, x_ref, h_ref, gf_ref, w1_ref, w2_ref, o_ref, acc_ref):
    f = pl.program_id(1)
    last = n_slices - 1

    def ffn_part(sl):
        hid = jnp.dot(h_ref[sl, :], w1_ref[...], preferred_element_type=F32)
        act = jnp.square(jnp.maximum(hid, 0.0)).astype(BF16)
        return jnp.dot(act, w2_ref[...], preferred_element_type=F32)

    @pl.when(jnp.logical_and(f == 0, f < last))
    def _():
        acc_ref[...] = ffn_part(slice(None))

    @pl.when(jnp.logical_and(f > 0, f < last))
    def _():
        acc_ref[...] += ffn_part(slice(None))

    @pl.when(f == last)
    def _():
        rows = x_ref.shape[0] // EDGE_STEP_PIECES
        for r in range(EDGE_STEP_PIECES):
            sl = slice(r * rows, (r + 1) * rows)
            y = ffn_part(sl) if last == 0 else acc_ref[sl, :] + ffn_part(sl)
            x = x_ref[sl, :] + y
            ms = jnp.mean(x * x, axis=-1, keepdims=True)
            o_ref[sl, :] = x * lax.rsqrt(ms + RMS_EPS) * gf_ref[...]


def _ffn(x2, h2, gf, w1, w2, tm, tf):
    m, d = x2.shape
    dff = w1.shape[1]
    return pl.pallas_call(
        functools.partial(_ffn_kernel, dff // tf),
        grid=(m // tm, dff // tf),
        in_specs=[
            pl.BlockSpec((tm, d), lambda i, f: (i, 0)),
            pl.BlockSpec((tm, d), lambda i, f: (i, 0)),
            pl.BlockSpec((1, d), lambda i, f: (0, 0)),
            pl.BlockSpec((d, tf), lambda i, f: (0, f)),
            pl.BlockSpec((tf, d), lambda i, f: (f, 0)),
        ],
        out_specs=pl.BlockSpec((tm, d), lambda i, f: (i, 0)),
        out_shape=jax.ShapeDtypeStruct((m, d), F32),
        scratch_shapes=[pltpu.VMEM((tm, d), F32)],
        compiler_params=_cparams(("parallel", "arbitrary")),
        name="ffn",
    )(x2, h2, gf, w1, w2)


def _pick_tile(n, candidates):
    for c in candidates:
        if n % c == 0:
            return c
    raise ValueError(f"no tile in {candidates} divides {n}")


def _layer(x, norm1_g, w_in, i_bias, f_bias, m_norm_g, conv_w, a_log, dt_bias, d_norm_g,
           w_bm, w_bd, w_out, norm2_g, w_ff1, w_ff2, out_g):
    b, s, d = x.shape
    m = b * s
    mw, mqk, dw = M_HEADS * M_DV, M_HEADS * M_DK, D_HEADS * D_DIM
    nm, nd = N_DIR * M_HEADS, N_DIR * D_HEADS
    splits = (mqk, mqk, mw, mw, nm, nm, 3 * dw, dw, nd, nd, 2 * d)
    bounds = [0]
    for w_ in splits:
        bounds.append(bounds[-1] + w_)
    wt = w_in.T.astype(BF16)
    rows = lambda first, last: wt[bounds[first]:bounds[last + 1]]
    seg_bounds = [(bounds[0], bounds[4]), (bounds[6], bounds[8]), (bounds[10], bounds[11])]
    n_small = 2 * nm + 2 * nd
    wt_small = jnp.concatenate([rows(4, 5), rows(8, 9), jnp.zeros((LANES - n_small, d), BF16)], axis=0)
    zpad = lambda n: jnp.zeros((n,), F32)
    bias_row = jnp.concatenate([i_bias.reshape(-1), f_bias.reshape(-1), dt_bias.reshape(-1), zpad(LANES - CH_B)])
    alog_row = jnp.concatenate([zpad(CH_G), a_log.reshape(-1), zpad(LANES - CH_B)])
    gate_params = jnp.concatenate([bias_row[None], alog_row[None], jnp.zeros((6, LANES), F32)], axis=0)

    x2 = x.reshape(m, d)
    tm = _pick_tile(m, (512, 256, 128))
    tn = next(c for c in (1024, 512, 256) if all((hi - lo) % c == 0 for lo, hi in seg_bounds))
    proj2, smalls2 = _inproj(x2, norm1_g.reshape(1, d), wt, wt_small, seg_bounds,
                             _pick_tile(m, (1024, 512, 256)), tn)
    proj3 = proj2.reshape(b, s, -1)
    gcol, grow = _gate_prep(smalls2.reshape(b, s, LANES), gate_params)

    off_d = 2 * mqk + 2 * mw
    off_g = off_d + 4 * dw
    ym, w_bm16, w_bd16, w_out16 = _mlstm(proj3, gcol, grow, m_norm_g.reshape(M_HEADS, 1, M_DV),
                                         0, mqk, 2 * mqk, 2 * mqk + mw, (w_bm, w_bd, w_out))
    yd, w_ff1_16, w_ff2_16 = _delta(proj3, conv_w, gcol, grow, d_norm_g.reshape(1, D_DIM),
                                    off_d, off_d + dw, off_d + 2 * dw, off_d + 3 * dw, (w_ff1, w_ff2))

    x1, h2 = _mix(x2, ym.reshape(m, mw), yd.reshape(m, dw), proj2, off_g, w_bm16, w_bd16, w_out16,
                  norm2_g.reshape(1, d), _pick_tile(m, (256, 128)))
    tf = _pick_tile(w_ff1.shape[1], (1024, 512, 256))
    out = _ffn(x1, h2, out_g.reshape(1, d), w_ff1_16, w_ff2_16, tm, tf)
    return out.reshape(b, s, d)


def kernel(x, norm1_g, w_in, mlstm_i_bias, mlstm_f_bias, mlstm_norm_g, delta_conv_w, delta_a_log,
           delta_dt_bias, delta_norm_g, w_branch_m, w_branch_d, w_out, norm2_g, w_ff1, w_ff2, norm_f_g):
    depth = w_in.shape[0]
    assert depth == 1, "the fused FFN + final-norm epilogue assumes a single layer"
    return _layer(x, norm1_g[0], w_in[0], mlstm_i_bias[0], mlstm_f_bias[0], mlstm_norm_g[0], delta_conv_w[0],
                  delta_a_log[0], delta_dt_bias[0], delta_norm_g[0], w_branch_m[0], w_branch_d[0], w_out[0],
                  norm2_g[0], w_ff1[0], w_ff2[0], norm_f_g)
```

```python
import functools
import math

import jax
import jax.numpy as jnp
import numpy as np
from jax import lax
from jax.experimental import pallas as pl
from jax.experimental.pallas import tpu as pltpu

F32 = jnp.float32
BF16 = jnp.bfloat16
I32 = jnp.int32

N_DIR = 2
M_HEADS = 4
M_DK = 128
M_DV = 256
D_HEADS = 8
D_DIM = 128
CONV_W = 5
RMS_EPS = 1e-6
L2_EPS = 1e-6
LOG2_E = math.log2(math.e)

LANES = 128
BF16_SUBLANES = 16
M_CHUNK = 256
M_CHUNKS_PER_ITER = 4
D_CHUNK = 64
D_GROUP = 256
D_PER_GROUP = D_GROUP // D_CHUNK
D_GROUPS_PER_ITER = 4
D_SCAN_STEPS_BETWEEN_STAGES = 8
GATE_ROWS = 256
GATE_BLOCKS_PER_STEP = 4
EDGE_STEP_PIECES = 4
CONV_HALO = 8

CH_I = 0
CH_F = 8
CH_G = 16
CH_B = 32
CH_T = 48

VMEM_LIMIT = 56 * 1024 * 1024

_NT = (((1,), (1,)), ((), ()))


def _cparams(sem):
    return pltpu.CompilerParams(dimension_semantics=sem, vmem_limit_bytes=VMEM_LIMIT)


def _bdot(a, b):
    return jnp.dot(a.astype(BF16), b.astype(BF16), preferred_element_type=F32)


def _bdot_nt(a, b):
    return lax.dot_general(a.astype(BF16), b.astype(BF16), _NT, preferred_element_type=F32)


def _sigmoid(x):
    return 1.0 / (1.0 + jnp.exp2(x * -LOG2_E))


def _softplus(x):
    return jnp.maximum(x, 0.0) + jnp.log1p(jnp.exp(-jnp.abs(x)))


def _cast_blocks(w_refs, o_refs):
    for w_ref, o_ref in zip(w_refs, o_refs):
        o_ref[...] = w_ref[...].astype(o_ref.dtype)


def _cast_specs(weights, n_steps, step_of):
    specs, shapes = [], []
    for w in weights:
        rows, cols = w.shape
        assert rows % (n_steps * BF16_SUBLANES) == 0
        specs.append(pl.BlockSpec((rows // n_steps, cols), lambda *g: (step_of(*g), 0)))
        shapes.append(jax.ShapeDtypeStruct((rows, cols), BF16))
    return specs, shapes


def _lane_pick(x, ch):
    lane = lax.broadcasted_iota(I32, x.shape, 1)
    return jnp.sum(jnp.where(lane == ch, x, 0.0), axis=1, keepdims=True)


def _inproj_kernel(x_ref, g_ref, w_ref, ws_ref, o_ref, os_ref, hn_ref):
    j = pl.program_id(1)

    @pl.when(j == 0)
    def _():
        rows = x_ref.shape[0] // EDGE_STEP_PIECES
        for r in range(EDGE_STEP_PIECES):
            sl = slice(r * rows, (r + 1) * rows)
            x = x_ref[sl, :]
            ms = jnp.mean(x * x, axis=-1, keepdims=True)
            hn = (x * lax.rsqrt(ms + RMS_EPS) * g_ref[...]).astype(BF16)
            hn_ref[sl, :] = hn
            os_ref[sl, :] = lax.dot_general(hn, ws_ref[...], _NT, preferred_element_type=F32)
            o_ref[sl, :] = lax.dot_general(hn, w_ref[...], _NT, preferred_element_type=F32)

    @pl.when(j > 0)
    def _():
        o_ref[...] = lax.dot_general(hn_ref[...], w_ref[...], _NT, preferred_element_type=F32)


def _inproj(x2, g, wt, wt_small, seg_bounds, tm, tn):
    m, d = x2.shape
    starts = [0]
    for lo, hi in seg_bounds:
        starts.append(starts[-1] + (hi - lo) // tn)

    def w_row(i, j):
        row = seg_bounds[0][0] + j * tn
        for k in range(1, len(seg_bounds)):
            row = jnp.where(j >= starts[k], seg_bounds[k][0] + (j - starts[k]) * tn, row)
        return pl.multiple_of(row, BF16_SUBLANES), 0

    return pl.pallas_call(
        _inproj_kernel,
        grid=(m // tm, starts[-1]),
        in_specs=[
            pl.BlockSpec((tm, d), lambda i, j: (i, 0)),
            pl.BlockSpec((1, d), lambda i, j: (0, 0)),
            pl.BlockSpec((pl.Element(tn), pl.Element(d)), w_row),
            pl.BlockSpec((LANES, d), lambda i, j: (0, 0)),
        ],
        out_specs=[
            pl.BlockSpec((tm, tn), lambda i, j: (i, j)),
            pl.BlockSpec((tm, LANES), lambda i, j: (i, 0)),
        ],
        out_shape=[jax.ShapeDtypeStruct((m, starts[-1] * tn), F32), jax.ShapeDtypeStruct((m, LANES), F32)],
        scratch_shapes=[pltpu.VMEM((tm, d), BF16)],
        compiler_params=_cparams(("parallel", "arbitrary")),
        name="inproj",
    )(x2, g, wt, wt_small)


def _cumsum_matrices():
    rr, cc = np.indices((GATE_ROWS, GATE_ROWS))
    low, upp = cc <= rr, cc >= rr
    same_d = (rr // D_CHUNK) == (cc // D_CHUNK)
    same_m = (rr // M_CHUNK) == (cc // M_CHUNK)
    mats = np.concatenate([
        low & same_m,
        upp & same_m,
        low & same_d,
        upp & same_d,
        same_d,
    ], axis=0)
    return jnp.asarray(mats.astype(np.float32), dtype=BF16)


def _gate_kernel(sm_ref, par_ref, mats_ref, col_ref, row_ref):
    r = GATE_ROWS
    lane = lax.broadcasted_iota(I32, (r, LANES), 1)
    neg_a = -jnp.exp(par_ref[1:2, :])
    xes, x3s = [], []
    for blk in range(GATE_BLOCKS_PER_STEP):
        x = sm_ref[0, blk * r:(blk + 1) * r, :] + par_ref[0:1, :]
        logf = -_softplus(-x)
        g = neg_a * _softplus(x)
        beta = _sigmoid(x)
        xe = jnp.where(lane < CH_F, x,
                       jnp.where(lane < CH_G, logf,
                                 jnp.where(lane < CH_B, g,
                                           jnp.where(lane < CH_T, beta, 0.0))))
        hi = xe.astype(BF16)
        r1 = xe - hi.astype(F32)
        mid = r1.astype(BF16)
        lo = (r1 - mid.astype(F32)).astype(BF16)
        xes.append(xe)
        x3s.append(jnp.concatenate([hi, mid, lo], axis=1))
    y3s = [jnp.dot(mats_ref[...], x3, preferred_element_type=F32) for x3 in x3s]
    half_m = CH_F + M_HEADS
    half_d = CH_G + D_HEADS
    for blk, (xe, y3) in enumerate(zip(xes, y3s)):
        y = y3[:, 0:LANES] + y3[:, LANES:2 * LANES] + y3[:, 2 * LANES:3 * LANES]
        pm, sm_, pd, sd, td = (y[i * r:(i + 1) * r] for i in range(5))
        out = jnp.where(lane < CH_F, xe,
              jnp.where(lane < half_m, pm,
              jnp.where(lane < CH_G, sm_,
              jnp.where(lane < half_d, pd,
              jnp.where(lane < CH_B, sd,
              jnp.where(lane < CH_T, xe,
              jnp.where(lane < CH_T + N_DIR * D_HEADS, pltpu.roll(td, CH_T - CH_G, axis=1), 0.0)))))))
        col_ref[0, blk * r:(blk + 1) * r, :] = out
        row_ref[0, :, blk * r:(blk + 1) * r] = out.T


def _gate_prep(smalls3, params):
    b, s, _ = smalls3.shape
    r = GATE_ROWS * GATE_BLOCKS_PER_STEP
    assert s % r == 0
    return pl.pallas_call(
        _gate_kernel,
        grid=(b, s // r),
        in_specs=[
            pl.BlockSpec((1, r, LANES), lambda i, j: (i, j, 0)),
            pl.BlockSpec((8, LANES), lambda i, j: (0, 0)),
            pl.BlockSpec((5 * GATE_ROWS, GATE_ROWS), lambda i, j: (0, 0)),
        ],
        out_specs=[
            pl.BlockSpec((1, r, LANES), lambda i, j: (i, j, 0)),
            pl.BlockSpec((1, LANES, r), lambda i, j: (i, 0, j)),
        ],
        out_shape=[jax.ShapeDtypeStruct((b, s, LANES), F32), jax.ShapeDtypeStruct((b, LANES, s), F32)],
        compiler_params=_cparams(("parallel", "parallel")),
        name="gateprep",
    )(smalls3, params, _cumsum_matrices())


def _mlstm_kernel(n_cast, q_ref, k_ref, v_ref, o_ref, gc_ref, gr_ref, ng_ref, *refs):
    w_refs, y_ref, wo_refs = refs[:n_cast], refs[n_cast], refs[n_cast + 1:2 * n_cast + 1]
    hn_ref, den_ref, mi_ref, cu_ref, nu_ref, bl_ref, ml_ref, cs_ref, ns_ref, ms_ref, c_ref = refs[2 * n_cast + 1:]
    _cast_blocks(w_refs, wo_refs)
    head = pl.program_id(1)
    s = q_ref.shape[1]
    L = M_CHUNK
    nc = s // L
    scale = M_DK ** -0.5
    rr = lax.broadcasted_iota(I32, (L, L), 0)
    cc = lax.broadcasted_iota(I32, (L, L), 1)

    def bcast8(x11):
        return jnp.broadcast_to(x11, (8, LANES))

    def local(cs):
        chains = []
        for c in cs:
            r0 = pl.multiple_of(c * L, L)
            q = q_ref[0, pl.ds(r0, L), :]
            k = k_ref[0, pl.ds(r0, L), :] * scale
            vb = v_ref[0, pl.ds(r0, L), :].astype(BF16)
            kb = k.astype(BF16)
            qk = lax.dot_general(q.astype(BF16), kb, _NT, preferred_element_type=F32)
            k_t = k.T
            gcol = gc_ref[0, pl.ds(r0, L), :]
            for d in range(N_DIR):
                ch_i = CH_I + d * M_HEADS + head
                ch_f = CH_F + d * M_HEADS + head
                bc_col = _lane_pick(gcol, ch_f)
                i_row = gr_ref[0, pl.ds(ch_i, 1), pl.ds(r0, L)]
                bc_row = gr_ref[0, pl.ds(ch_f, 1), pl.ds(r0, L)]
                a_row = i_row - bc_row
                mask = (cc <= rr) if d == 0 else (cc >= rr)
                log_intra = jnp.where(mask, bc_col + a_row, -jnp.inf)
                m_intra = jnp.max(log_intra, axis=1, keepdims=True)
                p = qk * jnp.exp(log_intra - m_intra)
                b_last = bc_row[:, L - 1:L] if d == 0 else bc_row[:, 0:1]
                log_state = b_last + a_row
                m_loc = jnp.max(log_state, axis=1, keepdims=True)
                w_state = jnp.exp(log_state - m_loc)
                chains.append(dict(
                    idx=d * nc + c, d=d, r0=r0, vb=vb, kb=kb, pb=p.astype(BF16),
                    den=jnp.sum(p, axis=1, keepdims=True), m_intra=m_intra,
                    kw=(k_t * w_state).astype(BF16), w8=jnp.broadcast_to(w_state, (8, L)).astype(BF16),
                    b_last=b_last, m_loc=m_loc))
        for c in chains:
            hn_ref[c["d"], pl.ds(c["r0"], L), :] = jnp.dot(c["pb"], c["vb"], preferred_element_type=F32)
            den_ref[c["d"], pl.ds(c["r0"], L), :] = jnp.broadcast_to(c["den"], (L, LANES))
            mi_ref[c["d"], pl.ds(c["r0"], L), :] = jnp.broadcast_to(c["m_intra"], (L, LANES))
        for c in chains:
            cu_ref[c["idx"]] = jnp.dot(c["kw"], c["vb"], preferred_element_type=F32)
            nu_ref[c["idx"]] = jnp.dot(c["w8"], c["kb"], preferred_element_type=F32)
            bl_ref[c["idx"]] = bcast8(c["b_last"])
            ml_ref[c["idx"]] = bcast8(c["m_loc"])

    def local_trip(i, _):
        local([M_CHUNKS_PER_ITER * i + t for t in range(M_CHUNKS_PER_ITER)])
        return 0

    lax.fori_loop(0, nc // M_CHUNKS_PER_ITER, local_trip, 0)

    c_ref[...] = jnp.zeros(c_ref.shape, F32)

    def scan(j, carry):
        out = []
        for d in range(N_DIR):
            m_st, n_st = carry[d]
            idx = d * nc + (j if d == 0 else nc - 1 - j)
            c_st = c_ref[d]
            cs_ref[idx] = c_st.astype(BF16)
            ns_ref[idx] = n_st
            ms_ref[idx] = m_st
            bl = bl_ref[idx]
            ml = ml_ref[idx]
            m_new = jnp.maximum(bl + m_st, ml)
            decay = jnp.exp(bl + m_st - m_new)
            gain = jnp.exp(ml - m_new)
            c_ref[d] = decay[0:1, 0:1] * c_st + gain[0:1, 0:1] * cu_ref[idx]
            out.append((m_new, decay * n_st + gain * nu_ref[idx]))
        return tuple(out)

    m0 = jnp.full((8, LANES), -jnp.inf, F32)
    n0 = jnp.zeros((8, M_DK), F32)
    lax.fori_loop(0, nc, scan, ((m0, n0), (m0, n0)))

    def wide(x):
        return jnp.concatenate([x] * (M_DV // LANES), axis=1)

    def combine(c, _):
        r0 = pl.multiple_of(c * L, L)
        q = q_ref[0, pl.ds(r0, L), :]
        qb = q.astype(BF16)
        gcol = gc_ref[0, pl.ds(r0, L), :]
        hh = None
        for d in range(N_DIR):
            idx = d * nc + c
            bc = jnp.broadcast_to(_lane_pick(gcol, CH_F + d * M_HEADS + head), (L, LANES))
            den_i = den_ref[d, pl.ds(r0, L), :]
            m_i = mi_ref[d, pl.ds(r0, L), :]
            log_inter = bc + ms_ref[idx][0:1, :]
            m_row = jnp.maximum(log_inter, m_i)
            w_i = jnp.exp(m_i - m_row)
            w_x = jnp.exp(log_inter - m_row)
            q_c = jnp.dot(qb, cs_ref[idx], preferred_element_type=F32)
            q_n = jnp.broadcast_to(jnp.sum(q * ns_ref[idx][0:1, :], axis=1, keepdims=True), (L, LANES))
            num = wide(w_i) * hn_ref[d, pl.ds(r0, L), :] + wide(w_x) * q_c
            den = w_i * den_i + w_x * q_n
            h_d = num * wide(1.0 / jnp.maximum(jnp.abs(den), jnp.exp(-m_row)))
            hh = h_d if hh is None else hh + h_d
        hh = hh * lax.rsqrt(jnp.mean(hh * hh, axis=-1, keepdims=True) + RMS_EPS)
        hh = hh * ng_ref[0]
        y_ref[0, pl.ds(r0, L), :] = (_sigmoid(o_ref[0, pl.ds(r0, L), :]) * hh).astype(y_ref.dtype)
        return 0

    lax.fori_loop(0, nc, combine, 0, unroll=2)


def _mlstm(proj3, gcol, grow, norm_g, off_q, off_k, off_v, off_o, cast_weights):
    b, s, _ = proj3.shape
    w_specs, w_shapes = _cast_specs(cast_weights, b * M_HEADS, lambda i, h: i * M_HEADS + h)
    assert s % (M_CHUNK * M_CHUNKS_PER_ITER) == 0
    nc = s // M_CHUNK
    bq, bk = off_q // M_DK, off_k // M_DK
    bv, bo = off_v // M_DV, off_o // M_DV
    return pl.pallas_call(
        functools.partial(_mlstm_kernel, len(cast_weights)),
        grid=(b, M_HEADS),
        in_specs=[
            pl.BlockSpec((1, s, M_DK), lambda i, h: (i, 0, bq + h)),
            pl.BlockSpec((1, s, M_DK), lambda i, h: (i, 0, bk + h)),
            pl.BlockSpec((1, s, M_DV), lambda i, h: (i, 0, bv + h)),
            pl.BlockSpec((1, s, M_DV), lambda i, h: (i, 0, bo + h)),
            pl.BlockSpec((1, s, LANES), lambda i, h: (i, 0, 0)),
            pl.BlockSpec((1, LANES, s), lambda i, h: (i, 0, 0)),
            pl.BlockSpec((1, 1, M_DV), lambda i, h: (h, 0, 0)),
            *w_specs,
        ],
        out_specs=[pl.BlockSpec((1, s, M_DV), lambda i, h: (i, 0, h)), *w_specs],
        out_shape=[jax.ShapeDtypeStruct((b, s, M_HEADS * M_DV), BF16), *w_shapes],
        scratch_shapes=[
            pltpu.VMEM((N_DIR, s, M_DV), F32),
            pltpu.VMEM((N_DIR, s, LANES), F32),
            pltpu.VMEM((N_DIR, s, LANES), F32),
            pltpu.VMEM((N_DIR * nc, M_DK, M_DV), F32),
            pltpu.VMEM((N_DIR * nc, 8, M_DK), F32),
            pltpu.VMEM((N_DIR * nc, 8, LANES), F32),
            pltpu.VMEM((N_DIR * nc, 8, LANES), F32),
            pltpu.VMEM((N_DIR * nc, M_DK, M_DV), BF16),
            pltpu.VMEM((N_DIR * nc, 8, M_DK), F32),
            pltpu.VMEM((N_DIR * nc, 8, LANES), F32),
            pltpu.VMEM((N_DIR, M_DK, M_DV), F32),
        ],
        compiler_params=_cparams(("parallel", "arbitrary")),
        name="mlstm",
    )(proj3, proj3, proj3, proj3, gcol, grow, norm_g, *cast_weights)


def _delta_kernel(n_cast, q_ref, k_ref, v_ref, z_ref, cwq_ref, cwk_ref, cwv_ref, gc_ref, gr_ref, ng_ref, *refs):
    w_refs, y_ref, wo_refs = refs[:n_cast], refs[n_cast], refs[n_cast + 1:2 * n_cast + 1]
    xp_ref, qs_ref, ks_ref, vs_ref, kq_ref, n_ref, dec_ref, o_ref, st_ref = refs[2 * n_cast + 1:]
    _cast_blocks(w_refs, wo_refs)
    t = pl.program_id(0)
    head = jnp.minimum(t, pl.num_programs(0) - 2) % D_HEADS
    s = q_ref.shape[1]
    L = D_CHUNK
    G = D_GROUP
    nc = s // L
    ng = s // G
    halo = CONV_HALO
    pad = CONV_W // 2

    @pl.when(t == 0)
    def _():
        for r in (kq_ref, n_ref, dec_ref, o_ref):
            r[...] = jnp.zeros(r.shape, r.dtype)

    xp_ref[0:halo, :] = jnp.zeros((halo, D_DIM), F32)
    xp_ref[s + halo:s + 2 * halo, :] = jnp.zeros((halo, D_DIM), F32)

    def conv_into(x_ref, cw_ref, dst_ref, l2, mult):
        xp_ref[halo:s + halo, :] = x_ref[0]
        for blk in range(ng):
            base = halo + blk * G - pad
            acc = xp_ref[base:base + G, :] * cw_ref[0:1, :]
            for j in range(1, CONV_W):
                acc = acc + xp_ref[base + j:base + j + G, :] * cw_ref[j:j + 1, :]
            y = acc * _sigmoid(acc)
            if l2:
                y = y * (lax.rsqrt(jnp.sum(y * y, axis=-1, keepdims=True) + L2_EPS) * mult)
            dst_ref[blk * G:(blk + 1) * G, :] = y

    conv_into(q_ref, cwq_ref, qs_ref, True, D_DIM ** -0.5)
    conv_into(k_ref, cwk_ref, ks_ref, True, 1.0)
    conv_into(v_ref, cwv_ref, vs_ref, False, 1.0)

    st_ref[...] = jnp.zeros(st_ref.shape, F32)

    def scan_step(scan, j):
        for d in range(N_DIR):
            c = j if d == 0 else nc - 1 - j
            r0 = pl.multiple_of(c * L, L)
            idx = (scan + d) * nc + c
            st = st_ref[d]
            res = jnp.dot(kq_ref[idx], st.astype(BF16), preferred_element_type=F32)
            st_ref[d] = dec_ref[idx][0:1, :] * st + res[0:D_DIM] + n_ref[idx]
            o_ref[scan + d, pl.ds(r0, L), :] = o_ref[scan + d, pl.ds(r0, L), :] + res[D_DIM:D_DIM + L]

    tchunk = lax.broadcasted_iota(I32, (D_DIM, D_GROUP), 1) // D_CHUNK
    rr = lax.broadcasted_iota(I32, (G, G), 0)
    cc = lax.broadcasted_iota(I32, (G, G), 1)
    same = (rr // L) == (cc // L)
    pr = lax.broadcasted_iota(I32, (L, G), 0)
    pc = lax.broadcasted_iota(I32, (L, G), 1)
    eye_p = jnp.where((pc % L) == pr, 1.0, 0.0).astype(F32)

    def pack(m):
        out = m[0:L]
        for i in range(1, D_PER_GROUP):
            out = out + m[i * L:(i + 1) * L]
        return out

    def unpack(p):
        return jnp.where(same, jnp.concatenate([p] * D_PER_GROUP, axis=0), 0.0)

    scan_per_trip = nc // (ng // D_GROUPS_PER_ITER)

    def groups(fill, scan, trip):
        pending = iter(range(scan_per_trip))

        def scan_some(n):
            for _ in range(n):
                j = next(pending, None)
                if j is not None:
                    scan_step(scan, trip * scan_per_trip + j)

        chains = []
        for gi in [D_GROUPS_PER_ITER * trip + g for g in range(D_GROUPS_PER_ITER)]:
            r0 = pl.multiple_of(gi * G, G)
            kk_ = ks_ref[pl.ds(r0, G), :]
            qq_ = qs_ref[pl.ds(r0, G), :]
            vv_ = vs_ref[pl.ds(r0, G), :]
            kb = kk_.astype(BF16)
            kk = lax.dot_general(kb, kb, _NT, preferred_element_type=F32)
            qk = lax.dot_general(qq_.astype(BF16), kb, _NT, preferred_element_type=F32)
            gcol = gc_ref[0, pl.ds(r0, G), :]
            for d in range(N_DIR):
                ch = d * D_HEADS + head
                g_col = _lane_pick(gcol, CH_G + ch)
                b_col = _lane_pick(gcol, CH_B + ch)
                t_col = _lane_pick(gcol, CH_T + ch)
                g_row = gr_ref[0, pl.ds(CH_G + ch, 1), pl.ds(r0, G)]
                tri = (cc <= rr) if d == 0 else (cc >= rr)
                strict = (cc < rr) if d == 0 else (cc > rr)
                gam = jnp.exp(jnp.where(tri, jnp.where(same, g_col - g_row, -jnp.inf), -jnp.inf))
                x = jnp.where(strict, -(b_col * kk * gam), 0.0)
                eg = jnp.exp(g_col)
                chains.append(dict(
                    gi=gi, slot=fill + d, r0=r0, x=x, xp=pack(x), attn=(qk * gam).astype(BF16),
                    rhs=jnp.concatenate([b_col * vv_, (b_col * eg) * kk_], axis=1).astype(BF16),
                    qg=qq_ * eg, kdt=(kk_ * jnp.exp(t_col - g_col)).T, dec=jnp.exp(t_col)))
            scan_some(-(-(scan_per_trip - D_SCAN_STEPS_BETWEEN_STAGES) // D_GROUPS_PER_ITER))
        for c in chains:
            c["tp"] = eye_p + c["xp"]
            c["p"] = _bdot(c["xp"], c["x"])
        scan_some(1)
        for _it in range(4):
            for c in chains:
                res = _bdot(jnp.concatenate([c["tp"], c["p"]], axis=0), unpack(c["p"]))
                c["tp"] = c["tp"] + res[0:L]
                c["p"] = res[L:2 * L]
            scan_some(1)
        for c in chains:
            c["tp"] = c["tp"] + _bdot(c["tp"], unpack(c["p"]))
        scan_some(1)
        for c in chains:
            c["uw"] = _bdot(unpack(c["tp"]), c["rhs"]).astype(BF16)
        scan_some(1)
        for c in chains:
            au = jnp.dot(c["attn"], c["uw"], preferred_element_type=F32)
            o_ref[c["slot"], pl.ds(c["r0"], G), :] = au[:, 0:D_DIM]
            c["qe"] = (c["qg"] - au[:, D_DIM:2 * D_DIM]).astype(BF16)
        scan_some(1)
        for c in chains:
            for ci in range(D_PER_GROUP):
                idx = c["slot"] * nc + c["gi"] * D_PER_GROUP + ci
                ku = _bdot(jnp.where(tchunk == ci, c["kdt"], 0.0), c["uw"])
                n_ref[idx] = ku[:, 0:D_DIM]
                kq_ref[idx, 0:D_DIM, :] = (-ku[:, D_DIM:2 * D_DIM]).astype(BF16)
                kq_ref[idx, D_DIM:D_DIM + L, :] = c["qe"][ci * L:(ci + 1) * L]
                dec_ref[idx] = jnp.broadcast_to(c["dec"][ci * L:ci * L + 8], (8, D_DIM))
        scan_some(scan_per_trip)

    def finish(scan, gi):
        r0 = pl.multiple_of(gi * G, G)
        o = o_ref[scan, pl.ds(r0, G), :] + o_ref[scan + 1, pl.ds(r0, G), :]
        o = o * lax.rsqrt(jnp.mean(o * o, axis=-1, keepdims=True) + RMS_EPS) * ng_ref[...]
        z = z_ref[0, pl.ds(r0, G), :]
        y_ref[0, pl.ds(r0, G), :] = (o * (z * _sigmoid(z))).astype(y_ref.dtype)

    fill = (t % 2) * N_DIR
    scan = N_DIR - fill

    def group_batch(i, carry):
        groups(fill, scan, i)
        return carry

    def finish_batch(gi, carry):
        finish(scan, gi)
        return carry

    lax.fori_loop(0, ng // D_GROUPS_PER_ITER, group_batch, 0)
    lax.fori_loop(0, ng, finish_batch, 0, unroll=4)


def _delta(proj3, conv_w, gcol, grow, norm_g, off_q, off_k, off_v, off_z, cast_weights):
    b, s, _ = proj3.shape
    w = D_HEADS * D_DIM
    nc = s // D_CHUNK
    trips = s // (D_GROUP * D_GROUPS_PER_ITER)
    assert s % (D_GROUP * D_GROUPS_PER_ITER) == 0 and nc % trips == 0
    bq, bk, bv, bz = (o // D_DIM for o in (off_q, off_k, off_v, off_z))
    n_units = b * D_HEADS
    cur = lambda t: jnp.minimum(t, n_units - 1)
    prev = lambda t: jnp.maximum(t - 1, 0)
    col = lambda unit, blk: lambda t: (unit(t) // D_HEADS, 0, blk + unit(t) % D_HEADS)
    w_specs, w_shapes = _cast_specs(cast_weights, n_units, cur)
    return pl.pallas_call(
        functools.partial(_delta_kernel, len(cast_weights)),
        grid=(n_units + 1,),
        in_specs=[
            pl.BlockSpec((1, s, D_DIM), col(cur, bq)),
            pl.BlockSpec((1, s, D_DIM), col(cur, bk)),
            pl.BlockSpec((1, s, D_DIM), col(cur, bv)),
            pl.BlockSpec((1, s, D_DIM), col(prev, bz)),
            pl.BlockSpec((CONV_W, D_DIM), lambda t: (0, cur(t) % D_HEADS)),
            pl.BlockSpec((CONV_W, D_DIM), lambda t: (0, D_HEADS + cur(t) % D_HEADS)),
            pl.BlockSpec((CONV_W, D_DIM), lambda t: (0, 2 * D_HEADS + cur(t) % D_HEADS)),
            pl.BlockSpec((1, s, LANES), lambda t: (cur(t) // D_HEADS, 0, 0)),
            pl.BlockSpec((1, LANES, s), lambda t: (cur(t) // D_HEADS, 0, 0)),
            pl.BlockSpec((1, D_DIM), lambda t: (0, 0)),
            *w_specs,
        ],
        out_specs=[pl.BlockSpec((1, s, D_DIM), col(prev, 0)), *w_specs],
        out_shape=[jax.ShapeDtypeStruct((b, s, w), BF16), *w_shapes],
        scratch_shapes=[
            pltpu.VMEM((s + 2 * CONV_HALO, D_DIM), F32),
            pltpu.VMEM((s, D_DIM), F32),
            pltpu.VMEM((s, D_DIM), F32),
            pltpu.VMEM((s, D_DIM), F32),
            pltpu.VMEM((2 * N_DIR * nc, D_DIM + D_CHUNK, D_DIM), BF16),
            pltpu.VMEM((2 * N_DIR * nc, D_DIM, D_DIM), F32),
            pltpu.VMEM((2 * N_DIR * nc, 8, D_DIM), F32),
            pltpu.VMEM((2 * N_DIR, s, D_DIM), F32),
            pltpu.VMEM((N_DIR, D_DIM, D_DIM), F32),
        ],
        compiler_params=_cparams(("arbitrary",)),
        name="delta",
    )(proj3, proj3, proj3, proj3, conv_w, conv_w, conv_w, gcol, grow, norm_g, *cast_weights)


def _mix_kernel(n_gate_blocks, x_ref, ym_ref, yd_ref, *refs):
    gate_refs, (wbm_ref, wbd_ref, wo_ref, g2_ref, o_ref, h_ref) = refs[:2 * n_gate_blocks], refs[2 * n_gate_blocks:]
    gate = lambda rs: _sigmoid(jnp.concatenate([r[...] for r in rs], axis=1))
    a = jnp.dot(ym_ref[...], wbm_ref[...], preferred_element_type=F32)
    b = jnp.dot(yd_ref[...], wbd_ref[...], preferred_element_type=F32)
    mixed = gate(gate_refs[:n_gate_blocks]) * a + gate(gate_refs[n_gate_blocks:]) * b
    x = x_ref[...] + jnp.dot(mixed.astype(BF16), wo_ref[...], preferred_element_type=F32)
    o_ref[...] = x
    ms = jnp.mean(x * x, axis=-1, keepdims=True)
    h_ref[...] = (x * lax.rsqrt(ms + RMS_EPS) * g2_ref[...]).astype(BF16)


def _mix(x2, ym2, yd2, proj2, off_gates, wbm, wbd, wo, g2, tm):
    m, d = x2.shape
    const = lambda i: (0, 0)
    gw = math.gcd(off_gates, d)
    gate_specs = [pl.BlockSpec((tm, gw), functools.partial(lambda i, c: (i, c), c=(off_gates + t * gw) // gw))
                  for t in range(2 * d // gw)]
    return pl.pallas_call(
        functools.partial(_mix_kernel, d // gw),
        grid=(m // tm,),
        in_specs=[
            pl.BlockSpec((tm, d), lambda i: (i, 0)),
            pl.BlockSpec((tm, ym2.shape[1]), lambda i: (i, 0)),
            pl.BlockSpec((tm, yd2.shape[1]), lambda i: (i, 0)),
            *gate_specs,
            pl.BlockSpec(wbm.shape, const),
            pl.BlockSpec(wbd.shape, const),
            pl.BlockSpec(wo.shape, const),
            pl.BlockSpec((1, d), const),
        ],
        out_specs=[pl.BlockSpec((tm, d), lambda i: (i, 0)), pl.BlockSpec((tm, d), lambda i: (i, 0))],
        out_shape=[jax.ShapeDtypeStruct((m, d), F32), jax.ShapeDtypeStruct((m, d), BF16)],
        compiler_params=_cparams(("parallel",)),
        name="mix",
    )(x2, ym2, yd2, *([proj2] * len(gate_specs)), wbm, wbd, wo, g2)


def _ffn_kernel(n_slices, x_ref, h_ref, gf_ref, w1_ref, w2_ref, o_ref, acc_ref):
    f = pl.program_id(1)
    last = n_slices - 1

    def ffn_part(sl):
        hid = jnp.dot(h_ref[sl, :], w1_ref[...], preferred_element_type=F32)
        act = jnp.square(jnp.maximum(hid, 0.0)).astype(BF16)
        return jnp.dot(act, w2_ref[...], preferred_element_type=F32)

    @pl.when(jnp.logical_and(f == 0, f < last))
    def _():
        acc_ref[...] = ffn_part(slice(None))

    @pl.when(jnp.logical_and(f > 0, f < last))
    def _():
        acc_ref[...] += ffn_part(slice(None))

    @pl.when(f == last)
    def _():
        rows = x_ref.shape[0] // EDGE_STEP_PIECES
        for r in range(EDGE_STEP_PIECES):
            sl = slice(r * rows, (r + 1) * rows)
            y = ffn_part(sl) if last == 0 else acc_ref[sl, :] + ffn_part(sl)
            x = x_ref[sl, :] + y
            ms = jnp.mean(x * x, axis=-1, keepdims=True)
            o_ref[sl, :] = x * lax.rsqrt(ms + RMS_EPS) * gf_ref[...]


def _ffn(x2, h2, gf, w1, w2, tm, tf):
    m, d = x2.shape
    dff = w1.shape[1]
    return pl.pallas_call(
        functools.partial(_ffn_kernel, dff // tf),
        grid=(m // tm, dff // tf),
        in_specs=[
            pl.BlockSpec((tm, d), lambda i, f: (i, 0)),
            pl.BlockSpec((tm, d), lambda i, f: (i, 0)),
            pl.BlockSpec((1, d), lambda i, f: (0, 0)),
            pl.BlockSpec((d, tf), lambda i, f: (0, f)),
            pl.BlockSpec((tf, d), lambda i, f: (f, 0)),
        ],
        out_specs=pl.BlockSpec((tm, d), lambda i, f: (i, 0)),
        out_shape=jax.ShapeDtypeStruct((m, d), F32),
        scratch_shapes=[pltpu.VMEM((tm, d), F32)],
        compiler_params=_cparams(("parallel", "arbitrary")),
        name="ffn",
    )(x2, h2, gf, w1, w2)


def _pick_tile(n, candidates):
    for c in candidates:
        if n % c == 0:
            return c
    raise ValueError(f"no tile in {candidates} divides {n}")


def _layer(x, norm1_g, w_in, i_bias, f_bias, m_norm_g, conv_w, a_log, dt_bias, d_norm_g,
           w_bm, w_bd, w_out, norm2_g, w_ff1, w_ff2, out_g):
    b, s, d = x.shape
    m = b * s
    mw, mqk, dw = M_HEADS * M_DV, M_HEADS * M_DK, D_HEADS * D_DIM
    nm, nd = N_DIR * M_HEADS, N_DIR * D_HEADS
    splits = (mqk, mqk, mw, mw, nm, nm, 3 * dw, dw, nd, nd, 2 * d)
    bounds = [0]
    for w_ in splits:
        bounds.append(bounds[-1] + w_)
    wt = w_in.T.astype(BF16)
    rows = lambda first, last: wt[bounds[first]:bounds[last + 1]]
    seg_bounds = [(bounds[0], bounds[4]), (bounds[6], bounds[8]), (bounds[10], bounds[11])]
    n_small = 2 * nm + 2 * nd
    wt_small = jnp.concatenate([rows(4, 5), rows(8, 9), jnp.zeros((LANES - n_small, d), BF16)], axis=0)
    zpad = lambda n: jnp.zeros((n,), F32)
    bias_row = jnp.concatenate([i_bias.reshape(-1), f_bias.reshape(-1), dt_bias.reshape(-1), zpad(LANES - CH_B)])
    alog_row = jnp.concatenate([zpad(CH_G), a_log.reshape(-1), zpad(LANES - CH_B)])
    gate_params = jnp.concatenate([bias_row[None], alog_row[None], jnp.zeros((6, LANES), F32)], axis=0)

    x2 = x.reshape(m, d)
    tm = _pick_tile(m, (512, 256, 128))
    tn = next(c for c in (1024, 512, 256) if all((hi - lo) % c == 0 for lo, hi in seg_bounds))
    proj2, smalls2 = _inproj(x2, norm1_g.reshape(1, d), wt, wt_small, seg_bounds,
                             _pick_tile(m, (1024, 512, 256)), tn)
    proj3 = proj2.reshape(b, s, -1)
    gcol, grow = _gate_prep(smalls2.reshape(b, s, LANES), gate_params)

    off_d = 2 * mqk + 2 * mw
    off_g = off_d + 4 * dw
    ym, w_bm16, w_bd16, w_out16 = _mlstm(proj3, gcol, grow, m_norm_g.reshape(M_HEADS, 1, M_DV),
                                         0, mqk, 2 * mqk, 2 * mqk + mw, (w_bm, w_bd, w_out))
    yd, w_ff1_16, w_ff2_16 = _delta(proj3, conv_w, gcol, grow, d_norm_g.reshape(1, D_DIM),
                                    off_d, off_d + dw, off_d + 2 * dw, off_d + 3 * dw, (w_ff1, w_ff2))

    x1, h2 = _mix(x2, ym.reshape(m, mw), yd.reshape(m, dw), proj2, off_g, w_bm16, w_bd16, w_out16,
                  norm2_g.reshape(1, d), _pick_tile(m, (256, 128)))
    tf = _pick_tile(w_ff1.shape[1], (1024, 512, 256))
    out = _ffn(x1, h2, out_g.reshape(1, d), w_ff1_16, w_ff2_16, tm, tf)
    return out.reshape(b, s, d)


def kernel(x, norm1_g, w_in, mlstm_i_bias, mlstm_f_bias, mlstm_norm_g, delta_conv_w, delta_a_log,
           delta_dt_bias, delta_norm_g, w_branch_m, w_branch_d, w_out, norm2_g, w_ff1, w_ff2, norm_f_g):
    depth = w_in.shape[0]
    assert depth == 1, "the fused FFN + final-norm epilogue assumes a single layer"
    return _layer(x, norm1_g[0], w_in[0], mlstm_i_bias[0], mlstm_f_bias[0], mlstm_norm_g[0], delta_conv_w[0],
                  delta_a_log[0], delta_dt_bias[0], delta_norm_g[0], w_branch_m[0], w_branch_d[0], w_out[0],
                  norm2_g[0], w_ff1[0], w_ff2[0], norm_f_g)
```

```python
import functools
import math

import jax
import jax.numpy as jnp
import numpy as np
from jax import lax
from jax.experimental import pallas as pl
from jax.experimental.pallas import tpu as pltpu

F32 = jnp.float32
BF16 = jnp.bfloat16
I32 = jnp.int32

N_DIR = 2
M_HEADS = 4
M_DK = 128
M_DV = 256
D_HEADS = 8
D_DIM = 128
CONV_W = 5
RMS_EPS = 1e-6
L2_EPS = 1e-6

LANES = 128
BF16_SUBLANES = 16
M_CHUNK = 256
M_CHUNKS_PER_ITER = 4
D_CHUNK = 64
D_GROUP = 256
D_PER_GROUP = D_GROUP // D_CHUNK
D_GROUPS_PER_ITER = 4
D_SCAN_STEPS_BETWEEN_STAGES = 8
GATE_ROWS = 256
GATE_BLOCKS_PER_STEP = 4
EDGE_STEP_PIECES = 4
CONV_HALO = 8

CH_I = 0
CH_F = 8
CH_G = 16
CH_B = 32
CH_T = 48

VMEM_LIMIT = 56 * 1024 * 1024

_NT = (((1,), (1,)), ((), ()))


def _cparams(sem):
    return pltpu.CompilerParams(dimension_semantics=sem, vmem_limit_bytes=VMEM_LIMIT)


def _bdot(a, b):
    return jnp.dot(a.astype(BF16), b.astype(BF16), preferred_element_type=F32)


def _bdot_nt(a, b):
    return lax.dot_general(a.astype(BF16), b.astype(BF16), _NT, preferred_element_type=F32)


def _sigmoid(x):
    return 1.0 / (1.0 + jnp.exp(-x))


def _softplus(x):
    return jnp.maximum(x, 0.0) + jnp.log1p(jnp.exp(-jnp.abs(x)))


def _cast_blocks(w_refs, o_refs):
    for w_ref, o_ref in zip(w_refs, o_refs):
        o_ref[...] = w_ref[...].astype(o_ref.dtype)


def _cast_specs(weights, n_steps, step_of):
    specs, shapes = [], []
    for w in weights:
        rows, cols = w.shape
        assert rows % (n_steps * BF16_SUBLANES) == 0
        specs.append(pl.BlockSpec((rows // n_steps, cols), lambda *g: (step_of(*g), 0)))
        shapes.append(jax.ShapeDtypeStruct((rows, cols), BF16))
    return specs, shapes


def _lane_pick(x, ch):
    lane = lax.broadcasted_iota(I32, x.shape, 1)
    return jnp.sum(jnp.where(lane == ch, x, 0.0), axis=1, keepdims=True)


def _inproj_kernel(x_ref, g_ref, w_ref, ws_ref, o_ref, os_ref, hn_ref):
    j = pl.program_id(1)

    @pl.when(j == 0)
    def _():
        rows = x_ref.shape[0] // EDGE_STEP_PIECES
        for r in range(EDGE_STEP_PIECES):
            sl = slice(r * rows, (r + 1) * rows)
            x = x_ref[sl, :]
            ms = jnp.mean(x * x, axis=-1, keepdims=True)
            hn = (x * lax.rsqrt(ms + RMS_EPS) * g_ref[...]).astype(BF16)
            hn_ref[sl, :] = hn
            os_ref[sl, :] = lax.dot_general(hn, ws_ref[...], _NT, preferred_element_type=F32)
            o_ref[sl, :] = lax.dot_general(hn, w_ref[...], _NT, preferred_element_type=F32)

    @pl.when(j > 0)
    def _():
        o_ref[...] = lax.dot_general(hn_ref[...], w_ref[...], _NT, preferred_element_type=F32)


def _inproj(x2, g, wt, wt_small, seg_bounds, tm, tn):
    m, d = x2.shape
    starts = [0]
    for lo, hi in seg_bounds:
        starts.append(starts[-1] + (hi - lo) // tn)

    def w_row(i, j):
        row = seg_bounds[0][0] + j * tn
        for k in range(1, len(seg_bounds)):
            row = jnp.where(j >= starts[k], seg_bounds[k][0] + (j - starts[k]) * tn, row)
        return pl.multiple_of(row, BF16_SUBLANES), 0

    return pl.pallas_call(
        _inproj_kernel,
        grid=(m // tm, starts[-1]),
        in_specs=[
            pl.BlockSpec((tm, d), lambda i, j: (i, 0)),
            pl.BlockSpec((1, d), lambda i, j: (0, 0)),
            pl.BlockSpec((pl.Element(tn), pl.Element(d)), w_row),
            pl.BlockSpec((LANES, d), lambda i, j: (0, 0)),
        ],
        out_specs=[
            pl.BlockSpec((tm, tn), lambda i, j: (i, j)),
            pl.BlockSpec((tm, LANES), lambda i, j: (i, 0)),
        ],
        out_shape=[jax.ShapeDtypeStruct((m, starts[-1] * tn), F32), jax.ShapeDtypeStruct((m, LANES), F32)],
        scratch_shapes=[pltpu.VMEM((tm, d), BF16)],
        compiler_params=_cparams(("parallel", "arbitrary")),
        name="inproj",
    )(x2, g, wt, wt_small)


def _cumsum_matrices():
    rr, cc = np.indices((GATE_ROWS, GATE_ROWS))
    low, upp = cc <= rr, cc >= rr
    same_d = (rr // D_CHUNK) == (cc // D_CHUNK)
    same_m = (rr // M_CHUNK) == (cc // M_CHUNK)
    mats = np.concatenate([
        low & same_m,
        upp & same_m,
        low & same_d,
        upp & same_d,
        same_d,
    ], axis=0)
    return jnp.asarray(mats.astype(np.float32), dtype=BF16)


def _gate_kernel(sm_ref, par_ref, mats_ref, col_ref, row_ref):
    r = GATE_ROWS
    lane = lax.broadcasted_iota(I32, (r, LANES), 1)
    neg_a = -jnp.exp(par_ref[1:2, :])
    xes, x3s = [], []
    for blk in range(GATE_BLOCKS_PER_STEP):
        x = sm_ref[0, blk * r:(blk + 1) * r, :] + par_ref[0:1, :]
        logf = -_softplus(-x)
        g = neg_a * _softplus(x)
        beta = _sigmoid(x)
        xe = jnp.where(lane < CH_F, x,
                       jnp.where(lane < CH_G, logf,
                                 jnp.where(lane < CH_B, g,
                                           jnp.where(lane < CH_T, beta, 0.0))))
        hi = xe.astype(BF16)
        r1 = xe - hi.astype(F32)
        mid = r1.astype(BF16)
        lo = (r1 - mid.astype(F32)).astype(BF16)
        xes.append(xe)
        x3s.append(jnp.concatenate([hi, mid, lo], axis=1))
    y3s = [jnp.dot(mats_ref[...], x3, preferred_element_type=F32) for x3 in x3s]
    half_m = CH_F + M_HEADS
    half_d = CH_G + D_HEADS
    for blk, (xe, y3) in enumerate(zip(xes, y3s)):
        y = y3[:, 0:LANES] + y3[:, LANES:2 * LANES] + y3[:, 2 * LANES:3 * LANES]
        pm, sm_, pd, sd, td = (y[i * r:(i + 1) * r] for i in range(5))
        out = jnp.where(lane < CH_F, xe,
              jnp.where(lane < half_m, pm,
              jnp.where(lane < CH_G, sm_,
              jnp.where(lane < half_d, pd,
              jnp.where(lane < CH_B, sd,
              jnp.where(lane < CH_T, xe,
              jnp.where(lane < CH_T + N_DIR * D_HEADS, pltpu.roll(td, CH_T - CH_G, axis=1), 0.0)))))))
        col_ref[0, blk * r:(blk + 1) * r, :] = out
        row_ref[0, :, blk * r:(blk + 1) * r] = out.T


def _gate_prep(smalls3, params):
    b, s, _ = smalls3.shape
    r = GATE_ROWS * GATE_BLOCKS_PER_STEP
    assert s % r == 0
    return pl.pallas_call(
        _gate_kernel,
        grid=(b, s // r),
        in_specs=[
            pl.BlockSpec((1, r, LANES), lambda i, j: (i, j, 0)),
            pl.BlockSpec((8, LANES), lambda i, j: (0, 0)),
            pl.BlockSpec((5 * GATE_ROWS, GATE_ROWS), lambda i, j: (0, 0)),
        ],
        out_specs=[
            pl.BlockSpec((1, r, LANES), lambda i, j: (i, j, 0)),
            pl.BlockSpec((1, LANES, r), lambda i, j: (i, 0, j)),
        ],
        out_shape=[jax.ShapeDtypeStruct((b, s, LANES), F32), jax.ShapeDtypeStruct((b, LANES, s), F32)],
        compiler_params=_cparams(("parallel", "parallel")),
        name="gateprep",
    )(smalls3, params, _cumsum_matrices())


def _mlstm_kernel(n_cast, q_ref, k_ref, v_ref, o_ref, gc_ref, gr_ref, ng_ref, *refs):
    w_refs, y_ref, wo_refs = refs[:n_cast], refs[n_cast], refs[n_cast + 1:2 * n_cast + 1]
    hn_ref, den_ref, mi_ref, cu_ref, nu_ref, bl_ref, ml_ref, cs_ref, ns_ref, ms_ref, c_ref = refs[2 * n_cast + 1:]
    _cast_blocks(w_refs, wo_refs)
    head = pl.program_id(1)
    s = q_ref.shape[1]
    L = M_CHUNK
    nc = s // L
    scale = M_DK ** -0.5
    rr = lax.broadcasted_iota(I32, (L, L), 0)
    cc = lax.broadcasted_iota(I32, (L, L), 1)

    def bcast8(x11):
        return jnp.broadcast_to(x11, (8, LANES))

    def local(cs):
        chains = []
        for c in cs:
            r0 = pl.multiple_of(c * L, L)
            q = q_ref[0, pl.ds(r0, L), :]
            k = k_ref[0, pl.ds(r0, L), :] * scale
            vb = v_ref[0, pl.ds(r0, L), :].astype(BF16)
            kb = k.astype(BF16)
            qk = lax.dot_general(q.astype(BF16), kb, _NT, preferred_element_type=F32)
            k_t = k.T
            gcol = gc_ref[0, pl.ds(r0, L), :]
            for d in range(N_DIR):
                ch_i = CH_I + d * M_HEADS + head
                ch_f = CH_F + d * M_HEADS + head
                bc_col = _lane_pick(gcol, ch_f)
                i_row = gr_ref[0, pl.ds(ch_i, 1), pl.ds(r0, L)]
                bc_row = gr_ref[0, pl.ds(ch_f, 1), pl.ds(r0, L)]
                a_row = i_row - bc_row
                mask = (cc <= rr) if d == 0 else (cc >= rr)
                log_intra = jnp.where(mask, bc_col + a_row, -jnp.inf)
                m_intra = jnp.max(log_intra, axis=1, keepdims=True)
                p = qk * jnp.exp(log_intra - m_intra)
                b_last = bc_row[:, L - 1:L] if d == 0 else bc_row[:, 0:1]
                log_state = b_last + a_row
                m_loc = jnp.max(log_state, axis=1, keepdims=True)
                w_state = jnp.exp(log_state - m_loc)
                chains.append(dict(
                    idx=d * nc + c, d=d, r0=r0, vb=vb, kb=kb, pb=p.astype(BF16),
                    den=jnp.sum(p, axis=1, keepdims=True), m_intra=m_intra,
                    kw=(k_t * w_state).astype(BF16), w8=jnp.broadcast_to(w_state, (8, L)).astype(BF16),
                    b_last=b_last, m_loc=m_loc))
        for c in chains:
            hn_ref[c["d"], pl.ds(c["r0"], L), :] = jnp.dot(c["pb"], c["vb"], preferred_element_type=F32)
            den_ref[c["d"], pl.ds(c["r0"], L), :] = jnp.broadcast_to(c["den"], (L, LANES))
            mi_ref[c["d"], pl.ds(c["r0"], L), :] = jnp.broadcast_to(c["m_intra"], (L, LANES))
        for c in chains:
            cu_ref[c["idx"]] = jnp.dot(c["kw"], c["vb"], preferred_element_type=F32)
            nu_ref[c["idx"]] = jnp.dot(c["w8"], c["kb"], preferred_element_type=F32)
            bl_ref[c["idx"]] = bcast8(c["b_last"])
            ml_ref[c["idx"]] = bcast8(c["m_loc"])

    def local_trip(i, _):
        local([M_CHUNKS_PER_ITER * i + t for t in range(M_CHUNKS_PER_ITER)])
        return 0

    lax.fori_loop(0, nc // M_CHUNKS_PER_ITER, local_trip, 0)

    c_ref[...] = jnp.zeros(c_ref.shape, F32)

    def scan(j, carry):
        out = []
        for d in range(N_DIR):
            m_st, n_st = carry[d]
            idx = d * nc + (j if d == 0 else nc - 1 - j)
            c_st = c_ref[d]
            cs_ref[idx] = c_st.astype(BF16)
            ns_ref[idx] = n_st
            ms_ref[idx] = m_st
            bl = bl_ref[idx]
            ml = ml_ref[idx]
            m_new = jnp.maximum(bl + m_st, ml)
            decay = jnp.exp(bl + m_st - m_new)
            gain = jnp.exp(ml - m_new)
            c_ref[d] = decay[0:1, 0:1] * c_st + gain[0:1, 0:1] * cu_ref[idx]
            out.append((m_new, decay * n_st + gain * nu_ref[idx]))
        return tuple(out)

    m0 = jnp.full((8, LANES), -jnp.inf, F32)
    n0 = jnp.zeros((8, M_DK), F32)
    lax.fori_loop(0, nc, scan, ((m0, n0), (m0, n0)))

    def wide(x):
        return jnp.concatenate([x] * (M_DV // LANES), axis=1)

    def combine(c, _):
        r0 = pl.multiple_of(c * L, L)
        q = q_ref[0, pl.ds(r0, L), :]
        qb = q.astype(BF16)
        gcol = gc_ref[0, pl.ds(r0, L), :]
        hh = None
        for d in range(N_DIR):
            idx = d * nc + c
            bc = jnp.broadcast_to(_lane_pick(gcol, CH_F + d * M_HEADS + head), (L, LANES))
            den_i = den_ref[d, pl.ds(r0, L), :]
            m_i = mi_ref[d, pl.ds(r0, L), :]
            log_inter = bc + ms_ref[idx][0:1, :]
            m_row = jnp.maximum(log_inter, m_i)
            w_i = jnp.exp(m_i - m_row)
            w_x = jnp.exp(log_inter - m_row)
            q_c = jnp.dot(qb, cs_ref[idx], preferred_element_type=F32)
            q_n = jnp.broadcast_to(jnp.sum(q * ns_ref[idx][0:1, :], axis=1, keepdims=True), (L, LANES))
            num = wide(w_i) * hn_ref[d, pl.ds(r0, L), :] + wide(w_x) * q_c
            den = w_i * den_i + w_x * q_n
            h_d = num * wide(1.0 / jnp.maximum(jnp.abs(den), jnp.exp(-m_row)))
            hh = h_d if hh is None else hh + h_d
        hh = hh * lax.rsqrt(jnp.mean(hh * hh, axis=-1, keepdims=True) + RMS_EPS)
        hh = hh * ng_ref[0]
        y_ref[0, pl.ds(r0, L), :] = (_sigmoid(o_ref[0, pl.ds(r0, L), :]) * hh).astype(y_ref.dtype)
        return 0

    lax.fori_loop(0, nc, combine, 0, unroll=2)


def _mlstm(proj3, gcol, grow, norm_g, off_q, off_k, off_v, off_o, cast_weights):
    b, s, _ = proj3.shape
    w_specs, w_shapes = _cast_specs(cast_weights, b * M_HEADS, lambda i, h: i * M_HEADS + h)
    assert s % (M_CHUNK * M_CHUNKS_PER_ITER) == 0
    nc = s // M_CHUNK
    bq, bk = off_q // M_DK, off_k // M_DK
    bv, bo = off_v // M_DV, off_o // M_DV
    return pl.pallas_call(
        functools.partial(_mlstm_kernel, len(cast_weights)),
        grid=(b, M_HEADS),
        in_specs=[
            pl.BlockSpec((1, s, M_DK), lambda i, h: (i, 0, bq + h)),
            pl.BlockSpec((1, s, M_DK), lambda i, h: (i, 0, bk + h)),
            pl.BlockSpec((1, s, M_DV), lambda i, h: (i, 0, bv + h)),
            pl.BlockSpec((1, s, M_DV), lambda i, h: (i, 0, bo + h)),
            pl.BlockSpec((1, s, LANES), lambda i, h: (i, 0, 0)),
            pl.BlockSpec((1, LANES, s), lambda i, h: (i, 0, 0)),
            pl.BlockSpec((1, 1, M_DV), lambda i, h: (h, 0, 0)),
            *w_specs,
        ],
        out_specs=[pl.BlockSpec((1, s, M_DV), lambda i, h: (i, 0, h)), *w_specs],
        out_shape=[jax.ShapeDtypeStruct((b, s, M_HEADS * M_DV), BF16), *w_shapes],
        scratch_shapes=[
            pltpu.VMEM((N_DIR, s, M_DV), F32),
            pltpu.VMEM((N_DIR, s, LANES), F32),
            pltpu.VMEM((N_DIR, s, LANES), F32),
            pltpu.VMEM((N_DIR * nc, M_DK, M_DV), F32),
            pltpu.VMEM((N_DIR * nc, 8, M_DK), F32),
            pltpu.VMEM((N_DIR * nc, 8, LANES), F32),
            pltpu.VMEM((N_DIR * nc, 8, LANES), F32),
            pltpu.VMEM((N_DIR * nc, M_DK, M_DV), BF16),
            pltpu.VMEM((N_DIR * nc, 8, M_DK), F32),
            pltpu.VMEM((N_DIR * nc, 8, LANES), F32),
            pltpu.VMEM((N_DIR, M_DK, M_DV), F32),
        ],
        compiler_params=_cparams(("parallel", "arbitrary")),
        name="mlstm",
    )(proj3, proj3, proj3, proj3, gcol, grow, norm_g, *cast_weights)


def _delta_kernel(n_cast, q_ref, k_ref, v_ref, z_ref, cwq_ref, cwk_ref, cwv_ref, gc_ref, gr_ref, ng_ref, *refs):
    w_refs, y_ref, wo_refs = refs[:n_cast], refs[n_cast], refs[n_cast + 1:2 * n_cast + 1]
    xp_ref, qs_ref, ks_ref, vs_ref, kq_ref, n_ref, dec_ref, o_ref, st_ref = refs[2 * n_cast + 1:]
    _cast_blocks(w_refs, wo_refs)
    t = pl.program_id(0)
    head = jnp.minimum(t, pl.num_programs(0) - 2) % D_HEADS
    s = q_ref.shape[1]
    L = D_CHUNK
    G = D_GROUP
    nc = s // L
    ng = s // G
    halo = CONV_HALO
    pad = CONV_W // 2

    @pl.when(t == 0)
    def _():
        for r in (kq_ref, n_ref, dec_ref, o_ref):
            r[...] = jnp.zeros(r.shape, r.dtype)

    xp_ref[0:halo, :] = jnp.zeros((halo, D_DIM), F32)
    xp_ref[s + halo:s + 2 * halo, :] = jnp.zeros((halo, D_DIM), F32)

    def conv_into(x_ref, cw_ref, dst_ref, l2, mult):
        xp_ref[halo:s + halo, :] = x_ref[0]
        for blk in range(ng):
            base = halo + blk * G - pad
            acc = xp_ref[base:base + G, :] * cw_ref[0:1, :]
            for j in range(1, CONV_W):
                acc = acc + xp_ref[base + j:base + j + G, :] * cw_ref[j:j + 1, :]
            y = acc * _sigmoid(acc)
            if l2:
                y = y * (lax.rsqrt(jnp.sum(y * y, axis=-1, keepdims=True) + L2_EPS) * mult)
            dst_ref[blk * G:(blk + 1) * G, :] = y

    conv_into(q_ref, cwq_ref, qs_ref, True, D_DIM ** -0.5)
    conv_into(k_ref, cwk_ref, ks_ref, True, 1.0)
    conv_into(v_ref, cwv_ref, vs_ref, False, 1.0)

    st_ref[...] = jnp.zeros(st_ref.shape, F32)

    def scan_step(scan, j):
        for d in range(N_DIR):
            c = j if d == 0 else nc - 1 - j
            r0 = pl.multiple_of(c * L, L)
            idx = (scan + d) * nc + c
            st = st_ref[d]
            res = jnp.dot(kq_ref[idx], st.astype(BF16), preferred_element_type=F32)
            st_ref[d] = dec_ref[idx][0:1, :] * st + res[0:D_DIM] + n_ref[idx]
            o_ref[scan + d, pl.ds(r0, L), :] = o_ref[scan + d, pl.ds(r0, L), :] + res[D_DIM:D_DIM + L]

    tchunk = lax.broadcasted_iota(I32, (D_DIM, D_GROUP), 1) // D_CHUNK
    rr = lax.broadcasted_iota(I32, (G, G), 0)
    cc = lax.broadcasted_iota(I32, (G, G), 1)
    same = (rr // L) == (cc // L)
    pr = lax.broadcasted_iota(I32, (L, G), 0)
    pc = lax.broadcasted_iota(I32, (L, G), 1)
    eye_p = jnp.where((pc % L) == pr, 1.0, 0.0).astype(F32)

    def pack(m):
        out = m[0:L]
        for i in range(1, D_PER_GROUP):
            out = out + m[i * L:(i + 1) * L]
        return out

    def unpack(p):
        return jnp.where(same, jnp.concatenate([p] * D_PER_GROUP, axis=0), 0.0)

    scan_per_trip = nc // (ng // D_GROUPS_PER_ITER)

    def groups(fill, scan, trip):
        pending = iter(range(scan_per_trip))

        def scan_some(n):
            for _ in range(n):
                j = next(pending, None)
                if j is not None:
                    scan_step(scan, trip * scan_per_trip + j)

        chains = []
        for gi in [D_GROUPS_PER_ITER * trip + g for g in range(D_GROUPS_PER_ITER)]:
            r0 = pl.multiple_of(gi * G, G)
            kk_ = ks_ref[pl.ds(r0, G), :]
            qq_ = qs_ref[pl.ds(r0, G), :]
            vv_ = vs_ref[pl.ds(r0, G), :]
            kb = kk_.astype(BF16)
            kk = lax.dot_general(kb, kb, _NT, preferred_element_type=F32)
            qk = lax.dot_general(qq_.astype(BF16), kb, _NT, preferred_element_type=F32)
            gcol = gc_ref[0, pl.ds(r0, G), :]
            for d in range(N_DIR):
                ch = d * D_HEADS + head
                g_col = _lane_pick(gcol, CH_G + ch)
                b_col = _lane_pick(gcol, CH_B + ch)
                t_col = _lane_pick(gcol, CH_T + ch)
                g_row = gr_ref[0, pl.ds(CH_G + ch, 1), pl.ds(r0, G)]
                tri = (cc <= rr) if d == 0 else (cc >= rr)
                strict = (cc < rr) if d == 0 else (cc > rr)
                gam = jnp.exp(jnp.where(tri, jnp.where(same, g_col - g_row, -jnp.inf), -jnp.inf))
                x = jnp.where(strict, -(b_col * kk * gam), 0.0)
                eg = jnp.exp(g_col)
                chains.append(dict(
                    gi=gi, slot=fill + d, r0=r0, x=x, xp=pack(x), attn=(qk * gam).astype(BF16),
                    rhs=jnp.concatenate([b_col * vv_, (b_col * eg) * kk_], axis=1).astype(BF16),
                    qg=qq_ * eg, kdt=(kk_ * jnp.exp(t_col - g_col)).T, dec=jnp.exp(t_col)))
            scan_some(-(-(scan_per_trip - D_SCAN_STEPS_BETWEEN_STAGES) // D_GROUPS_PER_ITER))
        for c in chains:
            c["tp"] = eye_p + c["xp"]
            c["p"] = _bdot(c["xp"], c["x"])
        scan_some(1)
        for _it in range(4):
            for c in chains:
                res = _bdot(jnp.concatenate([c["tp"], c["p"]], axis=0), unpack(c["p"]))
                c["tp"] = c["tp"] + res[0:L]
                c["p"] = res[L:2 * L]
            scan_some(1)
        for c in chains:
            c["tp"] = c["tp"] + _bdot(c["tp"], unpack(c["p"]))
        scan_some(1)
        for c in chains:
            c["uw"] = _bdot(unpack(c["tp"]), c["rhs"]).astype(BF16)
        scan_some(1)
        for c in chains:
            au = jnp.dot(c["attn"], c["uw"], preferred_element_type=F32)
            o_ref[c["slot"], pl.ds(c["r0"], G), :] = au[:, 0:D_DIM]
            c["qe"] = (c["qg"] - au[:, D_DIM:2 * D_DIM]).astype(BF16)
        scan_some(1)
        for c in chains:
            for ci in range(D_PER_GROUP):
                idx = c["slot"] * nc + c["gi"] * D_PER_GROUP + ci
                ku = _bdot(jnp.where(tchunk == ci, c["kdt"], 0.0), c["uw"])
                n_ref[idx] = ku[:, 0:D_DIM]
                kq_ref[idx, 0:D_DIM, :] = (-ku[:, D_DIM:2 * D_DIM]).astype(BF16)
                kq_ref[idx, D_DIM:D_DIM + L, :] = c["qe"][ci * L:(ci + 1) * L]
                dec_ref[idx] = jnp.broadcast_to(c["dec"][ci * L:ci * L + 8], (8, D_DIM))
        scan_some(scan_per_trip)

    def finish(scan, gi):
        r0 = pl.multiple_of(gi * G, G)
        o = o_ref[scan, pl.ds(r0, G), :] + o_ref[scan + 1, pl.ds(r0, G), :]
        o = o * lax.rsqrt(jnp.mean(o * o, axis=-1, keepdims=True) + RMS_EPS) * ng_ref[...]
        z = z_ref[0, pl.ds(r0, G), :]
        y_ref[0, pl.ds(r0, G), :] = (o * (z * _sigmoid(z))).astype(y_ref.dtype)

    fill = (t % 2) * N_DIR
    scan = N_DIR - fill

    def group_batch(i, carry):
        groups(fill, scan, i)
        return carry

    def finish_batch(gi, carry):
        finish(scan, gi)
        return carry

    lax.fori_loop(0, ng // D_GROUPS_PER_ITER, group_batch, 0)
    lax.fori_loop(0, ng, finish_batch, 0, unroll=4)


def _delta(proj3, conv_w, gcol, grow, norm_g, off_q, off_k, off_v, off_z, cast_weights):
    b, s, _ = proj3.shape
    w = D_HEADS * D_DIM
    nc = s // D_CHUNK
    trips = s // (D_GROUP * D_GROUPS_PER_ITER)
    assert s % (D_GROUP * D_GROUPS_PER_ITER) == 0 and nc % trips == 0
    bq, bk, bv, bz = (o // D_DIM for o in (off_q, off_k, off_v, off_z))
    n_units = b * D_HEADS
    cur = lambda t: jnp.minimum(t, n_units - 1)
    prev = lambda t: jnp.maximum(t - 1, 0)
    col = lambda unit, blk: lambda t: (unit(t) // D_HEADS, 0, blk + unit(t) % D_HEADS)
    w_specs, w_shapes = _cast_specs(cast_weights, n_units, cur)
    return pl.pallas_call(
        functools.partial(_delta_kernel, len(cast_weights)),
        grid=(n_units + 1,),
        in_specs=[
            pl.BlockSpec((1, s, D_DIM), col(cur, bq)),
            pl.BlockSpec((1, s, D_DIM), col(cur, bk)),
            pl.BlockSpec((1, s, D_DIM), col(cur, bv)),
            pl.BlockSpec((1, s, D_DIM), col(prev, bz)),
            pl.BlockSpec((CONV_W, D_DIM), lambda t: (0, cur(t) % D_HEADS)),
            pl.BlockSpec((CONV_W, D_DIM), lambda t: (0, D_HEADS + cur(t) % D_HEADS)),
            pl.BlockSpec((CONV_W, D_DIM), lambda t: (0, 2 * D_HEADS + cur(t) % D_HEADS)),
            pl.BlockSpec((1, s, LANES), lambda t: (cur(t) // D_HEADS, 0, 0)),
            pl.BlockSpec((1, LANES, s), lambda t: (cur(t) // D_HEADS, 0, 0)),
            pl.BlockSpec((1, D_DIM), lambda t: (0, 0)),
            *w_specs,
        ],
        out_specs=[pl.BlockSpec((1, s, D_DIM), col(prev, 0)), *w_specs],
        out_shape=[jax.ShapeDtypeStruct((b, s, w), BF16), *w_shapes],
        scratch_shapes=[
            pltpu.VMEM((s + 2 * CONV_HALO, D_DIM), F32),
            pltpu.VMEM((s, D_DIM), F32),
            pltpu.VMEM((s, D_DIM), F32),
            pltpu.VMEM((s, D_DIM), F32),
            pltpu.VMEM((2 * N_DIR * nc, D_DIM + D_CHUNK, D_DIM), BF16),
            pltpu.VMEM((2 * N_DIR * nc, D_DIM, D_DIM), F32),
            pltpu.VMEM((2 * N_DIR * nc, 8, D_DIM), F32),
            pltpu.VMEM((2 * N_DIR, s, D_DIM), F32),
            pltpu.VMEM((N_DIR, D_DIM, D_DIM), F32),
        ],
        compiler_params=_cparams(("arbitrary",)),
        name="delta",
    )(proj3, proj3, proj3, proj3, conv_w, conv_w, conv_w, gcol, grow, norm_g, *cast_weights)


def _mix_kernel(n_gate_blocks, x_ref, ym_ref, yd_ref, *refs):
    gate_refs, (wbm_ref, wbd_ref, wo_ref, g2_ref, o_ref, h_ref) = refs[:2 * n_gate_blocks], refs[2 * n_gate_blocks:]
    gate = lambda rs: _sigmoid(jnp.concatenate([r[...] for r in rs], axis=1))
    a = jnp.dot(ym_ref[...], wbm_ref[...], preferred_element_type=F32)
    b = jnp.dot(yd_ref[...], wbd_ref[...], preferred_element_type=F32)
    mixed = gate(gate_refs[:n_gate_blocks]) * a + gate(gate_refs[n_gate_blocks:]) * b
    x = x_ref[...] + jnp.dot(mixed.astype(BF16), wo_ref[...], preferred_element_type=F32)
    o_ref[...] = x
    ms = jnp.mean(x * x, axis=-1, keepdims=True)
    h_ref[...] = (x * lax.rsqrt(ms + RMS_EPS) * g2_ref[...]).astype(BF16)


def _mix(x2, ym2, yd2, proj2, off_gates, wbm, wbd, wo, g2, tm):
    m, d = x2.shape
    const = lambda i: (0, 0)
    gw = math.gcd(off_gates, d)
    gate_specs = [pl.BlockSpec((tm, gw), functools.partial(lambda i, c: (i, c), c=(off_gates + t * gw) // gw))
                  for t in range(2 * d // gw)]
    return pl.pallas_call(
        functools.partial(_mix_kernel, d // gw),
        grid=(m // tm,),
        in_specs=[
            pl.BlockSpec((tm, d), lambda i: (i, 0)),
            pl.BlockSpec((tm, ym2.shape[1]), lambda i: (i, 0)),
            pl.BlockSpec((tm, yd2.shape[1]), lambda i: (i, 0)),
            *gate_specs,
            pl.BlockSpec(wbm.shape, const),
            pl.BlockSpec(wbd.shape, const),
            pl.BlockSpec(wo.shape, const),
            pl.BlockSpec((1, d), const),
        ],
        out_specs=[pl.BlockSpec((tm, d), lambda i: (i, 0)), pl.BlockSpec((tm, d), lambda i: (i, 0))],
        out_shape=[jax.ShapeDtypeStruct((m, d), F32), jax.ShapeDtypeStruct((m, d), BF16)],
        compiler_params=_cparams(("parallel",)),
        name="mix",
    )(x2, ym2, yd2, *([proj2] * len(gate_specs)), wbm, wbd, wo, g2)


def _ffn_kernel(n_slices, x_ref, h_ref, gf_ref, w1_ref, w2_ref, o_ref, acc_ref):
    f = pl.program_id(1)
    last = n_slices - 1

    def ffn_part(sl):
        hid = jnp.dot(h_ref[sl, :], w1_ref[...], preferred_element_type=F32)
        act = jnp.square(jnp.maximum(hid, 0.0)).astype(BF16)
        return jnp.dot(act, w2_ref[...], preferred_element_type=F32)

    @pl.when(jnp.logical_and(f == 0, f < last))
    def _():
        acc_ref[...] = ffn_part(slice(None))

    @pl.when(jnp.logical_and(f > 0, f < last))
    def _():
        acc_ref[...] += ffn_part(slice(None))

    @pl.when(f == last)
    def _():
        rows = x_ref.shape[0] // EDGE_STEP_PIECES
        for r in range(EDGE_STEP_PIECES):
            sl = slice(r * rows, (r + 1) * rows)
            y = ffn_part(sl) if last == 0 else acc_ref[sl, :] + ffn_part(sl)
            x = x_ref[sl, :] + y
            ms = jnp.mean(x * x, axis=-1, keepdims=True)
            o_ref[sl, :] = x * lax.rsqrt(ms + RMS_EPS) * gf_ref[...]


def _ffn(x2, h2, gf, w1, w2, tm, tf):
    m, d = x2.shape
    dff = w1.shape[1]
    return pl.pallas_call(
        functools.partial(_ffn_kernel, dff // tf),
        grid=(m // tm, dff // tf),
        in_specs=[
            pl.BlockSpec((tm, d), lambda i, f: (i, 0)),
            pl.BlockSpec((tm, d), lambda i, f: (i, 0)),
            pl.BlockSpec((1, d), lambda i, f: (0, 0)),
            pl.BlockSpec((d, tf), lambda i, f: (0, f)),
            pl.BlockSpec((tf, d), lambda i, f: (f, 0)),
        ],
        out_specs=pl.BlockSpec((tm, d), lambda i, f: (i, 0)),
        out_shape=jax.ShapeDtypeStruct((m, d), F32),
        scratch_shapes=[pltpu.VMEM((tm, d), F32)],
        compiler_params=_cparams(("parallel", "arbitrary")),
        name="ffn",
    )(x2, h2, gf, w1, w2)


def _pick_tile(n, candidates):
    for c in candidates:
        if n % c == 0:
            return c
    raise ValueError(f"no tile in {candidates} divides {n}")


def _layer(x, norm1_g, w_in, i_bias, f_bias, m_norm_g, conv_w, a_log, dt_bias, d_norm_g,
           w_bm, w_bd, w_out, norm2_g, w_ff1, w_ff2, out_g):
    b, s, d = x.shape
    m = b * s
    mw, mqk, dw = M_HEADS * M_DV, M_HEADS * M_DK, D_HEADS * D_DIM
    nm, nd = N_DIR * M_HEADS, N_DIR * D_HEADS
    splits = (mqk, mqk, mw, mw, nm, nm, 3 * dw, dw, nd, nd, 2 * d)
    bounds = [0]
    for w_ in splits:
        bounds.append(bounds[-1] + w_)
    wt = w_in.T.astype(BF16)
    rows = lambda first, last: wt[bounds[first]:bounds[last + 1]]
    seg_bounds = [(bounds[0], bounds[4]), (bounds[6], bounds[8]), (bounds[10], bounds[11])]
    n_small = 2 * nm + 2 * nd
    wt_small = jnp.concatenate([rows(4, 5), rows(8, 9), jnp.zeros((LANES - n_small, d), BF16)], axis=0)
    zpad = lambda n: jnp.zeros((n,), F32)
    bias_row = jnp.concatenate([i_bias.reshape(-1), f_bias.reshape(-1), dt_bias.reshape(-1), zpad(LANES - CH_B)])
    alog_row = jnp.concatenate([zpad(CH_G), a_log.reshape(-1), zpad(LANES - CH_B)])
    gate_params = jnp.concatenate([bias_row[None], alog_row[None], jnp.zeros((6, LANES), F32)], axis=0)

    x2 = x.reshape(m, d)
    tm = _pick_tile(m, (512, 256, 128))
    tn = next(c for c in (1024, 512, 256) if all((hi - lo) % c == 0 for lo, hi in seg_bounds))
    proj2, smalls2 = _inproj(x2, norm1_g.reshape(1, d), wt, wt_small, seg_bounds,
                             _pick_tile(m, (1024, 512, 256)), tn)
    proj3 = proj2.reshape(b, s, -1)
    gcol, grow = _gate_prep(smalls2.reshape(b, s, LANES), gate_params)

    off_d = 2 * mqk + 2 * mw
    off_g = off_d + 4 * dw
    ym, w_bm16, w_bd16, w_out16 = _mlstm(proj3, gcol, grow, m_norm_g.reshape(M_HEADS, 1, M_DV),
                                         0, mqk, 2 * mqk, 2 * mqk + mw, (w_bm, w_bd, w_out))
    yd, w_ff1_16, w_ff2_16 = _delta(proj3, conv_w, gcol, grow, d_norm_g.reshape(1, D_DIM),
                                    off_d, off_d + dw, off_d + 2 * dw, off_d + 3 * dw, (w_ff1, w_ff2))

    x1, h2 = _mix(x2, ym.reshape(m, mw), yd.reshape(m, dw), proj2, off_g, w_bm16, w_bd16, w_out16,
                  norm2_g.reshape(1, d), _pick_tile(m, (256, 128)))
    tf = _pick_tile(w_ff1.shape[1], (1024, 512, 256))
    out = _ffn(x1, h2, out_g.reshape(1, d), w_ff1_16, w_ff2_16, tm, tf)
    return out.reshape(b, s, d)


def kernel(x, norm1_g, w_in, mlstm_i_bias, mlstm_f_bias, mlstm_norm_g, delta_conv_w, delta_a_log,
           delta_dt_bias, delta_norm_g, w_branch_m, w_branch_d, w_out, norm2_g, w_ff1, w_ff2, norm_f_g):
    depth = w_in.shape[0]
    assert depth == 1, "the fused FFN + final-norm epilogue assumes a single layer"
    return _layer(x, norm1_g[0], w_in[0], mlstm_i_bias[0], mlstm_f_bias[0], mlstm_norm_g[0], delta_conv_w[0],
                  delta_a_log[0], delta_dt_bias[0], delta_norm_g[0], w_branch_m[0], w_branch_d[0], w_out[0],
                  norm2_g[0], w_ff1[0], w_ff2[0], norm_f_g)
```

```python
import functools
import math

import jax
import jax.numpy as jnp
import numpy as np
from jax import lax
from jax.experimental import pallas as pl
from jax.experimental.pallas import tpu as pltpu

F32 = jnp.float32
BF16 = jnp.bfloat16
I32 = jnp.int32

N_DIR = 2
M_HEADS = 4
M_DK = 128
M_DV = 256
D_HEADS = 8
D_DIM = 128
CONV_W = 5
RMS_EPS = 1e-6
L2_EPS = 1e-6

LANES = 128
BF16_SUBLANES = 16
M_CHUNK = 256
M_CHUNKS_PER_ITER = 4
D_CHUNK = 64
D_GROUP = 256
D_PER_GROUP = D_GROUP // D_CHUNK
D_GROUPS_PER_ITER = 4
D_SCAN_STEPS_BETWEEN_STAGES = 8
GATE_ROWS = 256
GATE_BLOCKS_PER_STEP = 4
EDGE_STEP_PIECES = 4
CONV_HALO = 8

CH_I = 0
CH_F = 8
CH_G = 16
CH_B = 32
CH_T = 48

VMEM_LIMIT = 56 * 1024 * 1024

_NT = (((1,), (1,)), ((), ()))


def _cparams(sem):
    return pltpu.CompilerParams(dimension_semantics=sem, vmem_limit_bytes=VMEM_LIMIT)


def _bdot(a, b):
    return jnp.dot(a.astype(BF16), b.astype(BF16), preferred_element_type=F32)


def _bdot_nt(a, b):
    return lax.dot_general(a.astype(BF16), b.astype(BF16), _NT, preferred_element_type=F32)


def _sigmoid(x):
    return 1.0 / (1.0 + jnp.exp(-x))


def _softplus(x):
    return jnp.maximum(x, 0.0) + jnp.log1p(jnp.exp(-jnp.abs(x)))


def _cast_blocks(w_refs, o_refs):
    for w_ref, o_ref in zip(w_refs, o_refs):
        o_ref[...] = w_ref[...].astype(o_ref.dtype)


def _cast_specs(weights, n_steps, step_of):
    specs, shapes = [], []
    for w in weights:
        rows, cols = w.shape
        assert rows % (n_steps * BF16_SUBLANES) == 0
        specs.append(pl.BlockSpec((rows // n_steps, cols), lambda *g: (step_of(*g), 0)))
        shapes.append(jax.ShapeDtypeStruct((rows, cols), BF16))
    return specs, shapes


def _lane_pick(x, ch):
    lane = lax.broadcasted_iota(I32, x.shape, 1)
    return jnp.sum(jnp.where(lane == ch, x, 0.0), axis=1, keepdims=True)


def _inproj_first_kernel(n_small, x_ref, g_ref, w_ref, *refs):
    small_refs, (o_ref, os_ref, hn_ref, wb_ref, ws_ref) = refs[:n_small], refs[n_small:]

    @pl.when(pl.program_id(0) == 0)
    def _():
        wb_ref[...] = w_ref[...].astype(BF16)
        ws_ref[...] = jnp.zeros(ws_ref.shape, BF16)
        off = 0
        for s_ref in small_refs:
            ws_ref[off:off + s_ref.shape[0], :] = s_ref[...].astype(BF16)
            off += s_ref.shape[0]

    rows = x_ref.shape[0] // EDGE_STEP_PIECES
    for r in range(EDGE_STEP_PIECES):
        sl = slice(r * rows, (r + 1) * rows)
        x = x_ref[sl, :]
        ms = jnp.mean(x * x, axis=-1, keepdims=True)
        hn = (x * lax.rsqrt(ms + RMS_EPS) * g_ref[...]).astype(BF16)
        hn_ref[sl, :] = hn
        os_ref[sl, :] = lax.dot_general(hn, ws_ref[...], _NT, preferred_element_type=F32)
        o_ref[sl, :] = lax.dot_general(hn, wb_ref[...], _NT, preferred_element_type=F32)


def _inproj_rest_kernel(h_ref, w_ref, p_ref, o_ref, wb_ref):
    del p_ref
    @pl.when(pl.program_id(1) == 0)
    def _():
        wb_ref[...] = w_ref[...].astype(BF16)

    o_ref[...] = lax.dot_general(h_ref[...], wb_ref[...], _NT, preferred_element_type=F32)


def _inproj(x2, g, wt, seg_bounds, small_bounds, tm_first, tm, tn):
    m, d = x2.shape
    assert seg_bounds[0][0] == 0
    starts = [0]
    for lo, hi in seg_bounds:
        starts.append(starts[-1] + (hi - lo) // tn)
    n_tiles = starts[-1]

    def w_row(j):
        row = seg_bounds[0][0] + j * tn
        for k in range(1, len(seg_bounds)):
            row = jnp.where(j >= starts[k], seg_bounds[k][0] + (j - starts[k]) * tn, row)
        return pl.multiple_of(row, BF16_SUBLANES), 0

    proj0, smalls, hn = pl.pallas_call(
        functools.partial(_inproj_first_kernel, len(small_bounds)),
        grid=(m // tm_first,),
        in_specs=[
            pl.BlockSpec((tm_first, d), lambda i: (i, 0)),
            pl.BlockSpec((1, d), lambda i: (0, 0)),
            pl.BlockSpec((tn, d), lambda i: (0, 0)),
            *[pl.BlockSpec((pl.Element(hi - lo), pl.Element(d)), functools.partial(lambda i, lo: (lo, 0), lo=lo))
              for lo, hi in small_bounds],
        ],
        out_specs=[
            pl.BlockSpec((tm_first, tn), lambda i: (i, 0)),
            pl.BlockSpec((tm_first, LANES), lambda i: (i, 0)),
            pl.BlockSpec((tm_first, d), lambda i: (i, 0)),
        ],
        out_shape=[jax.ShapeDtypeStruct((m, n_tiles * tn), F32), jax.ShapeDtypeStruct((m, LANES), F32),
                   jax.ShapeDtypeStruct((m, d), BF16)],
        scratch_shapes=[pltpu.VMEM((tn, d), BF16), pltpu.VMEM((LANES, d), BF16)],
        compiler_params=_cparams(("arbitrary",)),
        name="inproj_first",
    )(x2, g, wt, *([wt] * len(small_bounds)))
    proj = pl.pallas_call(
        _inproj_rest_kernel,
        grid=(n_tiles - 1, m // tm),
        in_specs=[
            pl.BlockSpec((tm, d), lambda j, i: (i, 0)),
            pl.BlockSpec((pl.Element(tn), pl.Element(d)), lambda j, i: w_row(j + 1)),
            pl.BlockSpec(memory_space=pl.ANY),
        ],
        out_specs=pl.BlockSpec((tm, tn), lambda j, i: (i, j + 1)),
        out_shape=jax.ShapeDtypeStruct((m, n_tiles * tn), F32),
        scratch_shapes=[pltpu.VMEM((tn, d), BF16)],
        input_output_aliases={2: 0},
        compiler_params=_cparams(("arbitrary", "arbitrary")),
        name="inproj_rest",
    )(hn, wt, proj0)
    return proj, smalls


def _cumsum_matrices():
    rr, cc = np.indices((GATE_ROWS, GATE_ROWS))
    low, upp = cc <= rr, cc >= rr
    same_d = (rr // D_CHUNK) == (cc // D_CHUNK)
    same_m = (rr // M_CHUNK) == (cc // M_CHUNK)
    mats = np.concatenate([
        low & same_m,
        upp & same_m,
        low & same_d,
        upp & same_d,
        same_d,
    ], axis=0)
    return jnp.asarray(mats.astype(np.float32), dtype=BF16)


def _gate_kernel(sm_ref, par_ref, mats_ref, col_ref, row_ref):
    r = GATE_ROWS
    lane = lax.broadcasted_iota(I32, (r, LANES), 1)
    neg_a = -jnp.exp(par_ref[1:2, :])
    xes, x3s = [], []
    for blk in range(GATE_BLOCKS_PER_STEP):
        x = sm_ref[0, blk * r:(blk + 1) * r, :] + par_ref[0:1, :]
        logf = -_softplus(-x)
        g = neg_a * _softplus(x)
        beta = _sigmoid(x)
        xe = jnp.where(lane < CH_F, x,
                       jnp.where(lane < CH_G, logf,
                                 jnp.where(lane < CH_B, g,
                                           jnp.where(lane < CH_T, beta, 0.0))))
        hi = xe.astype(BF16)
        r1 = xe - hi.astype(F32)
        mid = r1.astype(BF16)
        lo = (r1 - mid.astype(F32)).astype(BF16)
        xes.append(xe)
        x3s.append(jnp.concatenate([hi, mid, lo], axis=1))
    y3s = [jnp.dot(mats_ref[...], x3, preferred_element_type=F32) for x3 in x3s]
    half_m = CH_F + M_HEADS
    half_d = CH_G + D_HEADS
    for blk, (xe, y3) in enumerate(zip(xes, y3s)):
        y = y3[:, 0:LANES] + y3[:, LANES:2 * LANES] + y3[:, 2 * LANES:3 * LANES]
        pm, sm_, pd, sd, td = (y[i * r:(i + 1) * r] for i in range(5))
        out = jnp.where(lane < CH_F, xe,
              jnp.where(lane < half_m, pm,
              jnp.where(lane < CH_G, sm_,
              jnp.where(lane < half_d, pd,
              jnp.where(lane < CH_B, sd,
              jnp.where(lane < CH_T, xe,
              jnp.where(lane < CH_T + N_DIR * D_HEADS, pltpu.roll(td, CH_T - CH_G, axis=1), 0.0)))))))
        col_ref[0, blk * r:(blk + 1) * r, :] = out
        row_ref[0, :, blk * r:(blk + 1) * r] = out.T


def _gate_prep(smalls3, params):
    b, s, _ = smalls3.shape
    r = GATE_ROWS * GATE_BLOCKS_PER_STEP
    assert s % r == 0
    return pl.pallas_call(
        _gate_kernel,
        grid=(b, s // r),
        in_specs=[
            pl.BlockSpec((1, r, LANES), lambda i, j: (i, j, 0)),
            pl.BlockSpec((8, LANES), lambda i, j: (0, 0)),
            pl.BlockSpec((5 * GATE_ROWS, GATE_ROWS), lambda i, j: (0, 0)),
        ],
        out_specs=[
            pl.BlockSpec((1, r, LANES), lambda i, j: (i, j, 0)),
            pl.BlockSpec((1, LANES, r), lambda i, j: (i, 0, j)),
        ],
        out_shape=[jax.ShapeDtypeStruct((b, s, LANES), F32), jax.ShapeDtypeStruct((b, LANES, s), F32)],
        compiler_params=_cparams(("parallel", "parallel")),
        name="gateprep",
    )(smalls3, params, _cumsum_matrices())


def _mlstm_kernel(n_cast, q_ref, k_ref, v_ref, o_ref, gc_ref, gr_ref, ng_ref, *refs):
    w_refs, y_ref, wo_refs = refs[:n_cast], refs[n_cast], refs[n_cast + 1:2 * n_cast + 1]
    hn_ref, den_ref, mi_ref, cu_ref, nu_ref, bl_ref, ml_ref, cs_ref, ns_ref, ms_ref, c_ref = refs[2 * n_cast + 1:]
    _cast_blocks(w_refs, wo_refs)
    head = pl.program_id(1)
    s = q_ref.shape[1]
    L = M_CHUNK
    nc = s // L
    scale = M_DK ** -0.5
    rr = lax.broadcasted_iota(I32, (L, L), 0)
    cc = lax.broadcasted_iota(I32, (L, L), 1)

    def bcast8(x11):
        return jnp.broadcast_to(x11, (8, LANES))

    def local(cs):
        chains = []
        for c in cs:
            r0 = pl.multiple_of(c * L, L)
            q = q_ref[0, pl.ds(r0, L), :]
            k = k_ref[0, pl.ds(r0, L), :] * scale
            vb = v_ref[0, pl.ds(r0, L), :].astype(BF16)
            kb = k.astype(BF16)
            qk = lax.dot_general(q.astype(BF16), kb, _NT, preferred_element_type=F32)
            k_t = k.T
            gcol = gc_ref[0, pl.ds(r0, L), :]
            for d in range(N_DIR):
                ch_i = CH_I + d * M_HEADS + head
                ch_f = CH_F + d * M_HEADS + head
                bc_col = _lane_pick(gcol, ch_f)
                i_row = gr_ref[0, pl.ds(ch_i, 1), pl.ds(r0, L)]
                bc_row = gr_ref[0, pl.ds(ch_f, 1), pl.ds(r0, L)]
                a_row = i_row - bc_row
                mask = (cc <= rr) if d == 0 else (cc >= rr)
                log_intra = jnp.where(mask, bc_col + a_row, -jnp.inf)
                m_intra = jnp.max(log_intra, axis=1, keepdims=True)
                p = qk * jnp.exp(log_intra - m_intra)
                b_last = bc_row[:, L - 1:L] if d == 0 else bc_row[:, 0:1]
                log_state = b_last + a_row
                m_loc = jnp.max(log_state, axis=1, keepdims=True)
                w_state = jnp.exp(log_state - m_loc)
                chains.append(dict(
                    idx=d * nc + c, d=d, r0=r0, vb=vb, kb=kb, pb=p.astype(BF16),
                    den=jnp.sum(p, axis=1, keepdims=True), m_intra=m_intra,
                    kw=(k_t * w_state).astype(BF16), w8=jnp.broadcast_to(w_state, (8, L)).astype(BF16),
                    b_last=b_last, m_loc=m_loc))
        for c in chains:
            hn_ref[c["d"], pl.ds(c["r0"], L), :] = jnp.dot(c["pb"], c["vb"], preferred_element_type=F32)
            den_ref[c["d"], pl.ds(c["r0"], L), :] = jnp.broadcast_to(c["den"], (L, LANES))
            mi_ref[c["d"], pl.ds(c["r0"], L), :] = jnp.broadcast_to(c["m_intra"], (L, LANES))
        for c in chains:
            cu_ref[c["idx"]] = jnp.dot(c["kw"], c["vb"], preferred_element_type=F32)
            nu_ref[c["idx"]] = jnp.dot(c["w8"], c["kb"], preferred_element_type=F32)
            bl_ref[c["idx"]] = bcast8(c["b_last"])
            ml_ref[c["idx"]] = bcast8(c["m_loc"])

    def local_trip(i, _):
        local([M_CHUNKS_PER_ITER * i + t for t in range(M_CHUNKS_PER_ITER)])
        return 0

    lax.fori_loop(0, nc // M_CHUNKS_PER_ITER, local_trip, 0)

    c_ref[...] = jnp.zeros(c_ref.shape, F32)

    def scan(j, carry):
        out = []
        for d in range(N_DIR):
            m_st, n_st = carry[d]
            idx = d * nc + (j if d == 0 else nc - 1 - j)
            c_st = c_ref[d]
            cs_ref[idx] = c_st.astype(BF16)
            ns_ref[idx] = n_st
            ms_ref[idx] = m_st
            bl = bl_ref[idx]
            ml = ml_ref[idx]
            m_new = jnp.maximum(bl + m_st, ml)
            decay = jnp.exp(bl + m_st - m_new)
            gain = jnp.exp(ml - m_new)
            c_ref[d] = decay[0:1, 0:1] * c_st + gain[0:1, 0:1] * cu_ref[idx]
            out.append((m_new, decay * n_st + gain * nu_ref[idx]))
        return tuple(out)

    m0 = jnp.full((8, LANES), -jnp.inf, F32)
    n0 = jnp.zeros((8, M_DK), F32)
    lax.fori_loop(0, nc, scan, ((m0, n0), (m0, n0)))

    def wide(x):
        return jnp.concatenate([x] * (M_DV // LANES), axis=1)

    def combine(c, _):
        r0 = pl.multiple_of(c * L, L)
        q = q_ref[0, pl.ds(r0, L), :]
        qb = q.astype(BF16)
        gcol = gc_ref[0, pl.ds(r0, L), :]
        hh = None
        for d in range(N_DIR):
            idx = d * nc + c
            bc = jnp.broadcast_to(_lane_pick(gcol, CH_F + d * M_HEADS + head), (L, LANES))
            den_i = den_ref[d, pl.ds(r0, L), :]
            m_i = mi_ref[d, pl.ds(r0, L), :]
            log_inter = bc + ms_ref[idx][0:1, :]
            m_row = jnp.maximum(log_inter, m_i)
            w_i = jnp.exp(m_i - m_row)
            w_x = jnp.exp(log_inter - m_row)
            q_c = jnp.dot(qb, cs_ref[idx], preferred_element_type=F32)
            q_n = jnp.broadcast_to(jnp.sum(q * ns_ref[idx][0:1, :], axis=1, keepdims=True), (L, LANES))
            num = wide(w_i) * hn_ref[d, pl.ds(r0, L), :] + wide(w_x) * q_c
            den = w_i * den_i + w_x * q_n
            h_d = num * wide(1.0 / jnp.maximum(jnp.abs(den), jnp.exp(-m_row)))
            hh = h_d if hh is None else hh + h_d
        hh = hh * lax.rsqrt(jnp.mean(hh * hh, axis=-1, keepdims=True) + RMS_EPS)
        hh = hh * ng_ref[0]
        y_ref[0, pl.ds(r0, L), :] = (_sigmoid(o_ref[0, pl.ds(r0, L), :]) * hh).astype(y_ref.dtype)
        return 0

    lax.fori_loop(0, nc, combine, 0, unroll=2)


def _mlstm(proj3, gcol, grow, norm_g, off_q, off_k, off_v, off_o, cast_weights):
    b, s, _ = proj3.shape
    w_specs, w_shapes = _cast_specs(cast_weights, b * M_HEADS, lambda i, h: i * M_HEADS + h)
    assert s % (M_CHUNK * M_CHUNKS_PER_ITER) == 0
    nc = s // M_CHUNK
    bq, bk = off_q // M_DK, off_k // M_DK
    bv, bo = off_v // M_DV, off_o // M_DV
    return pl.pallas_call(
        functools.partial(_mlstm_kernel, len(cast_weights)),
        grid=(b, M_HEADS),
        in_specs=[
            pl.BlockSpec((1, s, M_DK), lambda i, h: (i, 0, bq + h)),
            pl.BlockSpec((1, s, M_DK), lambda i, h: (i, 0, bk + h)),
            pl.BlockSpec((1, s, M_DV), lambda i, h: (i, 0, bv + h)),
            pl.BlockSpec((1, s, M_DV), lambda i, h: (i, 0, bo + h)),
            pl.BlockSpec((1, s, LANES), lambda i, h: (i, 0, 0)),
            pl.BlockSpec((1, LANES, s), lambda i, h: (i, 0, 0)),
            pl.BlockSpec((1, 1, M_DV), lambda i, h: (h, 0, 0)),
            *w_specs,
        ],
        out_specs=[pl.BlockSpec((1, s, M_DV), lambda i, h: (i, 0, h)), *w_specs],
        out_shape=[jax.ShapeDtypeStruct((b, s, M_HEADS * M_DV), BF16), *w_shapes],
        scratch_shapes=[
            pltpu.VMEM((N_DIR, s, M_DV), F32),
            pltpu.VMEM((N_DIR, s, LANES), F32),
            pltpu.VMEM((N_DIR, s, LANES), F32),
            pltpu.VMEM((N_DIR * nc, M_DK, M_DV), F32),
            pltpu.VMEM((N_DIR * nc, 8, M_DK), F32),
            pltpu.VMEM((N_DIR * nc, 8, LANES), F32),
            pltpu.VMEM((N_DIR * nc, 8, LANES), F32),
            pltpu.VMEM((N_DIR * nc, M_DK, M_DV), BF16),
            pltpu.VMEM((N_DIR * nc, 8, M_DK), F32),
            pltpu.VMEM((N_DIR * nc, 8, LANES), F32),
            pltpu.VMEM((N_DIR, M_DK, M_DV), F32),
        ],
        compiler_params=_cparams(("parallel", "arbitrary")),
        name="mlstm",
    )(proj3, proj3, proj3, proj3, gcol, grow, norm_g, *cast_weights)


def _delta_kernel(n_cast, q_ref, k_ref, v_ref, z_ref, cwq_ref, cwk_ref, cwv_ref, gc_ref, gr_ref, ng_ref, *refs):
    w_refs, y_ref, wo_refs = refs[:n_cast], refs[n_cast], refs[n_cast + 1:2 * n_cast + 1]
    xp_ref, qs_ref, ks_ref, vs_ref, kq_ref, n_ref, dec_ref, o_ref, st_ref = refs[2 * n_cast + 1:]
    _cast_blocks(w_refs, wo_refs)
    t = pl.program_id(0)
    head = jnp.minimum(t, pl.num_programs(0) - 2) % D_HEADS
    s = q_ref.shape[1]
    L = D_CHUNK
    G = D_GROUP
    nc = s // L
    ng = s // G
    halo = CONV_HALO
    pad = CONV_W // 2

    @pl.when(t == 0)
    def _():
        for r in (kq_ref, n_ref, dec_ref, o_ref):
            r[...] = jnp.zeros(r.shape, r.dtype)

    xp_ref[0:halo, :] = jnp.zeros((halo, D_DIM), F32)
    xp_ref[s + halo:s + 2 * halo, :] = jnp.zeros((halo, D_DIM), F32)

    def conv_into(x_ref, cw_ref, dst_ref, l2, mult):
        xp_ref[halo:s + halo, :] = x_ref[0]
        for blk in range(ng):
            base = halo + blk * G - pad
            acc = xp_ref[base:base + G, :] * cw_ref[0:1, :]
            for j in range(1, CONV_W):
                acc = acc + xp_ref[base + j:base + j + G, :] * cw_ref[j:j + 1, :]
            y = acc * _sigmoid(acc)
            if l2:
                y = y * (lax.rsqrt(jnp.sum(y * y, axis=-1, keepdims=True) + L2_EPS) * mult)
            dst_ref[blk * G:(blk + 1) * G, :] = y

    conv_into(q_ref, cwq_ref, qs_ref, True, D_DIM ** -0.5)
    conv_into(k_ref, cwk_ref, ks_ref, True, 1.0)
    conv_into(v_ref, cwv_ref, vs_ref, False, 1.0)

    st_ref[...] = jnp.zeros(st_ref.shape, F32)

    def scan_step(scan, j):
        for d in range(N_DIR):
            c = j if d == 0 else nc - 1 - j
            r0 = pl.multiple_of(c * L, L)
            idx = (scan + d) * nc + c
            st = st_ref[d]
            res = jnp.dot(kq_ref[idx], st.astype(BF16), preferred_element_type=F32)
            st_ref[d] = dec_ref[idx][0:1, :] * st + res[0:D_DIM] + n_ref[idx]
            o_ref[scan + d, pl.ds(r0, L), :] = o_ref[scan + d, pl.ds(r0, L), :] + res[D_DIM:D_DIM + L]

    tchunk = lax.broadcasted_iota(I32, (D_DIM, D_GROUP), 1) // D_CHUNK
    rr = lax.broadcasted_iota(I32, (G, G), 0)
    cc = lax.broadcasted_iota(I32, (G, G), 1)
    same = (rr // L) == (cc // L)
    pr = lax.broadcasted_iota(I32, (L, G), 0)
    pc = lax.broadcasted_iota(I32, (L, G), 1)
    eye_p = jnp.where((pc % L) == pr, 1.0, 0.0).astype(F32)

    def pack(m):
        out = m[0:L]
        for i in range(1, D_PER_GROUP):
            out = out + m[i * L:(i + 1) * L]
        return out

    def unpack(p):
        return jnp.where(same, jnp.concatenate([p] * D_PER_GROUP, axis=0), 0.0)

    scan_per_trip = nc // (ng // D_GROUPS_PER_ITER)

    def groups(fill, scan, trip):
        pending = iter(range(scan_per_trip))

        def scan_some(n):
            for _ in range(n):
                j = next(pending, None)
                if j is not None:
                    scan_step(scan, trip * scan_per_trip + j)

        chains = []
        for gi in [D_GROUPS_PER_ITER * trip + g for g in range(D_GROUPS_PER_ITER)]:
            r0 = pl.multiple_of(gi * G, G)
            kk_ = ks_ref[pl.ds(r0, G), :]
            qq_ = qs_ref[pl.ds(r0, G), :]
            vv_ = vs_ref[pl.ds(r0, G), :]
            kb = kk_.astype(BF16)
            kk = lax.dot_general(kb, kb, _NT, preferred_element_type=F32)
            qk = lax.dot_general(qq_.astype(BF16), kb, _NT, preferred_element_type=F32)
            gcol = gc_ref[0, pl.ds(r0, G), :]
            for d in range(N_DIR):
                ch = d * D_HEADS + head
                g_col = _lane_pick(gcol, CH_G + ch)
                b_col = _lane_pick(gcol, CH_B + ch)
                t_col = _lane_pick(gcol, CH_T + ch)
                g_row = gr_ref[0, pl.ds(CH_G + ch, 1), pl.ds(r0, G)]
                tri = (cc <= rr) if d == 0 else (cc >= rr)
                strict = (cc < rr) if d == 0 else (cc > rr)
                gam = jnp.exp(jnp.where(tri, jnp.where(same, g_col - g_row, -jnp.inf), -jnp.inf))
                x = jnp.where(strict, -(b_col * kk * gam), 0.0)
                eg = jnp.exp(g_col)
                chains.append(dict(
                    gi=gi, slot=fill + d, r0=r0, x=x, xp=pack(x), attn=(qk * gam).astype(BF16),
                    rhs=jnp.concatenate([b_col * vv_, (b_col * eg) * kk_], axis=1).astype(BF16),
                    qg=qq_ * eg, kdt=(kk_ * jnp.exp(t_col - g_col)).T, dec=jnp.exp(t_col)))
            scan_some(-(-(scan_per_trip - D_SCAN_STEPS_BETWEEN_STAGES) // D_GROUPS_PER_ITER))
        for c in chains:
            c["tp"] = eye_p + c["xp"]
            c["p"] = _bdot(c["xp"], c["x"])
        scan_some(1)
        for _it in range(4):
            for c in chains:
                res = _bdot(jnp.concatenate([c["tp"], c["p"]], axis=0), unpack(c["p"]))
                c["tp"] = c["tp"] + res[0:L]
                c["p"] = res[L:2 * L]
            scan_some(1)
        for c in chains:
            c["tp"] = c["tp"] + _bdot(c["tp"], unpack(c["p"]))
        scan_some(1)
        for c in chains:
            c["uw"] = _bdot(unpack(c["tp"]), c["rhs"]).astype(BF16)
        scan_some(1)
        for c in chains:
            au = jnp.dot(c["attn"], c["uw"], preferred_element_type=F32)
            o_ref[c["slot"], pl.ds(c["r0"], G), :] = au[:, 0:D_DIM]
            c["qe"] = (c["qg"] - au[:, D_DIM:2 * D_DIM]).astype(BF16)
        scan_some(1)
        for c in chains:
            for ci in range(D_PER_GROUP):
                idx = c["slot"] * nc + c["gi"] * D_PER_GROUP + ci
                ku = _bdot(jnp.where(tchunk == ci, c["kdt"], 0.0), c["uw"])
                n_ref[idx] = ku[:, 0:D_DIM]
                kq_ref[idx, 0:D_DIM, :] = (-ku[:, D_DIM:2 * D_DIM]).astype(BF16)
                kq_ref[idx, D_DIM:D_DIM + L, :] = c["qe"][ci * L:(ci + 1) * L]
                dec_ref[idx] = jnp.broadcast_to(c["dec"][ci * L:ci * L + 8], (8, D_DIM))
        scan_some(scan_per_trip)

    def finish(scan, gi):
        r0 = pl.multiple_of(gi * G, G)
        o = o_ref[scan, pl.ds(r0, G), :] + o_ref[scan + 1, pl.ds(r0, G), :]
        o = o * lax.rsqrt(jnp.mean(o * o, axis=-1, keepdims=True) + RMS_EPS) * ng_ref[...]
        z = z_ref[0, pl.ds(r0, G), :]
        y_ref[0, pl.ds(r0, G), :] = (o * (z * _sigmoid(z))).astype(y_ref.dtype)

    fill = (t % 2) * N_DIR
    scan = N_DIR - fill

    def group_batch(i, carry):
        groups(fill, scan, i)
        return carry

    def finish_batch(gi, carry):
        finish(scan, gi)
        return carry

    lax.fori_loop(0, ng // D_GROUPS_PER_ITER, group_batch, 0)
    lax.fori_loop(0, ng, finish_batch, 0, unroll=4)


def _delta(proj3, conv_w, gcol, grow, norm_g, off_q, off_k, off_v, off_z, cast_weights):
    b, s, _ = proj3.shape
    w = D_HEADS * D_DIM
    nc = s // D_CHUNK
    trips = s // (D_GROUP * D_GROUPS_PER_ITER)
    assert s % (D_GROUP * D_GROUPS_PER_ITER) == 0 and nc % trips == 0
    bq, bk, bv, bz = (o // D_DIM for o in (off_q, off_k, off_v, off_z))
    n_units = b * D_HEADS
    cur = lambda t: jnp.minimum(t, n_units - 1)
    prev = lambda t: jnp.maximum(t - 1, 0)
    col = lambda unit, blk: lambda t: (unit(t) // D_HEADS, 0, blk + unit(t) % D_HEADS)
    w_specs, w_shapes = _cast_specs(cast_weights, n_units, cur)
    return pl.pallas_call(
        functools.partial(_delta_kernel, len(cast_weights)),
        grid=(n_units + 1,),
        in_specs=[
            pl.BlockSpec((1, s, D_DIM), col(cur, bq)),
            pl.BlockSpec((1, s, D_DIM), col(cur, bk)),
            pl.BlockSpec((1, s, D_DIM), col(cur, bv)),
            pl.BlockSpec((1, s, D_DIM), col(prev, bz)),
            pl.BlockSpec((CONV_W, D_DIM), lambda t: (0, cur(t) % D_HEADS)),
            pl.BlockSpec((CONV_W, D_DIM), lambda t: (0, D_HEADS + cur(t) % D_HEADS)),
            pl.BlockSpec((CONV_W, D_DIM), lambda t: (0, 2 * D_HEADS + cur(t) % D_HEADS)),
            pl.BlockSpec((1, s, LANES), lambda t: (cur(t) // D_HEADS, 0, 0)),
            pl.BlockSpec((1, LANES, s), lambda t: (cur(t) // D_HEADS, 0, 0)),
            pl.BlockSpec((1, D_DIM), lambda t: (0, 0)),
            *w_specs,
        ],
        out_specs=[pl.BlockSpec((1, s, D_DIM), col(prev, 0)), *w_specs],
        out_shape=[jax.ShapeDtypeStruct((b, s, w), BF16), *w_shapes],
        scratch_shapes=[
            pltpu.VMEM((s + 2 * CONV_HALO, D_DIM), F32),
            pltpu.VMEM((s, D_DIM), F32),
            pltpu.VMEM((s, D_DIM), F32),
            pltpu.VMEM((s, D_DIM), F32),
            pltpu.VMEM((2 * N_DIR * nc, D_DIM + D_CHUNK, D_DIM), BF16),
            pltpu.VMEM((2 * N_DIR * nc, D_DIM, D_DIM), F32),
            pltpu.VMEM((2 * N_DIR * nc, 8, D_DIM), F32),
            pltpu.VMEM((2 * N_DIR, s, D_DIM), F32),
            pltpu.VMEM((N_DIR, D_DIM, D_DIM), F32),
        ],
        compiler_params=_cparams(("arbitrary",)),
        name="delta",
    )(proj3, proj3, proj3, proj3, conv_w, conv_w, conv_w, gcol, grow, norm_g, *cast_weights)


def _mix_kernel(n_gate_blocks, x_ref, ym_ref, yd_ref, *refs):
    gate_refs, (wbm_ref, wbd_ref, wo_ref, g2_ref, o_ref, h_ref) = refs[:2 * n_gate_blocks], refs[2 * n_gate_blocks:]
    gate = lambda rs: _sigmoid(jnp.concatenate([r[...] for r in rs], axis=1))
    a = jnp.dot(ym_ref[...], wbm_ref[...], preferred_element_type=F32)
    b = jnp.dot(yd_ref[...], wbd_ref[...], preferred_element_type=F32)
    mixed = gate(gate_refs[:n_gate_blocks]) * a + gate(gate_refs[n_gate_blocks:]) * b
    x = x_ref[...] + jnp.dot(mixed.astype(BF16), wo_ref[...], preferred_element_type=F32)
    o_ref[...] = x
    ms = jnp.mean(x * x, axis=-1, keepdims=True)
    h_ref[...] = (x * lax.rsqrt(ms + RMS_EPS) * g2_ref[...]).astype(BF16)


def _mix(x2, ym2, yd2, proj2, off_gates, wbm, wbd, wo, g2, tm):
    m, d = x2.shape
    const = lambda i: (0, 0)
    gw = math.gcd(off_gates, d)
    gate_specs = [pl.BlockSpec((tm, gw), functools.partial(lambda i, c: (i, c), c=(off_gates + t * gw) // gw))
                  for t in range(2 * d // gw)]
    return pl.pallas_call(
        functools.partial(_mix_kernel, d // gw),
        grid=(m // tm,),
        in_specs=[
            pl.BlockSpec((tm, d), lambda i: (i, 0)),
            pl.BlockSpec((tm, ym2.shape[1]), lambda i: (i, 0)),
            pl.BlockSpec((tm, yd2.shape[1]), lambda i: (i, 0)),
            *gate_specs,
            pl.BlockSpec(wbm.shape, const),
            pl.BlockSpec(wbd.shape, const),
            pl.BlockSpec(wo.shape, const),
            pl.BlockSpec((1, d), const),
        ],
        out_specs=[pl.BlockSpec((tm, d), lambda i: (i, 0)), pl.BlockSpec((tm, d), lambda i: (i, 0))],
        out_shape=[jax.ShapeDtypeStruct((m, d), F32), jax.ShapeDtypeStruct((m, d), BF16)],
        compiler_params=_cparams(("parallel",)),
        name="mix",
    )(x2, ym2, yd2, *([proj2] * len(gate_specs)), wbm, wbd, wo, g2)


def _ffn_kernel(n_slices, x_ref, h_ref, gf_ref, w1_ref, w2_ref, o_ref, acc_ref):
    f = pl.program_id(1)
    last = n_slices - 1

    def ffn_part(sl):
        hid = jnp.dot(h_ref[sl, :], w1_ref[...], preferred_element_type=F32)
        act = jnp.square(jnp.maximum(hid, 0.0)).astype(BF16)
        return jnp.dot(act, w2_ref[...], preferred_element_type=F32)

    @pl.when(jnp.logical_and(f == 0, f < last))
    def _():
        acc_ref[...] = ffn_part(slice(None))

    @pl.when(jnp.logical_and(f > 0, f < last))
    def _():
        acc_ref[...] += ffn_part(slice(None))

    @pl.when(f == last)
    def _():
        rows = x_ref.shape[0] // EDGE_STEP_PIECES
        for r in range(EDGE_STEP_PIECES):
            sl = slice(r * rows, (r + 1) * rows)
            y = ffn_part(sl) if last == 0 else acc_ref[sl, :] + ffn_part(sl)
            x = x_ref[sl, :] + y
            ms = jnp.mean(x * x, axis=-1, keepdims=True)
            o_ref[sl, :] = x * lax.rsqrt(ms + RMS_EPS) * gf_ref[...]


def _ffn(x2, h2, gf, w1, w2, tm, tf):
    m, d = x2.shape
    dff = w1.shape[1]
    return pl.pallas_call(
        functools.partial(_ffn_kernel, dff // tf),
        grid=(m // tm, dff // tf),
        in_specs=[
            pl.BlockSpec((tm, d), lambda i, f: (i, 0)),
            pl.BlockSpec((tm, d), lambda i, f: (i, 0)),
            pl.BlockSpec((1, d), lambda i, f: (0, 0)),
            pl.BlockSpec((d, tf), lambda i, f: (0, f)),
            pl.BlockSpec((tf, d), lambda i, f: (f, 0)),
        ],
        out_specs=pl.BlockSpec((tm, d), lambda i, f: (i, 0)),
        out_shape=jax.ShapeDtypeStruct((m, d), F32),
        scratch_shapes=[pltpu.VMEM((tm, d), F32)],
        compiler_params=_cparams(("parallel", "arbitrary")),
        name="ffn",
    )(x2, h2, gf, w1, w2)


def _pick_tile(n, candidates):
    for c in candidates:
        if n % c == 0:
            return c
    raise ValueError(f"no tile in {candidates} divides {n}")


def _layer(x, norm1_g, w_in, i_bias, f_bias, m_norm_g, conv_w, a_log, dt_bias, d_norm_g,
           w_bm, w_bd, w_out, norm2_g, w_ff1, w_ff2, out_g):
    b, s, d = x.shape
    m = b * s
    mw, mqk, dw = M_HEADS * M_DV, M_HEADS * M_DK, D_HEADS * D_DIM
    nm, nd = N_DIR * M_HEADS, N_DIR * D_HEADS
    splits = (mqk, mqk, mw, mw, nm, nm, 3 * dw, dw, nd, nd, 2 * d)
    bounds = [0]
    for w_ in splits:
        bounds.append(bounds[-1] + w_)
    wt = w_in.T
    seg_bounds = [(bounds[0], bounds[4]), (bounds[6], bounds[8]), (bounds[10], bounds[11])]
    small_bounds = [(bounds[4], bounds[6]), (bounds[8], bounds[10])]
    assert small_bounds[0][1] - small_bounds[0][0] == CH_G and sum(hi - lo for lo, hi in small_bounds) == CH_T
    zpad = lambda n: jnp.zeros((n,), F32)
    bias_row = jnp.concatenate([i_bias.reshape(-1), f_bias.reshape(-1), dt_bias.reshape(-1), zpad(LANES - CH_B)])
    alog_row = jnp.concatenate([zpad(CH_G), a_log.reshape(-1), zpad(LANES - CH_B)])
    gate_params = jnp.concatenate([bias_row[None], alog_row[None], jnp.zeros((6, LANES), F32)], axis=0)

    x2 = x.reshape(m, d)
    tm = _pick_tile(m, (512, 256, 128))
    tn = next(c for c in (1024, 512, 256) if all((hi - lo) % c == 0 for lo, hi in seg_bounds))
    proj2, smalls2 = _inproj(x2, norm1_g.reshape(1, d), wt, seg_bounds, small_bounds,
                             tm, _pick_tile(m, (1024, 512, 256)), tn)
    proj3 = proj2.reshape(b, s, -1)
    gcol, grow = _gate_prep(smalls2.reshape(b, s, LANES), gate_params)

    off_d = 2 * mqk + 2 * mw
    off_g = off_d + 4 * dw
    ym, w_bm16, w_bd16, w_out16 = _mlstm(proj3, gcol, grow, m_norm_g.reshape(M_HEADS, 1, M_DV),
                                         0, mqk, 2 * mqk, 2 * mqk + mw, (w_bm, w_bd, w_out))
    yd, w_ff1_16, w_ff2_16 = _delta(proj3, conv_w, gcol, grow, d_norm_g.reshape(1, D_DIM),
                                    off_d, off_d + dw, off_d + 2 * dw, off_d + 3 * dw, (w_ff1, w_ff2))

    x1, h2 = _mix(x2, ym.reshape(m, mw), yd.reshape(m, dw), proj2, off_g, w_bm16, w_bd16, w_out16,
                  norm2_g.reshape(1, d), _pick_tile(m, (256, 128)))
    tf = _pick_tile(w_ff1.shape[1], (1024, 512, 256))
    out = _ffn(x1, h2, out_g.reshape(1, d), w_ff1_16, w_ff2_16, tm, tf)
    return out.reshape(b, s, d)


def kernel(x, norm1_g, w_in, mlstm_i_bias, mlstm_f_bias, mlstm_norm_g, delta_conv_w, delta_a_log,
           delta_dt_bias, delta_norm_g, w_branch_m, w_branch_d, w_out, norm2_g, w_ff1, w_ff2, norm_f_g):
    depth = w_in.shape[0]
    assert depth == 1, "the fused FFN + final-norm epilogue assumes a single layer"
    return _layer(x, norm1_g[0], w_in[0], mlstm_i_bias[0], mlstm_f_bias[0], mlstm_norm_g[0], delta_conv_w[0],
                  delta_a_log[0], delta_dt_bias[0], delta_norm_g[0], w_branch_m[0], w_branch_d[0], w_out[0],
                  norm2_g[0], w_ff1[0], w_ff2[0], norm_f_g)
```

```python
import functools
import math

import jax
import jax.numpy as jnp
import numpy as np
from jax import lax
from jax.experimental import pallas as pl
from jax.experimental.pallas import tpu as pltpu

F32 = jnp.float32
BF16 = jnp.bfloat16
I32 = jnp.int32

N_DIR = 2
M_HEADS = 4
M_DK = 128
M_DV = 256
D_HEADS = 8
D_DIM = 128
CONV_W = 5
RMS_EPS = 1e-6
L2_EPS = 1e-6

LANES = 128
BF16_SUBLANES = 16
M_CHUNK = 256
M_CHUNKS_PER_ITER = 4
D_CHUNK = 64
D_GROUP = 256
D_PER_GROUP = D_GROUP // D_CHUNK
D_GROUPS_PER_ITER = 4
D_SCAN_STEPS_BETWEEN_STAGES = 8
GATE_ROWS = 256
GATE_BLOCKS_PER_STEP = 4
EDGE_STEP_PIECES = 4
CONV_HALO = 8

CH_I = 0
CH_F = 8
CH_G = 16
CH_B = 32
CH_T = 48

VMEM_LIMIT = 56 * 1024 * 1024

_NT = (((1,), (1,)), ((), ()))


def _cparams(sem):
    return pltpu.CompilerParams(dimension_semantics=sem, vmem_limit_bytes=VMEM_LIMIT)


def _bdot(a, b):
    return jnp.dot(a.astype(BF16), b.astype(BF16), preferred_element_type=F32)


def _bdot_nt(a, b):
    return lax.dot_general(a.astype(BF16), b.astype(BF16), _NT, preferred_element_type=F32)


def _sigmoid(x):
    return 1.0 / (1.0 + jnp.exp(-x))


def _softplus(x):
    return jnp.maximum(x, 0.0) + jnp.log1p(jnp.exp(-jnp.abs(x)))


def _cast_blocks(w_refs, o_refs):
    for w_ref, o_ref in zip(w_refs, o_refs):
        o_ref[...] = w_ref[...].astype(o_ref.dtype)


def _cast_specs(weights, n_steps, step_of):
    specs, shapes = [], []
    for w in weights:
        rows, cols = w.shape
        assert rows % (n_steps * BF16_SUBLANES) == 0
        specs.append(pl.BlockSpec((rows // n_steps, cols), lambda *g: (step_of(*g), 0)))
        shapes.append(jax.ShapeDtypeStruct((rows, cols), BF16))
    return specs, shapes


def _lane_pick(x, ch):
    lane = lax.broadcasted_iota(I32, x.shape, 1)
    return jnp.sum(jnp.where(lane == ch, x, 0.0), axis=1, keepdims=True)


def _inproj_first_kernel(n_small, x_ref, g_ref, w_ref, *refs):
    small_refs, (o_ref, os_ref, hn_ref, wb_ref, ws_ref) = refs[:n_small], refs[n_small:]

    @pl.when(pl.program_id(0) == 0)
    def _():
        wb_ref[...] = w_ref[...].astype(BF16)
        ws_ref[...] = jnp.zeros(ws_ref.shape, BF16)
        off = 0
        for s_ref in small_refs:
            ws_ref[off:off + s_ref.shape[0], :] = s_ref[...].astype(BF16)
            off += s_ref.shape[0]

    rows = x_ref.shape[0] // EDGE_STEP_PIECES
    for r in range(EDGE_STEP_PIECES):
        sl = slice(r * rows, (r + 1) * rows)
        x = x_ref[sl, :]
        ms = jnp.mean(x * x, axis=-1, keepdims=True)
        hn = (x * lax.rsqrt(ms + RMS_EPS) * g_ref[...]).astype(BF16)
        hn_ref[sl, :] = hn
        os_ref[sl, :] = lax.dot_general(hn, ws_ref[...], _NT, preferred_element_type=F32)
        o_ref[sl, :] = lax.dot_general(hn, wb_ref[...], _NT, preferred_element_type=F32)


def _inproj_rest_kernel(h_ref, w_ref, p_ref, o_ref, wb_ref):
    del p_ref
    @pl.when(pl.program_id(1) == 0)
    def _():
        wb_ref[...] = w_ref[...].astype(BF16)

    o_ref[...] = lax.dot_general(h_ref[...], wb_ref[...], _NT, preferred_element_type=F32)


def _inproj(x2, g, wt, seg_bounds, small_bounds, tm_first, tm, tn):
    m, d = x2.shape
    assert seg_bounds[0][0] == 0
    starts = [0]
    for lo, hi in seg_bounds:
        starts.append(starts[-1] + (hi - lo) // tn)
    n_tiles = starts[-1]

    def w_row(j):
        row = seg_bounds[0][0] + j * tn
        for k in range(1, len(seg_bounds)):
            row = jnp.where(j >= starts[k], seg_bounds[k][0] + (j - starts[k]) * tn, row)
        return pl.multiple_of(row, BF16_SUBLANES), 0

    proj0, smalls, hn = pl.pallas_call(
        functools.partial(_inproj_first_kernel, len(small_bounds)),
        grid=(m // tm_first,),
        in_specs=[
            pl.BlockSpec((tm_first, d), lambda i: (i, 0)),
            pl.BlockSpec((1, d), lambda i: (0, 0)),
            pl.BlockSpec((tn, d), lambda i: (0, 0), pipeline_mode=pl.Buffered(1)),
            *[pl.BlockSpec((pl.Element(hi - lo), pl.Element(d)), functools.partial(lambda i, lo: (lo, 0), lo=lo))
              for lo, hi in small_bounds],
        ],
        out_specs=[
            pl.BlockSpec((tm_first, tn), lambda i: (i, 0)),
            pl.BlockSpec((tm_first, LANES), lambda i: (i, 0)),
            pl.BlockSpec((tm_first, d), lambda i: (i, 0)),
        ],
        out_shape=[jax.ShapeDtypeStruct((m, n_tiles * tn), F32), jax.ShapeDtypeStruct((m, LANES), F32),
                   jax.ShapeDtypeStruct((m, d), BF16)],
        scratch_shapes=[pltpu.VMEM((tn, d), BF16), pltpu.VMEM((LANES, d), BF16)],
        compiler_params=_cparams(("arbitrary",)),
        name="inproj_first",
    )(x2, g, wt, *([wt] * len(small_bounds)))
    proj = pl.pallas_call(
        _inproj_rest_kernel,
        grid=(n_tiles - 1, m // tm),
        in_specs=[
            pl.BlockSpec((tm, d), lambda j, i: (i, 0)),
            pl.BlockSpec((pl.Element(tn), pl.Element(d)), lambda j, i: w_row(j + 1)),
            pl.BlockSpec(memory_space=pl.ANY),
        ],
        out_specs=pl.BlockSpec((tm, tn), lambda j, i: (i, j + 1)),
        out_shape=jax.ShapeDtypeStruct((m, n_tiles * tn), F32),
        scratch_shapes=[pltpu.VMEM((tn, d), BF16)],
        input_output_aliases={2: 0},
        compiler_params=_cparams(("arbitrary", "arbitrary")),
        name="inproj_rest",
    )(hn, wt, proj0)
    return proj, smalls


def _cumsum_matrices():
    rr, cc = np.indices((GATE_ROWS, GATE_ROWS))
    low, upp = cc <= rr, cc >= rr
    same_d = (rr // D_CHUNK) == (cc // D_CHUNK)
    same_m = (rr // M_CHUNK) == (cc // M_CHUNK)
    mats = np.concatenate([
        low & same_m,
        upp & same_m,
        low & same_d,
        upp & same_d,
        same_d,
    ], axis=0)
    return jnp.asarray(mats.astype(np.float32), dtype=BF16)


def _gate_kernel(sm_ref, par_ref, mats_ref, col_ref, row_ref):
    r = GATE_ROWS
    lane = lax.broadcasted_iota(I32, (r, LANES), 1)
    neg_a = -jnp.exp(par_ref[1:2, :])
    xes, x3s = [], []
    for blk in range(GATE_BLOCKS_PER_STEP):
        x = sm_ref[0, blk * r:(blk + 1) * r, :] + par_ref[0:1, :]
        logf = -_softplus(-x)
        g = neg_a * _softplus(x)
        beta = _sigmoid(x)
        xe = jnp.where(lane < CH_F, x,
                       jnp.where(lane < CH_G, logf,
                                 jnp.where(lane < CH_B, g,
                                           jnp.where(lane < CH_T, beta, 0.0))))
        hi = xe.astype(BF16)
        r1 = xe - hi.astype(F32)
        mid = r1.astype(BF16)
        lo = (r1 - mid.astype(F32)).astype(BF16)
        xes.append(xe)
        x3s.append(jnp.concatenate([hi, mid, lo], axis=1))
    y3s = [jnp.dot(mats_ref[...], x3, preferred_element_type=F32) for x3 in x3s]
    half_m = CH_F + M_HEADS
    half_d = CH_G + D_HEADS
    for blk, (xe, y3) in enumerate(zip(xes, y3s)):
        y = y3[:, 0:LANES] + y3[:, LANES:2 * LANES] + y3[:, 2 * LANES:3 * LANES]
        pm, sm_, pd, sd, td = (y[i * r:(i + 1) * r] for i in range(5))
        out = jnp.where(lane < CH_F, xe,
              jnp.where(lane < half_m, pm,
              jnp.where(lane < CH_G, sm_,
              jnp.where(lane < half_d, pd,
              jnp.where(lane < CH_B, sd,
              jnp.where(lane < CH_T, xe,
              jnp.where(lane < CH_T + N_DIR * D_HEADS, pltpu.roll(td, CH_T - CH_G, axis=1), 0.0)))))))
        col_ref[0, blk * r:(blk + 1) * r, :] = out
        row_ref[0, :, blk * r:(blk + 1) * r] = out.T


def _gate_prep(smalls3, params):
    b, s, _ = smalls3.shape
    r = GATE_ROWS * GATE_BLOCKS_PER_STEP
    assert s % r == 0
    return pl.pallas_call(
        _gate_kernel,
        grid=(b, s // r),
        in_specs=[
            pl.BlockSpec((1, r, LANES), lambda i, j: (i, j, 0)),
            pl.BlockSpec((8, LANES), lambda i, j: (0, 0)),
            pl.BlockSpec((5 * GATE_ROWS, GATE_ROWS), lambda i, j: (0, 0)),
        ],
        out_specs=[
            pl.BlockSpec((1, r, LANES), lambda i, j: (i, j, 0)),
            pl.BlockSpec((1, LANES, r), lambda i, j: (i, 0, j)),
        ],
        out_shape=[jax.ShapeDtypeStruct((b, s, LANES), F32), jax.ShapeDtypeStruct((b, LANES, s), F32)],
        compiler_params=_cparams(("parallel", "parallel")),
        name="gateprep",
    )(smalls3, params, _cumsum_matrices())


def _mlstm_kernel(n_cast, q_ref, k_ref, v_ref, o_ref, gc_ref, gr_ref, ng_ref, *refs):
    w_refs, y_ref, wo_refs = refs[:n_cast], refs[n_cast], refs[n_cast + 1:2 * n_cast + 1]
    hn_ref, den_ref, mi_ref, cu_ref, nu_ref, bl_ref, ml_ref, cs_ref, ns_ref, ms_ref, c_ref = refs[2 * n_cast + 1:]
    _cast_blocks(w_refs, wo_refs)
    head = pl.program_id(1)
    s = q_ref.shape[1]
    L = M_CHUNK
    nc = s // L
    scale = M_DK ** -0.5
    rr = lax.broadcasted_iota(I32, (L, L), 0)
    cc = lax.broadcasted_iota(I32, (L, L), 1)

    def bcast8(x11):
        return jnp.broadcast_to(x11, (8, LANES))

    def local(cs):
        chains = []
        for c in cs:
            r0 = pl.multiple_of(c * L, L)
            q = q_ref[0, pl.ds(r0, L), :]
            k = k_ref[0, pl.ds(r0, L), :] * scale
            vb = v_ref[0, pl.ds(r0, L), :].astype(BF16)
            kb = k.astype(BF16)
            qk = lax.dot_general(q.astype(BF16), kb, _NT, preferred_element_type=F32)
            k_t = k.T
            gcol = gc_ref[0, pl.ds(r0, L), :]
            for d in range(N_DIR):
                ch_i = CH_I + d * M_HEADS + head
                ch_f = CH_F + d * M_HEADS + head
                bc_col = _lane_pick(gcol, ch_f)
                i_row = gr_ref[0, pl.ds(ch_i, 1), pl.ds(r0, L)]
                bc_row = gr_ref[0, pl.ds(ch_f, 1), pl.ds(r0, L)]
                a_row = i_row - bc_row
                mask = (cc <= rr) if d == 0 else (cc >= rr)
                log_intra = jnp.where(mask, bc_col + a_row, -jnp.inf)
                m_intra = jnp.max(log_intra, axis=1, keepdims=True)
                p = qk * jnp.exp(log_intra - m_intra)
                b_last = bc_row[:, L - 1:L] if d == 0 else bc_row[:, 0:1]
                log_state = b_last + a_row
                m_loc = jnp.max(log_state, axis=1, keepdims=True)
                w_state = jnp.exp(log_state - m_loc)
                chains.append(dict(
                    idx=d * nc + c, d=d, r0=r0, vb=vb, kb=kb, pb=p.astype(BF16),
                    den=jnp.sum(p, axis=1, keepdims=True), m_intra=m_intra,
                    kw=(k_t * w_state).astype(BF16), w8=jnp.broadcast_to(w_state, (8, L)).astype(BF16),
                    b_last=b_last, m_loc=m_loc))
        for c in chains:
            hn_ref[c["d"], pl.ds(c["r0"], L), :] = jnp.dot(c["pb"], c["vb"], preferred_element_type=F32)
            den_ref[c["d"], pl.ds(c["r0"], L), :] = jnp.broadcast_to(c["den"], (L, LANES))
            mi_ref[c["d"], pl.ds(c["r0"], L), :] = jnp.broadcast_to(c["m_intra"], (L, LANES))
        for c in chains:
            cu_ref[c["idx"]] = jnp.dot(c["kw"], c["vb"], preferred_element_type=F32)
            nu_ref[c["idx"]] = jnp.dot(c["w8"], c["kb"], preferred_element_type=F32)
            bl_ref[c["idx"]] = bcast8(c["b_last"])
            ml_ref[c["idx"]] = bcast8(c["m_loc"])

    def local_trip(i, _):
        local([M_CHUNKS_PER_ITER * i + t for t in range(M_CHUNKS_PER_ITER)])
        return 0

    lax.fori_loop(0, nc // M_CHUNKS_PER_ITER, local_trip, 0)

    c_ref[...] = jnp.zeros(c_ref.shape, F32)

    def scan(j, carry):
        out = []
        for d in range(N_DIR):
            m_st, n_st = carry[d]
            idx = d * nc + (j if d == 0 else nc - 1 - j)
            c_st = c_ref[d]
            cs_ref[idx] = c_st.astype(BF16)
            ns_ref[idx] = n_st
            ms_ref[idx] = m_st
            bl = bl_ref[idx]
            ml = ml_ref[idx]
            m_new = jnp.maximum(bl + m_st, ml)
            decay = jnp.exp(bl + m_st - m_new)
            gain = jnp.exp(ml - m_new)
            c_ref[d] = decay[0:1, 0:1] * c_st + gain[0:1, 0:1] * cu_ref[idx]
            out.append((m_new, decay * n_st + gain * nu_ref[idx]))
        return tuple(out)

    m0 = jnp.full((8, LANES), -jnp.inf, F32)
    n0 = jnp.zeros((8, M_DK), F32)
    lax.fori_loop(0, nc, scan, ((m0, n0), (m0, n0)))

    def wide(x):
        return jnp.concatenate([x] * (M_DV // LANES), axis=1)

    def combine(c, _):
        r0 = pl.multiple_of(c * L, L)
        q = q_ref[0, pl.ds(r0, L), :]
        qb = q.astype(BF16)
        gcol = gc_ref[0, pl.ds(r0, L), :]
        hh = None
        for d in range(N_DIR):
            idx = d * nc + c
            bc = jnp.broadcast_to(_lane_pick(gcol, CH_F + d * M_HEADS + head), (L, LANES))
            den_i = den_ref[d, pl.ds(r0, L), :]
            m_i = mi_ref[d, pl.ds(r0, L), :]
            log_inter = bc + ms_ref[idx][0:1, :]
            m_row = jnp.maximum(log_inter, m_i)
            w_i = jnp.exp(m_i - m_row)
            w_x = jnp.exp(log_inter - m_row)
            q_c = jnp.dot(qb, cs_ref[idx], preferred_element_type=F32)
            q_n = jnp.broadcast_to(jnp.sum(q * ns_ref[idx][0:1, :], axis=1, keepdims=True), (L, LANES))
            num = wide(w_i) * hn_ref[d, pl.ds(r0, L), :] + wide(w_x) * q_c
            den = w_i * den_i + w_x * q_n
            h_d = num * wide(1.0 / jnp.maximum(jnp.abs(den), jnp.exp(-m_row)))
            hh = h_d if hh is None else hh + h_d
        hh = hh * lax.rsqrt(jnp.mean(hh * hh, axis=-1, keepdims=True) + RMS_EPS)
        hh = hh * ng_ref[0]
        y_ref[0, pl.ds(r0, L), :] = (_sigmoid(o_ref[0, pl.ds(r0, L), :]) * hh).astype(y_ref.dtype)
        return 0

    lax.fori_loop(0, nc, combine, 0, unroll=2)


def _mlstm(proj3, gcol, grow, norm_g, off_q, off_k, off_v, off_o, cast_weights):
    b, s, _ = proj3.shape
    w_specs, w_shapes = _cast_specs(cast_weights, b * M_HEADS, lambda i, h: i * M_HEADS + h)
    assert s % (M_CHUNK * M_CHUNKS_PER_ITER) == 0
    nc = s // M_CHUNK
    bq, bk = off_q // M_DK, off_k // M_DK
    bv, bo = off_v // M_DV, off_o // M_DV
    return pl.pallas_call(
        functools.partial(_mlstm_kernel, len(cast_weights)),
        grid=(b, M_HEADS),
        in_specs=[
            pl.BlockSpec((1, s, M_DK), lambda i, h: (i, 0, bq + h)),
            pl.BlockSpec((1, s, M_DK), lambda i, h: (i, 0, bk + h)),
            pl.BlockSpec((1, s, M_DV), lambda i, h: (i, 0, bv + h)),
            pl.BlockSpec((1, s, M_DV), lambda i, h: (i, 0, bo + h)),
            pl.BlockSpec((1, s, LANES), lambda i, h: (i, 0, 0)),
            pl.BlockSpec((1, LANES, s), lambda i, h: (i, 0, 0)),
            pl.BlockSpec((1, 1, M_DV), lambda i, h: (h, 0, 0)),
            *w_specs,
        ],
        out_specs=[pl.BlockSpec((1, s, M_DV), lambda i, h: (i, 0, h)), *w_specs],
        out_shape=[jax.ShapeDtypeStruct((b, s, M_HEADS * M_DV), BF16), *w_shapes],
        scratch_shapes=[
            pltpu.VMEM((N_DIR, s, M_DV), F32),
            pltpu.VMEM((N_DIR, s, LANES), F32),
            pltpu.VMEM((N_DIR, s, LANES), F32),
            pltpu.VMEM((N_DIR * nc, M_DK, M_DV), F32),
            pltpu.VMEM((N_DIR * nc, 8, M_DK), F32),
            pltpu.VMEM((N_DIR * nc, 8, LANES), F32),
            pltpu.VMEM((N_DIR * nc, 8, LANES), F32),
            pltpu.VMEM((N_DIR * nc, M_DK, M_DV), BF16),
            pltpu.VMEM((N_DIR * nc, 8, M_DK), F32),
            pltpu.VMEM((N_DIR * nc, 8, LANES), F32),
            pltpu.VMEM((N_DIR, M_DK, M_DV), F32),
        ],
        compiler_params=_cparams(("parallel", "arbitrary")),
        name="mlstm",
    )(proj3, proj3, proj3, proj3, gcol, grow, norm_g, *cast_weights)


def _delta_kernel(n_cast, q_ref, k_ref, v_ref, z_ref, cwq_ref, cwk_ref, cwv_ref, gc_ref, gr_ref, ng_ref, *refs):
    w_refs, y_ref, wo_refs = refs[:n_cast], refs[n_cast], refs[n_cast + 1:2 * n_cast + 1]
    xp_ref, qs_ref, ks_ref, vs_ref, kq_ref, n_ref, dec_ref, o_ref, st_ref = refs[2 * n_cast + 1:]
    _cast_blocks(w_refs, wo_refs)
    t = pl.program_id(0)
    head = jnp.minimum(t, pl.num_programs(0) - 2) % D_HEADS
    s = q_ref.shape[1]
    L = D_CHUNK
    G = D_GROUP
    nc = s // L
    ng = s // G
    halo = CONV_HALO
    pad = CONV_W // 2

    @pl.when(t == 0)
    def _():
        for r in (kq_ref, n_ref, dec_ref, o_ref):
            r[...] = jnp.zeros(r.shape, r.dtype)

    xp_ref[0:halo, :] = jnp.zeros((halo, D_DIM), F32)
    xp_ref[s + halo:s + 2 * halo, :] = jnp.zeros((halo, D_DIM), F32)

    def conv_into(x_ref, cw_ref, dst_ref, l2, mult):
        xp_ref[halo:s + halo, :] = x_ref[0]
        for blk in range(ng):
            base = halo + blk * G - pad
            acc = xp_ref[base:base + G, :] * cw_ref[0:1, :]
            for j in range(1, CONV_W):
                acc = acc + xp_ref[base + j:base + j + G, :] * cw_ref[j:j + 1, :]
            y = acc * _sigmoid(acc)
            if l2:
                y = y * (lax.rsqrt(jnp.sum(y * y, axis=-1, keepdims=True) + L2_EPS) * mult)
            dst_ref[blk * G:(blk + 1) * G, :] = y

    conv_into(q_ref, cwq_ref, qs_ref, True, D_DIM ** -0.5)
    conv_into(k_ref, cwk_ref, ks_ref, True, 1.0)
    conv_into(v_ref, cwv_ref, vs_ref, False, 1.0)

    st_ref[...] = jnp.zeros(st_ref.shape, F32)

    def scan_step(scan, j):
        for d in range(N_DIR):
            c = j if d == 0 else nc - 1 - j
            r0 = pl.multiple_of(c * L, L)
            idx = (scan + d) * nc + c
            st = st_ref[d]
            res = jnp.dot(kq_ref[idx], st.astype(BF16), preferred_element_type=F32)
            st_ref[d] = dec_ref[idx][0:1, :] * st + res[0:D_DIM] + n_ref[idx]
            o_ref[scan + d, pl.ds(r0, L), :] = o_ref[scan + d, pl.ds(r0, L), :] + res[D_DIM:D_DIM + L]

    tchunk = lax.broadcasted_iota(I32, (D_DIM, D_GROUP), 1) // D_CHUNK
    rr = lax.broadcasted_iota(I32, (G, G), 0)
    cc = lax.broadcasted_iota(I32, (G, G), 1)
    same = (rr // L) == (cc // L)
    pr = lax.broadcasted_iota(I32, (L, G), 0)
    pc = lax.broadcasted_iota(I32, (L, G), 1)
    eye_p = jnp.where((pc % L) == pr, 1.0, 0.0).astype(F32)

    def pack(m):
        out = m[0:L]
        for i in range(1, D_PER_GROUP):
            out = out + m[i * L:(i + 1) * L]
        return out

    def unpack(p):
        return jnp.where(same, jnp.concatenate([p] * D_PER_GROUP, axis=0), 0.0)

    scan_per_trip = nc // (ng // D_GROUPS_PER_ITER)

    def groups(fill, scan, trip):
        pending = iter(range(scan_per_trip))

        def scan_some(n):
            for _ in range(n):
                j = next(pending, None)
                if j is not None:
                    scan_step(scan, trip * scan_per_trip + j)

        chains = []
        for gi in [D_GROUPS_PER_ITER * trip + g for g in range(D_GROUPS_PER_ITER)]:
            r0 = pl.multiple_of(gi * G, G)
            kk_ = ks_ref[pl.ds(r0, G), :]
            qq_ = qs_ref[pl.ds(r0, G), :]
            vv_ = vs_ref[pl.ds(r0, G), :]
            kb = kk_.astype(BF16)
            kk = lax.dot_general(kb, kb, _NT, preferred_element_type=F32)
            qk = lax.dot_general(qq_.astype(BF16), kb, _NT, preferred_element_type=F32)
            gcol = gc_ref[0, pl.ds(r0, G), :]
            for d in range(N_DIR):
                ch = d * D_HEADS + head
                g_col = _lane_pick(gcol, CH_G + ch)
                b_col = _lane_pick(gcol, CH_B + ch)
                t_col = _lane_pick(gcol, CH_T + ch)
                g_row = gr_ref[0, pl.ds(CH_G + ch, 1), pl.ds(r0, G)]
                tri = (cc <= rr) if d == 0 else (cc >= rr)
                strict = (cc < rr) if d == 0 else (cc > rr)
                gam = jnp.exp(jnp.where(tri, jnp.where(same, g_col - g_row, -jnp.inf), -jnp.inf))
                x = jnp.where(strict, -(b_col * kk * gam), 0.0)
                eg = jnp.exp(g_col)
                chains.append(dict(
                    gi=gi, slot=fill + d, r0=r0, x=x, xp=pack(x), attn=(qk * gam).astype(BF16),
                    rhs=jnp.concatenate([b_col * vv_, (b_col * eg) * kk_], axis=1).astype(BF16),
                    qg=qq_ * eg, kdt=(kk_ * jnp.exp(t_col - g_col)).T, dec=jnp.exp(t_col)))
            scan_some(-(-(scan_per_trip - D_SCAN_STEPS_BETWEEN_STAGES) // D_GROUPS_PER_ITER))
        for c in chains:
            c["tp"] = eye_p + c["xp"]
            c["p"] = _bdot(c["xp"], c["x"])
        scan_some(1)
        for _it in range(4):
            for c in chains:
                res = _bdot(jnp.concatenate([c["tp"], c["p"]], axis=0), unpack(c["p"]))
                c["tp"] = c["tp"] + res[0:L]
                c["p"] = res[L:2 * L]
            scan_some(1)
        for c in chains:
            c["tp"] = c["tp"] + _bdot(c["tp"], unpack(c["p"]))
        scan_some(1)
        for c in chains:
            c["uw"] = _bdot(unpack(c["tp"]), c["rhs"]).astype(BF16)
        scan_some(1)
        for c in chains:
            au = jnp.dot(c["attn"], c["uw"], preferred_element_type=F32)
            o_ref[c["slot"], pl.ds(c["r0"], G), :] = au[:, 0:D_DIM]
            c["qe"] = (c["qg"] - au[:, D_DIM:2 * D_DIM]).astype(BF16)
        scan_some(1)
        for c in chains:
            for ci in range(D_PER_GROUP):
                idx = c["slot"] * nc + c["gi"] * D_PER_GROUP + ci
                ku = _bdot(jnp.where(tchunk == ci, c["kdt"], 0.0), c["uw"])
                n_ref[idx] = ku[:, 0:D_DIM]
                kq_ref[idx, 0:D_DIM, :] = (-ku[:, D_DIM:2 * D_DIM]).astype(BF16)
                kq_ref[idx, D_DIM:D_DIM + L, :] = c["qe"][ci * L:(ci + 1) * L]
                dec_ref[idx] = jnp.broadcast_to(c["dec"][ci * L:ci * L + 8], (8, D_DIM))
        scan_some(scan_per_trip)

    def finish(scan, gi):
        r0 = pl.multiple_of(gi * G, G)
        o = o_ref[scan, pl.ds(r0, G), :] + o_ref[scan + 1, pl.ds(r0, G), :]
        o = o * lax.rsqrt(jnp.mean(o * o, axis=-1, keepdims=True) + RMS_EPS) * ng_ref[...]
        z = z_ref[0, pl.ds(r0, G), :]
        y_ref[0, pl.ds(r0, G), :] = (o * (z * _sigmoid(z))).astype(y_ref.dtype)

    fill = (t % 2) * N_DIR
    scan = N_DIR - fill

    def group_batch(i, carry):
        groups(fill, scan, i)
        return carry

    def finish_batch(gi, carry):
        finish(scan, gi)
        return carry

    lax.fori_loop(0, ng // D_GROUPS_PER_ITER, group_batch, 0)
    lax.fori_loop(0, ng, finish_batch, 0, unroll=4)


def _delta(proj3, conv_w, gcol, grow, norm_g, off_q, off_k, off_v, off_z, cast_weights):
    b, s, _ = proj3.shape
    w = D_HEADS * D_DIM
    nc = s // D_CHUNK
    trips = s // (D_GROUP * D_GROUPS_PER_ITER)
    assert s % (D_GROUP * D_GROUPS_PER_ITER) == 0 and nc % trips == 0
    bq, bk, bv, bz = (o // D_DIM for o in (off_q, off_k, off_v, off_z))
    n_units = b * D_HEADS
    cur = lambda t: jnp.minimum(t, n_units - 1)
    prev = lambda t: jnp.maximum(t - 1, 0)
    col = lambda unit, blk: lambda t: (unit(t) // D_HEADS, 0, blk + unit(t) % D_HEADS)
    w_specs, w_shapes = _cast_specs(cast_weights, n_units, cur)
    return pl.pallas_call(
        functools.partial(_delta_kernel, len(cast_weights)),
        grid=(n_units + 1,),
        in_specs=[
            pl.BlockSpec((1, s, D_DIM), col(cur, bq)),
            pl.BlockSpec((1, s, D_DIM), col(cur, bk)),
            pl.BlockSpec((1, s, D_DIM), col(cur, bv)),
            pl.BlockSpec((1, s, D_DIM), col(prev, bz)),
            pl.BlockSpec((CONV_W, D_DIM), lambda t: (0, cur(t) % D_HEADS)),
            pl.BlockSpec((CONV_W, D_DIM), lambda t: (0, D_HEADS + cur(t) % D_HEADS)),
            pl.BlockSpec((CONV_W, D_DIM), lambda t: (0, 2 * D_HEADS + cur(t) % D_HEADS)),
            pl.BlockSpec((1, s, LANES), lambda t: (cur(t) // D_HEADS, 0, 0)),
            pl.BlockSpec((1, LANES, s), lambda t: (cur(t) // D_HEADS, 0, 0)),
            pl.BlockSpec((1, D_DIM), lambda t: (0, 0)),
            *w_specs,
        ],
        out_specs=[pl.BlockSpec((1, s, D_DIM), col(prev, 0)), *w_specs],
        out_shape=[jax.ShapeDtypeStruct((b, s, w), BF16), *w_shapes],
        scratch_shapes=[
            pltpu.VMEM((s + 2 * CONV_HALO, D_DIM), F32),
            pltpu.VMEM((s, D_DIM), F32),
            pltpu.VMEM((s, D_DIM), F32),
            pltpu.VMEM((s, D_DIM), F32),
            pltpu.VMEM((2 * N_DIR * nc, D_DIM + D_CHUNK, D_DIM), BF16),
            pltpu.VMEM((2 * N_DIR * nc, D_DIM, D_DIM), F32),
            pltpu.VMEM((2 * N_DIR * nc, 8, D_DIM), F32),
            pltpu.VMEM((2 * N_DIR, s, D_DIM), F32),
            pltpu.VMEM((N_DIR, D_DIM, D_DIM), F32),
        ],
        compiler_params=_cparams(("arbitrary",)),
        name="delta",
    )(proj3, proj3, proj3, proj3, conv_w, conv_w, conv_w, gcol, grow, norm_g, *cast_weights)


def _mix_kernel(n_gate_blocks, x_ref, ym_ref, yd_ref, *refs):
    gate_refs, (wbm_ref, wbd_ref, wo_ref, g2_ref, o_ref, h_ref) = refs[:2 * n_gate_blocks], refs[2 * n_gate_blocks:]
    gate = lambda rs: _sigmoid(jnp.concatenate([r[...] for r in rs], axis=1))
    a = jnp.dot(ym_ref[...], wbm_ref[...], preferred_element_type=F32)
    b = jnp.dot(yd_ref[...], wbd_ref[...], preferred_element_type=F32)
    mixed = gate(gate_refs[:n_gate_blocks]) * a + gate(gate_refs[n_gate_blocks:]) * b
    x = x_ref[...] + jnp.dot(mixed.astype(BF16), wo_ref[...], preferred_element_type=F32)
    o_ref[...] = x
    ms = jnp.mean(x * x, axis=-1, keepdims=True)
    h_ref[...] = (x * lax.rsqrt(ms + RMS_EPS) * g2_ref[...]).astype(BF16)


def _mix(x2, ym2, yd2, proj2, off_gates, wbm, wbd, wo, g2, tm):
    m, d = x2.shape
    const = lambda i: (0, 0)
    gw = math.gcd(off_gates, d)
    gate_specs = [pl.BlockSpec((tm, gw), functools.partial(lambda i, c: (i, c), c=(off_gates + t * gw) // gw))
                  for t in range(2 * d // gw)]
    return pl.pallas_call(
        functools.partial(_mix_kernel, d // gw),
        grid=(m // tm,),
        in_specs=[
            pl.BlockSpec((tm, d), lambda i: (i, 0)),
            pl.BlockSpec((tm, ym2.shape[1]), lambda i: (i, 0)),
            pl.BlockSpec((tm, yd2.shape[1]), lambda i: (i, 0)),
            *gate_specs,
            pl.BlockSpec(wbm.shape, const),
            pl.BlockSpec(wbd.shape, const),
            pl.BlockSpec(wo.shape, const),
            pl.BlockSpec((1, d), const),
        ],
        out_specs=[pl.BlockSpec((tm, d), lambda i: (i, 0)), pl.BlockSpec((tm, d), lambda i: (i, 0))],
        out_shape=[jax.ShapeDtypeStruct((m, d), F32), jax.ShapeDtypeStruct((m, d), BF16)],
        compiler_params=_cparams(("parallel",)),
        name="mix",
    )(x2, ym2, yd2, *([proj2] * len(gate_specs)), wbm, wbd, wo, g2)


def _ffn_kernel(n_slices, x_ref, h_ref, gf_ref, w1_ref, w2_ref, o_ref, acc_ref):
    f = pl.program_id(1)
    last = n_slices - 1

    def ffn_part(sl):
        hid = jnp.dot(h_ref[sl, :], w1_ref[...], preferred_element_type=F32)
        act = jnp.square(jnp.maximum(hid, 0.0)).astype(BF16)
        return jnp.dot(act, w2_ref[...], preferred_element_type=F32)

    @pl.when(jnp.logical_and(f == 0, f < last))
    def _():
        acc_ref[...] = ffn_part(slice(None))

    @pl.when(jnp.logical_and(f > 0, f < last))
    def _():
        acc_ref[...] += ffn_part(slice(None))

    @pl.when(f == last)
    def _():
        rows = x_ref.shape[0] // EDGE_STEP_PIECES
        for r in range(EDGE_STEP_PIECES):
            sl = slice(r * rows, (r + 1) * rows)
            y = ffn_part(sl) if last == 0 else acc_ref[sl, :] + ffn_part(sl)
            x = x_ref[sl, :] + y
            ms = jnp.mean(x * x, axis=-1, keepdims=True)
            o_ref[sl, :] = x * lax.rsqrt(ms + RMS_EPS) * gf_ref[...]


def _ffn(x2, h2, gf, w1, w2, tm, tf):
    m, d = x2.shape
    dff = w1.shape[1]
    return pl.pallas_call(
        functools.partial(_ffn_kernel, dff // tf),
        grid=(m // tm, dff // tf),
        in_specs=[
            pl.BlockSpec((tm, d), lambda i, f: (i, 0)),
            pl.BlockSpec((tm, d), lambda i, f: (i, 0)),
            pl.BlockSpec((1, d), lambda i, f: (0, 0)),
            pl.BlockSpec((d, tf), lambda i, f: (0, f)),
            pl.BlockSpec((tf, d), lambda i, f: (f, 0)),
        ],
        out_specs=pl.BlockSpec((tm, d), lambda i, f: (i, 0)),
        out_shape=jax.ShapeDtypeStruct((m, d), F32),
        scratch_shapes=[pltpu.VMEM((tm, d), F32)],
        compiler_params=_cparams(("parallel", "arbitrary")),
        name="ffn",
    )(x2, h2, gf, w1, w2)


def _pick_tile(n, candidates):
    for c in candidates:
        if n % c == 0:
            return c
    raise ValueError(f"no tile in {candidates} divides {n}")


def _layer(x, norm1_g, w_in, i_bias, f_bias, m_norm_g, conv_w, a_log, dt_bias, d_norm_g,
           w_bm, w_bd, w_out, norm2_g, w_ff1, w_ff2, out_g):
    b, s, d = x.shape
    m = b * s
    mw, mqk, dw = M_HEADS * M_DV, M_HEADS * M_DK, D_HEADS * D_DIM
    nm, nd = N_DIR * M_HEADS, N_DIR * D_HEADS
    splits = (mqk, mqk, mw, mw, nm, nm, 3 * dw, dw, nd, nd, 2 * d)
    bounds = [0]
    for w_ in splits:
        bounds.append(bounds[-1] + w_)
    wt = w_in.T
    seg_bounds = [(bounds[0], bounds[4]), (bounds[6], bounds[8]), (bounds[10], bounds[11])]
    small_bounds = [(bounds[4], bounds[6]), (bounds[8], bounds[10])]
    assert small_bounds[0][1] - small_bounds[0][0] == CH_G and sum(hi - lo for lo, hi in small_bounds) == CH_T
    zpad = lambda n: jnp.zeros((n,), F32)
    bias_row = jnp.concatenate([i_bias.reshape(-1), f_bias.reshape(-1), dt_bias.reshape(-1), zpad(LANES - CH_B)])
    alog_row = jnp.concatenate([zpad(CH_G), a_log.reshape(-1), zpad(LANES - CH_B)])
    gate_params = jnp.concatenate([bias_row[None], alog_row[None], jnp.zeros((6, LANES), F32)], axis=0)

    x2 = x.reshape(m, d)
    tm = _pick_tile(m, (512, 256, 128))
    tn = next(c for c in (1024, 512, 256) if all((hi - lo) % c == 0 for lo, hi in seg_bounds))
    proj2, smalls2 = _inproj(x2, norm1_g.reshape(1, d), wt, seg_bounds, small_bounds,
                             _pick_tile(m, (1024, 512, 256)), _pick_tile(m, (1024, 512, 256)), tn)
    proj3 = proj2.reshape(b, s, -1)
    gcol, grow = _gate_prep(smalls2.reshape(b, s, LANES), gate_params)

    off_d = 2 * mqk + 2 * mw
    off_g = off_d + 4 * dw
    ym, w_bm16, w_bd16, w_out16 = _mlstm(proj3, gcol, grow, m_norm_g.reshape(M_HEADS, 1, M_DV),
                                         0, mqk, 2 * mqk, 2 * mqk + mw, (w_bm, w_bd, w_out))
    yd, w_ff1_16, w_ff2_16 = _delta(proj3, conv_w, gcol, grow, d_norm_g.reshape(1, D_DIM),
                                    off_d, off_d + dw, off_d + 2 * dw, off_d + 3 * dw, (w_ff1, w_ff2))

    x1, h2 = _mix(x2, ym.reshape(m, mw), yd.reshape(m, dw), proj2, off_g, w_bm16, w_bd16, w_out16,
                  norm2_g.reshape(1, d), _pick_tile(m, (256, 128)))
    tf = _pick_tile(w_ff1.shape[1], (1024, 512, 256))
    out = _ffn(x1, h2, out_g.reshape(1, d), w_ff1_16, w_ff2_16, tm, tf)
    return out.reshape(b, s, d)


def kernel(x, norm1_g, w_in, mlstm_i_bias, mlstm_f_bias, mlstm_norm_g, delta_conv_w, delta_a_log,
           delta_dt_bias, delta_norm_g, w_branch_m, w_branch_d, w_out, norm2_g, w_ff1, w_ff2, norm_f_g):
    depth = w_in.shape[0]
    assert depth == 1, "the fused FFN + final-norm epilogue assumes a single layer"
    return _layer(x, norm1_g[0], w_in[0], mlstm_i_bias[0], mlstm_f_bias[0], mlstm_norm_g[0], delta_conv_w[0],
                  delta_a_log[0], delta_dt_bias[0], delta_norm_g[0], w_branch_m[0], w_branch_d[0], w_out[0],
                  norm2_g[0], w_ff1[0], w_ff2[0], norm_f_g)
```

```python
import functools
import math

import jax
import jax.numpy as jnp
import numpy as np
from jax import lax
from jax.experimental import pallas as pl
from jax.experimental.pallas import tpu as pltpu

F32 = jnp.float32
BF16 = jnp.bfloat16
I32 = jnp.int32

N_DIR = 2
M_HEADS = 4
M_DK = 128
M_DV = 256
D_HEADS = 8
D_DIM = 128
CONV_W = 5
RMS_EPS = 1e-6
L2_EPS = 1e-6

LANES = 128
BF16_SUBLANES = 16
M_CHUNK = 256
M_CHUNKS_PER_ITER = 4
D_CHUNK = 64
D_GROUP = 256
D_PER_GROUP = D_GROUP // D_CHUNK
D_GROUPS_PER_ITER = 4
D_SCAN_STEPS_BETWEEN_STAGES = 8
GATE_ROWS = 256
GATE_BLOCKS_PER_STEP = 4
EDGE_STEP_PIECES = 4
CONV_HALO = 8

CH_I = 0
CH_F = 8
CH_G = 16
CH_B = 32
CH_T = 48

VMEM_LIMIT = 56 * 1024 * 1024

_NT = (((1,), (1,)), ((), ()))


def _cparams(sem):
    return pltpu.CompilerParams(dimension_semantics=sem, vmem_limit_bytes=VMEM_LIMIT)


def _bdot(a, b):
    return jnp.dot(a.astype(BF16), b.astype(BF16), preferred_element_type=F32)


def _bdot_nt(a, b):
    return lax.dot_general(a.astype(BF16), b.astype(BF16), _NT, preferred_element_type=F32)


def _sigmoid(x):
    return 1.0 / (1.0 + jnp.exp(-x))


def _softplus(x):
    return jnp.maximum(x, 0.0) + jnp.log1p(jnp.exp(-jnp.abs(x)))


def _cast_blocks(w_refs, o_refs):
    for w_ref, o_ref in zip(w_refs, o_refs):
        o_ref[...] = w_ref[...].astype(o_ref.dtype)


def _cast_specs(weights, n_steps, step_of):
    specs, shapes = [], []
    for w in weights:
        rows, cols = w.shape
        assert rows % (n_steps * BF16_SUBLANES) == 0
        specs.append(pl.BlockSpec((rows // n_steps, cols), lambda *g: (step_of(*g), 0)))
        shapes.append(jax.ShapeDtypeStruct((rows, cols), BF16))
    return specs, shapes


def _lane_pick(x, ch):
    lane = lax.broadcasted_iota(I32, x.shape, 1)
    return jnp.sum(jnp.where(lane == ch, x, 0.0), axis=1, keepdims=True)


def _inproj_first_kernel(n_small, x_ref, g_ref, w_ref, *refs):
    small_refs, (o_ref, os_ref, hn_ref, wb_ref, ws_ref) = refs[:n_small], refs[n_small:]

    @pl.when(pl.program_id(0) == 0)
    def _():
        wb_ref[...] = w_ref[...].astype(BF16)
        ws_ref[...] = jnp.zeros(ws_ref.shape, BF16)
        off = 0
        for s_ref in small_refs:
            ws_ref[off:off + s_ref.shape[0], :] = s_ref[...].astype(BF16)
            off += s_ref.shape[0]

    rows = x_ref.shape[0] // EDGE_STEP_PIECES
    for r in range(EDGE_STEP_PIECES):
        sl = slice(r * rows, (r + 1) * rows)
        x = x_ref[sl, :]
        ms = jnp.mean(x * x, axis=-1, keepdims=True)
        hn = (x * lax.rsqrt(ms + RMS_EPS) * g_ref[...]).astype(BF16)
        hn_ref[sl, :] = hn
        os_ref[sl, :] = lax.dot_general(hn, ws_ref[...], _NT, preferred_element_type=F32)
        o_ref[sl, :] = lax.dot_general(hn, wb_ref[...], _NT, preferred_element_type=F32)


def _inproj_rest_kernel(h_ref, w_ref, o_ref, wb_ref):
    @pl.when(pl.program_id(1) == 0)
    def _():
        wb_ref[...] = w_ref[...].astype(BF16)

    o_ref[...] = lax.dot_general(h_ref[...], wb_ref[...], _NT, preferred_element_type=F32)


def _inproj(x2, g, wt, seg_bounds, small_bounds, tm_first, tm, tn):
    m, d = x2.shape
    assert seg_bounds[0][0] == 0
    starts = [0]
    for lo, hi in seg_bounds:
        starts.append(starts[-1] + (hi - lo) // tn)
    n_tiles = starts[-1]

    def w_row(j):
        row = seg_bounds[0][0] + j * tn
        for k in range(1, len(seg_bounds)):
            row = jnp.where(j >= starts[k], seg_bounds[k][0] + (j - starts[k]) * tn, row)
        return pl.multiple_of(row, BF16_SUBLANES), 0

    proj0, smalls, hn = pl.pallas_call(
        functools.partial(_inproj_first_kernel, len(small_bounds)),
        grid=(m // tm_first,),
        in_specs=[
            pl.BlockSpec((tm_first, d), lambda i: (i, 0)),
            pl.BlockSpec((1, d), lambda i: (0, 0)),
            pl.BlockSpec((tn, d), lambda i: (0, 0), pipeline_mode=pl.Buffered(1)),
            *[pl.BlockSpec((pl.Element(hi - lo), pl.Element(d)), functools.partial(lambda i, lo: (lo, 0), lo=lo))
              for lo, hi in small_bounds],
        ],
        out_specs=[
            pl.BlockSpec((tm_first, tn), lambda i: (i, 0)),
            pl.BlockSpec((tm_first, LANES), lambda i: (i, 0)),
            pl.BlockSpec((tm_first, d), lambda i: (i, 0)),
        ],
        out_shape=[jax.ShapeDtypeStruct((m, tn), F32), jax.ShapeDtypeStruct((m, LANES), F32),
                   jax.ShapeDtypeStruct((m, d), BF16)],
        scratch_shapes=[pltpu.VMEM((tn, d), BF16), pltpu.VMEM((LANES, d), BF16)],
        compiler_params=_cparams(("arbitrary",)),
        name="inproj_first",
    )(x2, g, wt, *([wt] * len(small_bounds)))
    proj_rest = pl.pallas_call(
        _inproj_rest_kernel,
        grid=(n_tiles - 1, m // tm),
        in_specs=[
            pl.BlockSpec((tm, d), lambda j, i: (i, 0)),
            pl.BlockSpec((pl.Element(tn), pl.Element(d)), lambda j, i: w_row(j + 1)),
        ],
        out_specs=pl.BlockSpec((tm, tn), lambda j, i: (i, j)),
        out_shape=jax.ShapeDtypeStruct((m, (n_tiles - 1) * tn), F32),
        scratch_shapes=[pltpu.VMEM((tn, d), BF16)],
        compiler_params=_cparams(("arbitrary", "arbitrary")),
        name="inproj_rest",
    )(hn, wt)
    return proj0, proj_rest, smalls


def _cumsum_matrices():
    rr, cc = np.indices((GATE_ROWS, GATE_ROWS))
    low, upp = cc <= rr, cc >= rr
    same_d = (rr // D_CHUNK) == (cc // D_CHUNK)
    same_m = (rr // M_CHUNK) == (cc // M_CHUNK)
    mats = np.concatenate([
        low & same_m,
        upp & same_m,
        low & same_d,
        upp & same_d,
        same_d,
    ], axis=0)
    return jnp.asarray(mats.astype(np.float32), dtype=BF16)


def _gate_kernel(sm_ref, par_ref, mats_ref, col_ref, row_ref):
    r = GATE_ROWS
    lane = lax.broadcasted_iota(I32, (r, LANES), 1)
    neg_a = -jnp.exp(par_ref[1:2, :])
    xes, x3s = [], []
    for blk in range(GATE_BLOCKS_PER_STEP):
        x = sm_ref[0, blk * r:(blk + 1) * r, :] + par_ref[0:1, :]
        logf = -_softplus(-x)
        g = neg_a * _softplus(x)
        beta = _sigmoid(x)
        xe = jnp.where(lane < CH_F, x,
                       jnp.where(lane < CH_G, logf,
                                 jnp.where(lane < CH_B, g,
                                           jnp.where(lane < CH_T, beta, 0.0))))
        hi = xe.astype(BF16)
        r1 = xe - hi.astype(F32)
        mid = r1.astype(BF16)
        lo = (r1 - mid.astype(F32)).astype(BF16)
        xes.append(xe)
        x3s.append(jnp.concatenate([hi, mid, lo], axis=1))
    y3s = [jnp.dot(mats_ref[...], x3, preferred_element_type=F32) for x3 in x3s]
    half_m = CH_F + M_HEADS
    half_d = CH_G + D_HEADS
    for blk, (xe, y3) in enumerate(zip(xes, y3s)):
        y = y3[:, 0:LANES] + y3[:, LANES:2 * LANES] + y3[:, 2 * LANES:3 * LANES]
        pm, sm_, pd, sd, td = (y[i * r:(i + 1) * r] for i in range(5))
        out = jnp.where(lane < CH_F, xe,
              jnp.where(lane < half_m, pm,
              jnp.where(lane < CH_G, sm_,
              jnp.where(lane < half_d, pd,
              jnp.where(lane < CH_B, sd,
              jnp.where(lane < CH_T, xe,
              jnp.where(lane < CH_T + N_DIR * D_HEADS, pltpu.roll(td, CH_T - CH_G, axis=1), 0.0)))))))
        col_ref[0, blk * r:(blk + 1) * r, :] = out
        row_ref[0, :, blk * r:(blk + 1) * r] = out.T


def _gate_prep(smalls3, params):
    b, s, _ = smalls3.shape
    r = GATE_ROWS * GATE_BLOCKS_PER_STEP
    assert s % r == 0
    return pl.pallas_call(
        _gate_kernel,
        grid=(b, s // r),
        in_specs=[
            pl.BlockSpec((1, r, LANES), lambda i, j: (i, j, 0)),
            pl.BlockSpec((8, LANES), lambda i, j: (0, 0)),
            pl.BlockSpec((5 * GATE_ROWS, GATE_ROWS), lambda i, j: (0, 0)),
        ],
        out_specs=[
            pl.BlockSpec((1, r, LANES), lambda i, j: (i, j, 0)),
            pl.BlockSpec((1, LANES, r), lambda i, j: (i, 0, j)),
        ],
        out_shape=[jax.ShapeDtypeStruct((b, s, LANES), F32), jax.ShapeDtypeStruct((b, LANES, s), F32)],
        compiler_params=_cparams(("parallel", "parallel")),
        name="gateprep",
    )(smalls3, params, _cumsum_matrices())


def _mlstm_kernel(n_cast, q_ref, k_ref, v_ref, o_ref, gc_ref, gr_ref, ng_ref, *refs):
    w_refs, y_ref, wo_refs = refs[:n_cast], refs[n_cast], refs[n_cast + 1:2 * n_cast + 1]
    hn_ref, den_ref, mi_ref, cu_ref, nu_ref, bl_ref, ml_ref, cs_ref, ns_ref, ms_ref, c_ref = refs[2 * n_cast + 1:]
    _cast_blocks(w_refs, wo_refs)
    head = pl.program_id(1)
    s = q_ref.shape[1]
    L = M_CHUNK
    nc = s // L
    scale = M_DK ** -0.5
    rr = lax.broadcasted_iota(I32, (L, L), 0)
    cc = lax.broadcasted_iota(I32, (L, L), 1)

    def bcast8(x11):
        return jnp.broadcast_to(x11, (8, LANES))

    def local(cs):
        chains = []
        for c in cs:
            r0 = pl.multiple_of(c * L, L)
            q = q_ref[0, pl.ds(r0, L), :]
            k = k_ref[0, pl.ds(r0, L), :] * scale
            vb = v_ref[0, pl.ds(r0, L), :].astype(BF16)
            kb = k.astype(BF16)
            qk = lax.dot_general(q.astype(BF16), kb, _NT, preferred_element_type=F32)
            k_t = k.T
            gcol = gc_ref[0, pl.ds(r0, L), :]
            for d in range(N_DIR):
                ch_i = CH_I + d * M_HEADS + head
                ch_f = CH_F + d * M_HEADS + head
                bc_col = _lane_pick(gcol, ch_f)
                i_row = gr_ref[0, pl.ds(ch_i, 1), pl.ds(r0, L)]
                bc_row = gr_ref[0, pl.ds(ch_f, 1), pl.ds(r0, L)]
                a_row = i_row - bc_row
                mask = (cc <= rr) if d == 0 else (cc >= rr)
                log_intra = jnp.where(mask, bc_col + a_row, -jnp.inf)
                m_intra = jnp.max(log_intra, axis=1, keepdims=True)
                p = qk * jnp.exp(log_intra - m_intra)
                b_last = bc_row[:, L - 1:L] if d == 0 else bc_row[:, 0:1]
                log_state = b_last + a_row
                m_loc = jnp.max(log_state, axis=1, keepdims=True)
                w_state = jnp.exp(log_state - m_loc)
                chains.append(dict(
                    idx=d * nc + c, d=d, r0=r0, vb=vb, kb=kb, pb=p.astype(BF16),
                    den=jnp.sum(p, axis=1, keepdims=True), m_intra=m_intra,
                    kw=(k_t * w_state).astype(BF16), w8=jnp.broadcast_to(w_state, (8, L)).astype(BF16),
                    b_last=b_last, m_loc=m_loc))
        for c in chains:
            hn_ref[c["d"], pl.ds(c["r0"], L), :] = jnp.dot(c["pb"], c["vb"], preferred_element_type=F32)
            den_ref[c["d"], pl.ds(c["r0"], L), :] = jnp.broadcast_to(c["den"], (L, LANES))
            mi_ref[c["d"], pl.ds(c["r0"], L), :] = jnp.broadcast_to(c["m_intra"], (L, LANES))
        for c in chains:
            cu_ref[c["idx"]] = jnp.dot(c["kw"], c["vb"], preferred_element_type=F32)
            nu_ref[c["idx"]] = jnp.dot(c["w8"], c["kb"], preferred_element_type=F32)
            bl_ref[c["idx"]] = bcast8(c["b_last"])
            ml_ref[c["idx"]] = bcast8(c["m_loc"])

    def local_trip(i, _):
        local([M_CHUNKS_PER_ITER * i + t for t in range(M_CHUNKS_PER_ITER)])
        return 0

    lax.fori_loop(0, nc // M_CHUNKS_PER_ITER, local_trip, 0)

    c_ref[...] = jnp.zeros(c_ref.shape, F32)

    def scan(j, carry):
        out = []
        for d in range(N_DIR):
            m_st, n_st = carry[d]
            idx = d * nc + (j if d == 0 else nc - 1 - j)
            c_st = c_ref[d]
            cs_ref[idx] = c_st.astype(BF16)
            ns_ref[idx] = n_st
            ms_ref[idx] = m_st
            bl = bl_ref[idx]
            ml = ml_ref[idx]
            m_new = jnp.maximum(bl + m_st, ml)
            decay = jnp.exp(bl + m_st - m_new)
            gain = jnp.exp(ml - m_new)
            c_ref[d] = decay[0:1, 0:1] * c_st + gain[0:1, 0:1] * cu_ref[idx]
            out.append((m_new, decay * n_st + gain * nu_ref[idx]))
        return tuple(out)

    m0 = jnp.full((8, LANES), -jnp.inf, F32)
    n0 = jnp.zeros((8, M_DK), F32)
    lax.fori_loop(0, nc, scan, ((m0, n0), (m0, n0)))

    def wide(x):
        return jnp.concatenate([x] * (M_DV // LANES), axis=1)

    def combine(c, _):
        r0 = pl.multiple_of(c * L, L)
        q = q_ref[0, pl.ds(r0, L), :]
        qb = q.astype(BF16)
        gcol = gc_ref[0, pl.ds(r0, L), :]
        hh = None
        for d in range(N_DIR):
            idx = d * nc + c
            bc = jnp.broadcast_to(_lane_pick(gcol, CH_F + d * M_HEADS + head), (L, LANES))
            den_i = den_ref[d, pl.ds(r0, L), :]
            m_i = mi_ref[d, pl.ds(r0, L), :]
            log_inter = bc + ms_ref[idx][0:1, :]
            m_row = jnp.maximum(log_inter, m_i)
            w_i = jnp.exp(m_i - m_row)
            w_x = jnp.exp(log_inter - m_row)
            q_c = jnp.dot(qb, cs_ref[idx], preferred_element_type=F32)
            q_n = jnp.broadcast_to(jnp.sum(q * ns_ref[idx][0:1, :], axis=1, keepdims=True), (L, LANES))
            num = wide(w_i) * hn_ref[d, pl.ds(r0, L), :] + wide(w_x) * q_c
            den = w_i * den_i + w_x * q_n
            h_d = num * wide(1.0 / jnp.maximum(jnp.abs(den), jnp.exp(-m_row)))
            hh = h_d if hh is None else hh + h_d
        hh = hh * lax.rsqrt(jnp.mean(hh * hh, axis=-1, keepdims=True) + RMS_EPS)
        hh = hh * ng_ref[0]
        y_ref[0, pl.ds(r0, L), :] = (_sigmoid(o_ref[0, pl.ds(r0, L), :]) * hh).astype(y_ref.dtype)
        return 0

    lax.fori_loop(0, nc, combine, 0, unroll=2)


def _mlstm(qkvo, gcol, grow, norm_g, cast_weights):
    (q3, off_q), (k3, off_k), (v3, off_v), (o3, off_o) = qkvo
    b, s, _ = q3.shape
    w_specs, w_shapes = _cast_specs(cast_weights, b * M_HEADS, lambda i, h: i * M_HEADS + h)
    assert s % (M_CHUNK * M_CHUNKS_PER_ITER) == 0
    nc = s // M_CHUNK
    bq, bk = off_q // M_DK, off_k // M_DK
    bv, bo = off_v // M_DV, off_o // M_DV
    return pl.pallas_call(
        functools.partial(_mlstm_kernel, len(cast_weights)),
        grid=(b, M_HEADS),
        in_specs=[
            pl.BlockSpec((1, s, M_DK), lambda i, h: (i, 0, bq + h)),
            pl.BlockSpec((1, s, M_DK), lambda i, h: (i, 0, bk + h)),
            pl.BlockSpec((1, s, M_DV), lambda i, h: (i, 0, bv + h)),
            pl.BlockSpec((1, s, M_DV), lambda i, h: (i, 0, bo + h)),
            pl.BlockSpec((1, s, LANES), lambda i, h: (i, 0, 0)),
            pl.BlockSpec((1, LANES, s), lambda i, h: (i, 0, 0)),
            pl.BlockSpec((1, 1, M_DV), lambda i, h: (h, 0, 0)),
            *w_specs,
        ],
        out_specs=[pl.BlockSpec((1, s, M_DV), lambda i, h: (i, 0, h)), *w_specs],
        out_shape=[jax.ShapeDtypeStruct((b, s, M_HEADS * M_DV), BF16), *w_shapes],
        scratch_shapes=[
            pltpu.VMEM((N_DIR, s, M_DV), F32),
            pltpu.VMEM((N_DIR, s, LANES), F32),
            pltpu.VMEM((N_DIR, s, LANES), F32),
            pltpu.VMEM((N_DIR * nc, M_DK, M_DV), F32),
            pltpu.VMEM((N_DIR * nc, 8, M_DK), F32),
            pltpu.VMEM((N_DIR * nc, 8, LANES), F32),
            pltpu.VMEM((N_DIR * nc, 8, LANES), F32),
            pltpu.VMEM((N_DIR * nc, M_DK, M_DV), BF16),
            pltpu.VMEM((N_DIR * nc, 8, M_DK), F32),
            pltpu.VMEM((N_DIR * nc, 8, LANES), F32),
            pltpu.VMEM((N_DIR, M_DK, M_DV), F32),
        ],
        compiler_params=_cparams(("parallel", "arbitrary")),
        name="mlstm",
    )(q3, k3, v3, o3, gcol, grow, norm_g, *cast_weights)


def _delta_kernel(n_cast, q_ref, k_ref, v_ref, z_ref, cwq_ref, cwk_ref, cwv_ref, gc_ref, gr_ref, ng_ref, *refs):
    w_refs, y_ref, wo_refs = refs[:n_cast], refs[n_cast], refs[n_cast + 1:2 * n_cast + 1]
    xp_ref, qs_ref, ks_ref, vs_ref, kq_ref, n_ref, dec_ref, o_ref, st_ref = refs[2 * n_cast + 1:]
    _cast_blocks(w_refs, wo_refs)
    t = pl.program_id(0)
    head = jnp.minimum(t, pl.num_programs(0) - 2) % D_HEADS
    s = q_ref.shape[1]
    L = D_CHUNK
    G = D_GROUP
    nc = s // L
    ng = s // G
    halo = CONV_HALO
    pad = CONV_W // 2

    @pl.when(t == 0)
    def _():
        for r in (kq_ref, n_ref, dec_ref, o_ref):
            r[...] = jnp.zeros(r.shape, r.dtype)

    xp_ref[0:halo, :] = jnp.zeros((halo, D_DIM), F32)
    xp_ref[s + halo:s + 2 * halo, :] = jnp.zeros((halo, D_DIM), F32)

    def conv_into(x_ref, cw_ref, dst_ref, l2, mult):
        xp_ref[halo:s + halo, :] = x_ref[0]
        for blk in range(ng):
            base = halo + blk * G - pad
            acc = xp_ref[base:base + G, :] * cw_ref[0:1, :]
            for j in range(1, CONV_W):
                acc = acc + xp_ref[base + j:base + j + G, :] * cw_ref[j:j + 1, :]
            y = acc * _sigmoid(acc)
            if l2:
                y = y * (lax.rsqrt(jnp.sum(y * y, axis=-1, keepdims=True) + L2_EPS) * mult)
            dst_ref[blk * G:(blk + 1) * G, :] = y

    conv_into(q_ref, cwq_ref, qs_ref, True, D_DIM ** -0.5)
    conv_into(k_ref, cwk_ref, ks_ref, True, 1.0)
    conv_into(v_ref, cwv_ref, vs_ref, False, 1.0)

    st_ref[...] = jnp.zeros(st_ref.shape, F32)

    def scan_step(scan, j):
        for d in range(N_DIR):
            c = j if d == 0 else nc - 1 - j
            r0 = pl.multiple_of(c * L, L)
            idx = (scan + d) * nc + c
            st = st_ref[d]
            res = jnp.dot(kq_ref[idx], st.astype(BF16), preferred_element_type=F32)
            st_ref[d] = dec_ref[idx][0:1, :] * st + res[0:D_DIM] + n_ref[idx]
            o_ref[scan + d, pl.ds(r0, L), :] = o_ref[scan + d, pl.ds(r0, L), :] + res[D_DIM:D_DIM + L]

    tchunk = lax.broadcasted_iota(I32, (D_DIM, D_GROUP), 1) // D_CHUNK
    rr = lax.broadcasted_iota(I32, (G, G), 0)
    cc = lax.broadcasted_iota(I32, (G, G), 1)
    same = (rr // L) == (cc // L)
    pr = lax.broadcasted_iota(I32, (L, G), 0)
    pc = lax.broadcasted_iota(I32, (L, G), 1)
    eye_p = jnp.where((pc % L) == pr, 1.0, 0.0).astype(F32)

    def pack(m):
        out = m[0:L]
        for i in range(1, D_PER_GROUP):
            out = out + m[i * L:(i + 1) * L]
        return out

    def unpack(p):
        return jnp.where(same, jnp.concatenate([p] * D_PER_GROUP, axis=0), 0.0)

    scan_per_trip = nc // (ng // D_GROUPS_PER_ITER)

    def groups(fill, scan, trip):
        pending = iter(range(scan_per_trip))

        def scan_some(n):
            for _ in range(n):
                j = next(pending, None)
                if j is not None:
                    scan_step(scan, trip * scan_per_trip + j)

        chains = []
        for gi in [D_GROUPS_PER_ITER * trip + g for g in range(D_GROUPS_PER_ITER)]:
            r0 = pl.multiple_of(gi * G, G)
            kk_ = ks_ref[pl.ds(r0, G), :]
            qq_ = qs_ref[pl.ds(r0, G), :]
            vv_ = vs_ref[pl.ds(r0, G), :]
            kb = kk_.astype(BF16)
            kk = lax.dot_general(kb, kb, _NT, preferred_element_type=F32)
            qk = lax.dot_general(qq_.astype(BF16), kb, _NT, preferred_element_type=F32)
            gcol = gc_ref[0, pl.ds(r0, G), :]
            for d in range(N_DIR):
                ch = d * D_HEADS + head
                g_col = _lane_pick(gcol, CH_G + ch)
                b_col = _lane_pick(gcol, CH_B + ch)
                t_col = _lane_pick(gcol, CH_T + ch)
                g_row = gr_ref[0, pl.ds(CH_G + ch, 1), pl.ds(r0, G)]
                tri = (cc <= rr) if d == 0 else (cc >= rr)
                strict = (cc < rr) if d == 0 else (cc > rr)
                gam = jnp.exp(jnp.where(tri, jnp.where(same, g_col - g_row, -jnp.inf), -jnp.inf))
                x = jnp.where(strict, -(b_col * kk * gam), 0.0)
                eg = jnp.exp(g_col)
                chains.append(dict(
                    gi=gi, slot=fill + d, r0=r0, x=x, xp=pack(x), attn=(qk * gam).astype(BF16),
                    rhs=jnp.concatenate([b_col * vv_, (b_col * eg) * kk_], axis=1).astype(BF16),
                    qg=qq_ * eg, kdt=(kk_ * jnp.exp(t_col - g_col)).T, dec=jnp.exp(t_col)))
            scan_some(-(-(scan_per_trip - D_SCAN_STEPS_BETWEEN_STAGES) // D_GROUPS_PER_ITER))
        for c in chains:
            c["tp"] = eye_p + c["xp"]
            c["p"] = _bdot(c["xp"], c["x"])
        scan_some(1)
        for _it in range(4):
            for c in chains:
                res = _bdot(jnp.concatenate([c["tp"], c["p"]], axis=0), unpack(c["p"]))
                c["tp"] = c["tp"] + res[0:L]
                c["p"] = res[L:2 * L]
            scan_some(1)
        for c in chains:
            c["tp"] = c["tp"] + _bdot(c["tp"], unpack(c["p"]))
        scan_some(1)
        for c in chains:
            c["uw"] = _bdot(unpack(c["tp"]), c["rhs"]).astype(BF16)
        scan_some(1)
        for c in chains:
            au = jnp.dot(c["attn"], c["uw"], preferred_element_type=F32)
            o_ref[c["slot"], pl.ds(c["r0"], G), :] = au[:, 0:D_DIM]
            c["qe"] = (c["qg"] - au[:, D_DIM:2 * D_DIM]).astype(BF16)
        scan_some(1)
        for c in chains:
            for ci in range(D_PER_GROUP):
                idx = c["slot"] * nc + c["gi"] * D_PER_GROUP + ci
                ku = _bdot(jnp.where(tchunk == ci, c["kdt"], 0.0), c["uw"])
                n_ref[idx] = ku[:, 0:D_DIM]
                kq_ref[idx, 0:D_DIM, :] = (-ku[:, D_DIM:2 * D_DIM]).astype(BF16)
                kq_ref[idx, D_DIM:D_DIM + L, :] = c["qe"][ci * L:(ci + 1) * L]
                dec_ref[idx] = jnp.broadcast_to(c["dec"][ci * L:ci * L + 8], (8, D_DIM))
        scan_some(scan_per_trip)

    def finish(scan, gi):
        r0 = pl.multiple_of(gi * G, G)
        o = o_ref[scan, pl.ds(r0, G), :] + o_ref[scan + 1, pl.ds(r0, G), :]
        o = o * lax.rsqrt(jnp.mean(o * o, axis=-1, keepdims=True) + RMS_EPS) * ng_ref[...]
        z = z_ref[0, pl.ds(r0, G), :]
        y_ref[0, pl.ds(r0, G), :] = (o * (z * _sigmoid(z))).astype(y_ref.dtype)

    fill = (t % 2) * N_DIR
    scan = N_DIR - fill

    def group_batch(i, carry):
        groups(fill, scan, i)
        return carry

    def finish_batch(gi, carry):
        finish(scan, gi)
        return carry

    lax.fori_loop(0, ng // D_GROUPS_PER_ITER, group_batch, 0)
    lax.fori_loop(0, ng, finish_batch, 0, unroll=4)


def _delta(proj3, conv_w, gcol, grow, norm_g, off_q, off_k, off_v, off_z, cast_weights):
    b, s, _ = proj3.shape
    w = D_HEADS * D_DIM
    nc = s // D_CHUNK
    trips = s // (D_GROUP * D_GROUPS_PER_ITER)
    assert s % (D_GROUP * D_GROUPS_PER_ITER) == 0 and nc % trips == 0
    bq, bk, bv, bz = (o // D_DIM for o in (off_q, off_k, off_v, off_z))
    n_units = b * D_HEADS
    cur = lambda t: jnp.minimum(t, n_units - 1)
    prev = lambda t: jnp.maximum(t - 1, 0)
    col = lambda unit, blk: lambda t: (unit(t) // D_HEADS, 0, blk + unit(t) % D_HEADS)
    w_specs, w_shapes = _cast_specs(cast_weights, n_units, cur)
    return pl.pallas_call(
        functools.partial(_delta_kernel, len(cast_weights)),
        grid=(n_units + 1,),
        in_specs=[
            pl.BlockSpec((1, s, D_DIM), col(cur, bq)),
            pl.BlockSpec((1, s, D_DIM), col(cur, bk)),
            pl.BlockSpec((1, s, D_DIM), col(cur, bv)),
            pl.BlockSpec((1, s, D_DIM), col(prev, bz)),
            pl.BlockSpec((CONV_W, D_DIM), lambda t: (0, cur(t) % D_HEADS)),
            pl.BlockSpec((CONV_W, D_DIM), lambda t: (0, D_HEADS + cur(t) % D_HEADS)),
            pl.BlockSpec((CONV_W, D_DIM), lambda t: (0, 2 * D_HEADS + cur(t) % D_HEADS)),
            pl.BlockSpec((1, s, LANES), lambda t: (cur(t) // D_HEADS, 0, 0)),
            pl.BlockSpec((1, LANES, s), lambda t: (cur(t) // D_HEADS, 0, 0)),
            pl.BlockSpec((1, D_DIM), lambda t: (0, 0)),
            *w_specs,
        ],
        out_specs=[pl.BlockSpec((1, s, D_DIM), col(prev, 0)), *w_specs],
        out_shape=[jax.ShapeDtypeStruct((b, s, w), BF16), *w_shapes],
        scratch_shapes=[
            pltpu.VMEM((s + 2 * CONV_HALO, D_DIM), F32),
            pltpu.VMEM((s, D_DIM), F32),
            pltpu.VMEM((s, D_DIM), F32),
            pltpu.VMEM((s, D_DIM), F32),
            pltpu.VMEM((2 * N_DIR * nc, D_DIM + D_CHUNK, D_DIM), BF16),
            pltpu.VMEM((2 * N_DIR * nc, D_DIM, D_DIM), F32),
            pltpu.VMEM((2 * N_DIR * nc, 8, D_DIM), F32),
            pltpu.VMEM((2 * N_DIR, s, D_DIM), F32),
            pltpu.VMEM((N_DIR, D_DIM, D_DIM), F32),
        ],
        compiler_params=_cparams(("arbitrary",)),
        name="delta",
    )(proj3, proj3, proj3, proj3, conv_w, conv_w, conv_w, gcol, grow, norm_g, *cast_weights)


def _mix_kernel(n_gate_blocks, x_ref, ym_ref, yd_ref, *refs):
    gate_refs, (wbm_ref, wbd_ref, wo_ref, g2_ref, o_ref, h_ref) = refs[:2 * n_gate_blocks], refs[2 * n_gate_blocks:]
    gate = lambda rs: _sigmoid(jnp.concatenate([r[...] for r in rs], axis=1))
    a = jnp.dot(ym_ref[...], wbm_ref[...], preferred_element_type=F32)
    b = jnp.dot(yd_ref[...], wbd_ref[...], preferred_element_type=F32)
    mixed = gate(gate_refs[:n_gate_blocks]) * a + gate(gate_refs[n_gate_blocks:]) * b
    x = x_ref[...] + jnp.dot(mixed.astype(BF16), wo_ref[...], preferred_element_type=F32)
    o_ref[...] = x
    ms = jnp.mean(x * x, axis=-1, keepdims=True)
    h_ref[...] = (x * lax.rsqrt(ms + RMS_EPS) * g2_ref[...]).astype(BF16)


def _mix(x2, ym2, yd2, proj2, off_gates, wbm, wbd, wo, g2, tm):
    m, d = x2.shape
    const = lambda i: (0, 0)
    gw = math.gcd(off_gates, d)
    gate_specs = [pl.BlockSpec((tm, gw), functools.partial(lambda i, c: (i, c), c=(off_gates + t * gw) // gw))
                  for t in range(2 * d // gw)]
    return pl.pallas_call(
        functools.partial(_mix_kernel, d // gw),
        grid=(m // tm,),
        in_specs=[
            pl.BlockSpec((tm, d), lambda i: (i, 0)),
            pl.BlockSpec((tm, ym2.shape[1]), lambda i: (i, 0)),
            pl.BlockSpec((tm, yd2.shape[1]), lambda i: (i, 0)),
            *gate_specs,
            pl.BlockSpec(wbm.shape, const),
            pl.BlockSpec(wbd.shape, const),
            pl.BlockSpec(wo.shape, const),
            pl.BlockSpec((1, d), const),
        ],
        out_specs=[pl.BlockSpec((tm, d), lambda i: (i, 0)), pl.BlockSpec((tm, d), lambda i: (i, 0))],
        out_shape=[jax.ShapeDtypeStruct((m, d), F32), jax.ShapeDtypeStruct((m, d), BF16)],
        compiler_params=_cparams(("parallel",)),
        name="mix",
    )(x2, ym2, yd2, *([proj2] * len(gate_specs)), wbm, wbd, wo, g2)


def _ffn_kernel(n_slices, x_ref, h_ref, gf_ref, w1_ref, w2_ref, o_ref, acc_ref):
    f = pl.program_id(1)
    last = n_slices - 1

    def ffn_part(sl):
        hid = jnp.dot(h_ref[sl, :], w1_ref[...], preferred_element_type=F32)
        act = jnp.square(jnp.maximum(hid, 0.0)).astype(BF16)
        return jnp.dot(act, w2_ref[...], preferred_element_type=F32)

    @pl.when(jnp.logical_and(f == 0, f < last))
    def _():
        acc_ref[...] = ffn_part(slice(None))

    @pl.when(jnp.logical_and(f > 0, f < last))
    def _():
        acc_ref[...] += ffn_part(slice(None))

    @pl.when(f == last)
    def _():
        rows = x_ref.shape[0] // EDGE_STEP_PIECES
        for r in range(EDGE_STEP_PIECES):
            sl = slice(r * rows, (r + 1) * rows)
            y = ffn_part(sl) if last == 0 else acc_ref[sl, :] + ffn_part(sl)
            x = x_ref[sl, :] + y
            ms = jnp.mean(x * x, axis=-1, keepdims=True)
            o_ref[sl, :] = x * lax.rsqrt(ms + RMS_EPS) * gf_ref[...]


def _ffn(x2, h2, gf, w1, w2, tm, tf):
    m, d = x2.shape
    dff = w1.shape[1]
    return pl.pallas_call(
        functools.partial(_ffn_kernel, dff // tf),
        grid=(m // tm, dff // tf),
        in_specs=[
            pl.BlockSpec((tm, d), lambda i, f: (i, 0)),
            pl.BlockSpec((tm, d), lambda i, f: (i, 0)),
            pl.BlockSpec((1, d), lambda i, f: (0, 0)),
            pl.BlockSpec((d, tf), lambda i, f: (0, f)),
            pl.BlockSpec((tf, d), lambda i, f: (f, 0)),
        ],
        out_specs=pl.BlockSpec((tm, d), lambda i, f: (i, 0)),
        out_shape=jax.ShapeDtypeStruct((m, d), F32),
        scratch_shapes=[pltpu.VMEM((tm, d), F32)],
        compiler_params=_cparams(("parallel", "arbitrary")),
        name="ffn",
    )(x2, h2, gf, w1, w2)


def _pick_tile(n, candidates):
    for c in candidates:
        if n % c == 0:
            return c
    raise ValueError(f"no tile in {candidates} divides {n}")


def _layer(x, norm1_g, w_in, i_bias, f_bias, m_norm_g, conv_w, a_log, dt_bias, d_norm_g,
           w_bm, w_bd, w_out, norm2_g, w_ff1, w_ff2, out_g):
    b, s, d = x.shape
    m = b * s
    mw, mqk, dw = M_HEADS * M_DV, M_HEADS * M_DK, D_HEADS * D_DIM
    nm, nd = N_DIR * M_HEADS, N_DIR * D_HEADS
    splits = (mqk, mqk, mw, mw, nm, nm, 3 * dw, dw, nd, nd, 2 * d)
    bounds = [0]
    for w_ in splits:
        bounds.append(bounds[-1] + w_)
    wt = w_in.T
    seg_bounds = [(bounds[0], bounds[4]), (bounds[6], bounds[8]), (bounds[10], bounds[11])]
    small_bounds = [(bounds[4], bounds[6]), (bounds[8], bounds[10])]
    assert small_bounds[0][1] - small_bounds[0][0] == CH_G and sum(hi - lo for lo, hi in small_bounds) == CH_T
    zpad = lambda n: jnp.zeros((n,), F32)
    bias_row = jnp.concatenate([i_bias.reshape(-1), f_bias.reshape(-1), dt_bias.reshape(-1), zpad(LANES - CH_B)])
    alog_row = jnp.concatenate([zpad(CH_G), a_log.reshape(-1), zpad(LANES - CH_B)])
    gate_params = jnp.concatenate([bias_row[None], alog_row[None], jnp.zeros((6, LANES), F32)], axis=0)

    x2 = x.reshape(m, d)
    tm = _pick_tile(m, (512, 256, 128))
    tn = next(c for c in (1024, 512, 256) if all((hi - lo) % c == 0 for lo, hi in seg_bounds))
    proj_first, proj_rest, smalls2 = _inproj(x2, norm1_g.reshape(1, d), wt, seg_bounds, small_bounds,
                             _pick_tile(m, (1024, 512, 256)), _pick_tile(m, (1024, 512, 256)), tn)
    first3, rest3 = proj_first.reshape(b, s, -1), proj_rest.reshape(b, s, -1)
    col = lambda c: (first3, c) if c < tn else (rest3, c - tn)
    gcol, grow = _gate_prep(smalls2.reshape(b, s, LANES), gate_params)

    off_d = 2 * mqk + 2 * mw
    off_g = off_d + 4 * dw
    ym, w_bm16, w_bd16, w_out16 = _mlstm([col(0), col(mqk), col(2 * mqk), col(2 * mqk + mw)], gcol, grow,
                                         m_norm_g.reshape(M_HEADS, 1, M_DV), (w_bm, w_bd, w_out))
    off_d, off_g = off_d - tn, off_g - tn
    yd, w_ff1_16, w_ff2_16 = _delta(rest3, conv_w, gcol, grow, d_norm_g.reshape(1, D_DIM),
                                    off_d, off_d + dw, off_d + 2 * dw, off_d + 3 * dw, (w_ff1, w_ff2))

    x1, h2 = _mix(x2, ym.reshape(m, mw), yd.reshape(m, dw), proj_rest, off_g, w_bm16, w_bd16, w_out16,
                  norm2_g.reshape(1, d), _pick_tile(m, (256, 128)))
    tf = _pick_tile(w_ff1.shape[1], (1024, 512, 256))
    out = _ffn(x1, h2, out_g.reshape(1, d), w_ff1_16, w_ff2_16, tm, tf)
    return out.reshape(b, s, d)


def kernel(x, norm1_g, w_in, mlstm_i_bias, mlstm_f_bias, mlstm_norm_g, delta_conv_w, delta_a_log,
           delta_dt_bias, delta_norm_g, w_branch_m, w_branch_d, w_out, norm2_g, w_ff1, w_ff2, norm_f_g):
    depth = w_in.shape[0]
    assert depth == 1, "the fused FFN + final-norm epilogue assumes a single layer"
    return _layer(x, norm1_g[0], w_in[0], mlstm_i_bias[0], mlstm_f_bias[0], mlstm_norm_g[0], delta_conv_w[0],
                  delta_a_log[0], delta_dt_bias[0], delta_norm_g[0], w_branch_m[0], w_branch_d[0], w_out[0],
                  norm2_g[0], w_ff1[0], w_ff2[0], norm_f_g)
```
